```python
import jax, jax.numpy as jnp
from jax import lax
import numpy as np

D_MODEL = 1024
BATCH = 8
SEQ = 2048
DEPTH = 4
DEC_BATCH = 128
DEC_SEQ = 8
PAST_LEN = 16384
PAGE_SIZE = 128

D_MIX = D_MODEL
D_POOL = D_MIX // 2
D_RET = D_MIX - D_POOL
POOL_WINDOWS = (2, 4, 8, 16)
POOL_GROUPS = len(POOL_WINDOWS)
POOL_CG = D_POOL // POOL_GROUPS
POOL_BUF = max(POOL_WINDOWS) - 1
RET_HEADS = 4
RET_DK = D_RET // RET_HEADS
RET_DV = D_RET // RET_HEADS
RET_CHUNK = 128
ROPE_BASE = 10000.0
D_IN = D_POOL + 3 * D_RET + D_MIX
EPS = 1e-6

kernel_name = 'hymba_pool_retention_decode_step'


def _rmsnorm(x, g):
    xf = x.astype(jnp.float32)
    y = xf * lax.rsqrt(jnp.mean(xf * xf, axis=-1, keepdims=True) + EPS)
    if g is not None:
        y = y * g.astype(jnp.float32)
    return y.astype(x.dtype)


def _rope(x, pos):
    half = x.shape[-1] // 2
    inv = 1.0 / (ROPE_BASE ** (jnp.arange(half, dtype=jnp.float32) / half))
    ang = pos.astype(jnp.float32)[:, None] * inv[None, :]
    cos = jnp.cos(ang)[None, :, None, :]
    sin = jnp.sin(ang)[None, :, None, :]
    xf = x.astype(jnp.float32)
    x1, x2 = xf[..., :half], xf[..., half:]
    return jnp.concatenate([x1 * cos - x2 * sin, x1 * sin + x2 * cos], axis=-1)


def _log_gamma():
    h = jnp.arange(RET_HEADS, dtype=jnp.float32)
    return jnp.log(1.0 - jnp.exp2(-5.0 - h))


def _pool_mixer(u_full, n_new, w_pool, scale):
    B, Lf, _ = u_full.shape
    uf = u_full.astype(jnp.float32).reshape(B, Lf, POOL_GROUPS, POOL_CG)
    P = jnp.concatenate([jnp.zeros((B, 1, POOL_GROUPS, POOL_CG), jnp.float32),
                         jnp.cumsum(uf, axis=1)], axis=1)
    t = jnp.arange(Lf - n_new, Lf)
    outs = []
    for g, w in enumerate(POOL_WINDOWS):
        lo = jnp.maximum(t + 1 - w, 0)
        s = P[:, t + 1, g] - P[:, lo, g]
        cnt = (t + 1 - lo).astype(jnp.float32)[None, :, None]
        outs.append(s / cnt - uf[:, t, g])
    m = jnp.stack(outs, axis=2)
    y = jnp.einsum('bngc,gcd->bngd', m, w_pool.astype(jnp.float32))
    y = y.reshape(B, n_new, D_POOL) * scale.astype(jnp.float32)
    return y


def _retention(q, k, v, S0):
    B, L, H, DK = q.shape
    DV = v.shape[-1]
    C = RET_CHUNK if L % RET_CHUNK == 0 else L
    n = L // C
    lg = _log_gamma()
    qc = q.reshape(B, n, C, H, DK)
    kc = k.reshape(B, n, C, H, DK)
    vc = v.astype(jnp.float32).reshape(B, n, C, H, DV)
    idx = jnp.arange(C)
    diff = idx[:, None] - idx[None, :]
    dmask = jnp.where((diff >= 0)[:, :, None],
                      jnp.exp(jnp.maximum(diff, 0).astype(jnp.float32)[:, :, None] * lg), 0.0)
    scores = jnp.einsum('bnihd,bnjhd->bnhij', qc, kc) * jnp.transpose(dmask, (2, 0, 1))
    intra = jnp.einsum('bnhij,bnjhe->bnihe', scores, vc)
    kdec = jnp.exp((C - 1 - idx).astype(jnp.float32)[:, None] * lg)
    kv = jnp.einsum('bnjhd,jh,bnjhe->bnhde', kc, kdec, vc)
    gC = jnp.exp(C * lg)

    def step(S, kv_c):
        return gC[None, :, None, None] * S + kv_c, S

    S_fin, S_prev = lax.scan(step, S0.astype(jnp.float32), jnp.moveaxis(kv, 1, 0))
    S_prev = jnp.moveaxis(S_prev, 0, 1)
    qdec = jnp.exp((idx + 1).astype(jnp.float32)[:, None] * lg)
    cross = jnp.einsum('bnihd,ih,bnhde->bnihe', qc, qdec, S_prev)
    o = (intra + cross).reshape(B, L, H, DV)
    return o, S_fin


def _layer(x, c, pos, S0, pool_buf, w_ada, b_ada, g_pre, g_post, w_in, w_pool, pool_scale, w_o):
    B, L, _ = x.shape
    mod = jax.nn.silu(c.astype(jnp.float32)) @ w_ada.astype(jnp.float32) + b_ada.astype(jnp.float32)
    shift, scl, gate_res = jnp.split(mod, 3, axis=-1)
    h = _rmsnorm(x, g_pre).astype(jnp.float32) * (1.0 + scl[:, None]) + shift[:, None]
    z = (h.astype(x.dtype) @ w_in).astype(jnp.float32)
    o1 = D_POOL
    o2 = o1 + D_RET
    o3 = o2 + D_RET
    o4 = o3 + D_RET
    u, q, k, v, gate = z[..., :o1], z[..., o1:o2], z[..., o2:o3], z[..., o3:o4], z[..., o4:]
    if pool_buf is None:
        u_full = u
    else:
        u_full = jnp.concatenate([pool_buf.astype(jnp.float32), u], axis=1)
    pool_out = _pool_mixer(u_full, L, w_pool, pool_scale)
    new_buf = u_full[:, -POOL_BUF:]
    q = _rope(q.reshape(B, L, RET_HEADS, RET_DK), pos)
    k = _rope(k.reshape(B, L, RET_HEADS, RET_DK), pos) * (RET_DK ** -0.5)
    v = v.reshape(B, L, RET_HEADS, RET_DV)
    o, S_new = _retention(q, k, v, S0)
    o = _rmsnorm(o, None).reshape(B, L, D_RET)
    mix = jnp.concatenate([pool_out, o], axis=-1) * jax.nn.silu(gate)
    y = mix.astype(x.dtype) @ w_o
    y = _rmsnorm(y, g_post).astype(jnp.float32)
    x_out = (x.astype(jnp.float32) + gate_res[:, None] * y).astype(x.dtype)
    return x_out, S_new, new_buf


def setup_inputs(seed: int = 0) -> dict:
    key = jax.random.key(seed)
    ks = jax.random.split(key, 16)
    f = jnp.float32
    x_prompt = jax.random.normal(ks[0], (BATCH, SEQ, D_MODEL), f)
    x_sample = jax.random.normal(ks[1], (DEC_BATCH, DEC_SEQ, D_MODEL), f)
    c_prompt = jax.random.normal(ks[2], (BATCH, D_MODEL), f)
    c_sample = jax.random.normal(ks[3], (DEC_BATCH, D_MODEL), f)
    state_ret = 0.1 * jax.random.normal(ks[4], (DEPTH, DEC_BATCH, RET_HEADS, RET_DK, RET_DV), f)
    state_pool = jax.random.normal(ks[5], (DEPTH, DEC_BATCH, POOL_BUF, D_POOL), f)
    w_ada = 0.5 * D_MODEL ** -0.5 * jax.random.normal(ks[6], (DEPTH, D_MODEL, 3 * D_MODEL), f)
    b_ada = 0.02 * jax.random.normal(ks[7], (DEPTH, 3 * D_MODEL), f)
    g_pre = 1.0 + 0.05 * jax.random.normal(ks[8], (DEPTH, D_MODEL), f)
    g_post = 1.0 + 0.05 * jax.random.normal(ks[9], (DEPTH, D_MODEL), f)
    w_in = D_MODEL ** -0.5 * jax.random.normal(ks[10], (DEPTH, D_MODEL, D_IN), f)
    w_pool = POOL_CG ** -0.5 * jax.random.normal(ks[11], (DEPTH, POOL_GROUPS, POOL_CG, POOL_CG), f)
    pool_scale = 1.0 + 0.1 * jax.random.normal(ks[12], (DEPTH, D_POOL), f)
    w_o = D_MIX ** -0.5 * jax.random.normal(ks[13], (DEPTH, D_MIX, D_MODEL), f)
    return {'x_prompt': x_prompt, 'x_sample': x_sample, 'c_prompt': c_prompt, 'c_sample': c_sample,
            'state_ret': state_ret, 'state_pool': state_pool, 'w_ada': w_ada, 'b_ada': b_ada,
            'g_pre': g_pre, 'g_post': g_post, 'w_in': w_in, 'w_pool': w_pool,
            'pool_scale': pool_scale, 'w_o': w_o}


def reference(x_prompt, x_sample, c_prompt, c_sample, state_ret, state_pool, w_ada, b_ada,
              g_pre, g_post, w_in, w_pool, pool_scale, w_o):
    pos_p = jnp.arange(x_prompt.shape[1])
    pos_s = PAST_LEN + jnp.arange(x_sample.shape[1])
    Bp = x_prompt.shape[0]
    hp, hs = x_prompt, x_sample
    ret_p, pool_p, ret_s, pool_s = [], [], [], []
    for l in range(DEPTH):
        S0 = jnp.zeros((Bp, RET_HEADS, RET_DK, RET_DV), jnp.float32)
        hp, Sp, bp = _layer(hp, c_prompt, pos_p, S0, None, w_ada[l], b_ada[l], g_pre[l], g_post[l],
                            w_in[l], w_pool[l], pool_scale[l], w_o[l])
        hs, Ss, bs = _layer(hs, c_sample, pos_s, state_ret[l], state_pool[l], w_ada[l], b_ada[l],
                            g_pre[l], g_post[l], w_in[l], w_pool[l], pool_scale[l], w_o[l])
        ret_p.append(Sp.astype(x_prompt.dtype))
        pool_p.append(bp.astype(x_prompt.dtype))
        ret_s.append(Ss.astype(state_ret.dtype))
        pool_s.append(bs.astype(state_pool.dtype))
    ret_prompt = jnp.stack(ret_p, axis=0)
    pool_prompt = jnp.stack(pool_p, axis=0)
    ret_sample = jnp.stack(ret_s, axis=0)
    pool_sample = jnp.stack(pool_s, axis=0)
    return (hp, hs, ret_prompt, pool_prompt, ret_sample, pool_sample)
```

```python
import functools

import jax
import jax.numpy as jnp
from jax import lax
from jax.experimental import pallas as pl
from jax.experimental.pallas import tpu as pltpu

F32 = jnp.float32
BF16 = jnp.bfloat16

D_MODEL = 1024
D_POOL = 512
D_RET = 512
POOL_WINDOWS = (2, 4, 8, 16)
POOL_CG = 128
HIST = 16
RET_HEADS = 4
RET_DK = 128
RET_CHUNK = 128
ROPE_BASE = 10000.0
D_IN = D_POOL + 3 * D_RET + D_MODEL
EPS = 1e-6
K_SCALE = RET_DK ** -0.5

PROMPT_TILE = 512
SAMPLE_BT = 16
ADA_COLS = 768
VMEM_LIMIT = 56 * 1024 * 1024


def _const_spec(shape):
    return pl.BlockSpec(shape, lambda *_: (0,) * len(shape))


def _decay_tables(lg_lane, lg_row, c, dmask_ref, qdec_ref, kdec_ref, gc_ref):
    idx = lax.broadcasted_iota(jnp.int32, (c, D_RET), 0).astype(F32)
    qdec_ref[...] = jnp.exp((idx + 1.0) * lg_lane)
    kdec_ref[...] = jnp.exp((c - 1.0 - idx) * lg_lane)
    gc_ref[...] = jnp.exp(jnp.full((8, D_RET), float(c), F32) * lg_lane)
    ii = lax.broadcasted_iota(jnp.int32, (c, c), 0)
    jj = lax.broadcasted_iota(jnp.int32, (c, c), 1)
    diff = ii - jj
    for h in range(RET_HEADS):
        dec = jnp.exp(jnp.maximum(diff, 0).astype(F32) * lg_row[h])
        dmask_ref[h] = jnp.where(diff >= 0, dec, 0.0)


def _tables_kernel(inv_ref, csp_ref, snp_ref, css_ref, sns_ref,
                   dmp_ref, qdp_ref, kdp_ref, gcp_ref,
                   dms_ref, qds_ref, kds_ref, gcs_ref, *, seq, dec_seq, past_len):
    inv = inv_ref[...]
    lane = lax.broadcasted_iota(jnp.int32, (1, 2 * 64), 1)
    sign = jnp.where(lane < 64, -1.0, 1.0).astype(F32)

    def rope(n, start, cs_ref, sn_ref):
        pos = (lax.broadcasted_iota(jnp.int32, (n, 128), 0) + start).astype(F32)
        ang = pos * inv
        cs_ref[...] = jnp.cos(ang)
        sn_ref[...] = jnp.sin(ang) * sign

    rope(seq, 0, csp_ref, snp_ref)
    rope(dec_seq, past_len, css_ref, sns_ref)

    head_lane = (lax.broadcasted_iota(jnp.int32, (1, D_RET), 1) // RET_DK).astype(F32)
    lg_lane = jnp.log(1.0 - jnp.exp2(-5.0 - head_lane))
    for c, refs in ((RET_CHUNK, (dmp_ref, qdp_ref, kdp_ref, gcp_ref)),
                    (dec_seq, (dms_ref, qds_ref, kds_ref, gcs_ref))):
        lg_row = [jnp.log(1.0 - jnp.exp2(jnp.full((1, c), -5.0 - h, F32))) for h in range(RET_HEADS)]
        _decay_tables(lg_lane, lg_row, c, *refs)


def _make_tables(seq, dec_seq, past_len):
    half = RET_DK // 2
    inv = 1.0 / (ROPE_BASE ** (jnp.arange(half, dtype=F32) / half))
    inv2 = jnp.concatenate([inv, inv])[None, :]
    c, cs = RET_CHUNK, dec_seq
    out_shape = (
        jax.ShapeDtypeStruct((seq, 128), F32), jax.ShapeDtypeStruct((seq, 128), F32),
        jax.ShapeDtypeStruct((dec_seq, 128), F32), jax.ShapeDtypeStruct((dec_seq, 128), F32),
        jax.ShapeDtypeStruct((RET_HEADS, c, c), F32), jax.ShapeDtypeStruct((c, D_RET), F32),
        jax.ShapeDtypeStruct((c, D_RET), F32), jax.ShapeDtypeStruct((8, D_RET), F32),
        jax.ShapeDtypeStruct((RET_HEADS, cs, cs), F32), jax.ShapeDtypeStruct((cs, D_RET), F32),
        jax.ShapeDtypeStruct((cs, D_RET), F32), jax.ShapeDtypeStruct((8, D_RET), F32),
    )
    return pl.pallas_call(
        functools.partial(_tables_kernel, seq=seq, dec_seq=dec_seq, past_len=past_len),
        out_shape=out_shape,
        name="tables",
    )(inv2)


def _ada_kernel(c_ref, w_ref, b_ref, o_ref):
    c = c_ref[...]
    a = (c * jax.nn.sigmoid(c)).astype(BF16)
    o_ref[0] = jnp.dot(a, w_ref[0].astype(BF16), preferred_element_type=F32) + b_ref[0]


def _ada_mod(c_all, w_ada, b_ada):
    depth, d, d3 = w_ada.shape
    n = c_all.shape[0]
    return pl.pallas_call(
        _ada_kernel,
        grid=(depth, d3 // ADA_COLS),
        in_specs=[
            pl.BlockSpec((n, d), lambda l, j: (0, 0)),
            pl.BlockSpec((1, d, ADA_COLS), lambda l, j: (l, 0, j)),
            pl.BlockSpec((1, 1, ADA_COLS), lambda l, j: (l, 0, j)),
        ],
        out_specs=pl.BlockSpec((1, n, ADA_COLS), lambda l, j: (l, 0, j)),
        out_shape=jax.ShapeDtypeStruct((depth, n, d3), F32),
        compiler_params=pltpu.CompilerParams(dimension_semantics=("arbitrary", "arbitrary")),
        name="ada_mod",
    )(c_all, w_ada, b_ada.reshape(depth, 1, d3))


def _rope(x, cs, sn):
    return x * cs + pltpu.roll(x, RET_DK // 2, x.ndim - 1) * sn


def _rms(x):
    return x * lax.rsqrt(jnp.mean(x * x, axis=-1, keepdims=True) + EPS)


def _prompt_kernel(x_ref, mod_ref, gpre_ref, gpost_ref, win_ref, wpool_ref, pscale_ref, wo_ref,
                   cs_ref, sn_ref, dmask_ref, qdec_ref, kdec_ref, gc_ref,
                   xo_ref, s_ref, nb_ref,
                   hb_ref, ubuf_ref, q_ref, qd_ref, k_ref, kd_ref, v_ref, mix_ref, *, tile):
    t = pl.program_id(1)
    nt = pl.num_programs(1)
    n_chunks = tile // RET_CHUNK

    @pl.when(t == 0)
    def _():
        s_ref[...] = jnp.zeros_like(s_ref)
        ubuf_ref[0:HIST, :] = jnp.zeros((HIST, D_POOL), F32)

    m = mod_ref[0]
    shift, scl, gres = m[:, 0:D_MODEL], m[:, D_MODEL:2 * D_MODEL], m[:, 2 * D_MODEL:]

    h = _rms(x_ref[0]) * gpre_ref[...]
    h = h * (1.0 + scl) + shift
    hb_ref[...] = h.astype(BF16)

    ubuf_ref[HIST:HIST + tile, :] = jnp.dot(hb_ref[...], win_ref[:, 0:D_POOL], preferred_element_type=F32)
    row = lax.broadcasted_iota(jnp.int32, (tile, 1), 0) + t * tile
    for g, w in enumerate(POOL_WINDOWS):
        lo, hi = g * POOL_CG, (g + 1) * POOL_CG
        u_g = ubuf_ref[HIST:HIST + tile, lo:hi]
        acc = u_g
        for k in range(1, w):
            acc = acc + ubuf_ref[HIST - k:HIST - k + tile, lo:hi]
        cnt = jnp.minimum(row + 1, w).astype(F32)
        m_g = acc / cnt - u_g
        y_g = jnp.dot(m_g.astype(BF16), wpool_ref[g], preferred_element_type=F32)
        mix_ref[:, lo:hi] = y_g * pscale_ref[:, lo:hi]

    @pl.when(t == nt - 1)
    def _():
        nb_ref[0] = ubuf_ref[tile:tile + HIST, :]

    ubuf_ref[0:HIST, :] = ubuf_ref[tile:tile + HIST, :]

    o1, o2, o3, o4 = D_POOL, D_POOL + D_RET, D_POOL + 2 * D_RET, D_POOL + 3 * D_RET
    zq = jnp.dot(hb_ref[...], win_ref[:, o1:o2], preferred_element_type=F32)
    zk = jnp.dot(hb_ref[...], win_ref[:, o2:o3], preferred_element_type=F32)
    cs, sn = cs_ref[...], sn_ref[...]
    for hh in range(RET_HEADS):
        sl = slice(hh * RET_DK, (hh + 1) * RET_DK)
        qr = _rope(zq[:, sl], cs, sn)
        kr = _rope(zk[:, sl], cs, sn) * K_SCALE
        q_ref[:, sl] = qr.astype(BF16)
        k_ref[:, sl] = kr.astype(BF16)
        for c in range(n_chunks):
            rs = slice(c * RET_CHUNK, (c + 1) * RET_CHUNK)
            qd_ref[rs, sl] = (qr[rs] * qdec_ref[:, sl]).astype(BF16)
            kd_ref[rs, sl] = (kr[rs] * kdec_ref[:, sl]).astype(BF16)
    v_ref[...] = jnp.dot(hb_ref[...], win_ref[:, o3:o4], preferred_element_type=F32).astype(BF16)

    for c in range(n_chunks):
        rs = slice(c * RET_CHUNK, (c + 1) * RET_CHUNK)
        for hh in range(RET_HEADS):
            sl = slice(hh * RET_DK, (hh + 1) * RET_DK)
            s_prev = s_ref[0, hh]
            v_h = v_ref[rs, sl]
            scores = lax.dot_general(q_ref[rs, sl], k_ref[rs, sl], (((1,), (1,)), ((), ())),
                                     preferred_element_type=F32) * dmask_ref[hh]
            lhs = jnp.concatenate([scores.astype(BF16), qd_ref[rs, sl]], axis=1)
            rhs = jnp.concatenate([v_h, s_prev.astype(BF16)], axis=0)
            o_h = jnp.dot(lhs, rhs, preferred_element_type=F32)
            kv = lax.dot_general(kd_ref[rs, sl], v_h, (((0,), (0,)), ((), ())),
                                 preferred_element_type=F32)
            s_ref[0, hh] = gc_ref[0:1, sl] * s_prev + kv
            mix_ref[rs, D_POOL + hh * RET_DK:D_POOL + (hh + 1) * RET_DK] = _rms(o_h)

    zg = jnp.dot(hb_ref[...], win_ref[:, o4:], preferred_element_type=F32)
    mixb = (mix_ref[...] * (zg * jax.nn.sigmoid(zg))).astype(BF16)
    y = jnp.dot(mixb, wo_ref[...], preferred_element_type=F32)
    xo_ref[0] = x_ref[0] + gres * (_rms(y) * gpost_ref[...])


def _prompt_layer(l, x, mod, g_pre, g_post, w_in, w_pool, pool_scale, w_o, tabs):
    b, seq, d = x.shape
    tile = PROMPT_TILE
    cs, sn, dmask, qdec, kdec, gc = tabs
    layer = lambda *_: (l, 0, 0)
    kern = functools.partial(_prompt_kernel, tile=tile)
    return pl.pallas_call(
        kern,
        grid=(b, seq // tile),
        in_specs=[
            pl.BlockSpec((1, tile, d), lambda i, t: (i, t, 0)),
            pl.BlockSpec((None, 1, 1, 3 * d), lambda i, t: (l, i, 0, 0)),
            pl.BlockSpec((None, 1, d), layer),
            pl.BlockSpec((None, 1, d), layer),
            pl.BlockSpec((None, d, D_IN), layer),
            pl.BlockSpec((None, len(POOL_WINDOWS), POOL_CG, POOL_CG), lambda *_: (l, 0, 0, 0)),
            pl.BlockSpec((None, 1, D_POOL), layer),
            pl.BlockSpec((None, d, d), layer),
            pl.BlockSpec((tile, 128), lambda i, t: (t, 0)),
            pl.BlockSpec((tile, 128), lambda i, t: (t, 0)),
            _const_spec(dmask.shape), _const_spec(qdec.shape), _const_spec(kdec.shape), _const_spec(gc.shape),
        ],
        out_specs=[
            pl.BlockSpec((1, tile, d), lambda i, t: (i, t, 0)),
            pl.BlockSpec((1, RET_HEADS, RET_DK, RET_DK), lambda i, t: (i, 0, 0, 0)),
            pl.BlockSpec((1, HIST, D_POOL), lambda i, t: (i, 0, 0)),
        ],
        out_shape=[
            jax.ShapeDtypeStruct((b, seq, d), F32),
            jax.ShapeDtypeStruct((b, RET_HEADS, RET_DK, RET_DK), F32),
            jax.ShapeDtypeStruct((b, HIST, D_POOL), F32),
        ],
        scratch_shapes=[
            pltpu.VMEM((tile, d), BF16),
            pltpu.VMEM((HIST + tile, D_POOL), F32),
            pltpu.VMEM((tile, D_RET), BF16),
            pltpu.VMEM((tile, D_RET), BF16),
            pltpu.VMEM((tile, D_RET), BF16),
            pltpu.VMEM((tile, D_RET), BF16),
            pltpu.VMEM((tile, D_RET), BF16),
            pltpu.VMEM((tile, d), F32),
        ],
        compiler_params=pltpu.CompilerParams(
            dimension_semantics=("arbitrary", "arbitrary"), vmem_limit_bytes=VMEM_LIMIT),
        name=f"prompt_layer{l}",
    )(x, mod, g_pre, g_post, w_in, w_pool, pool_scale, w_o, cs, sn, dmask, qdec, kdec, gc)


def _sample_kernel(x_ref, mod_ref, gpre_ref, gpost_ref, win_ref, wpool_ref, pscale_ref, wo_ref,
                   cs_ref, sn_ref, dmask_ref, qdec_ref, kdec_ref, gc_ref, s0_ref, hist_ref,
                   xo_ref, s_ref, nb_ref,
                   ubuf_ref, mix_ref, *, bt, dec_seq):
    n = bt * dec_seq
    m = mod_ref[...]
    shift, scl, gres = m[:, :, 0:D_MODEL], m[:, :, D_MODEL:2 * D_MODEL], m[:, :, 2 * D_MODEL:]

    x3 = x_ref[...]
    h3 = _rms(x3) * gpre_ref[...][None]
    h3 = h3 * (1.0 + scl) + shift
    hb = h3.reshape(n, D_MODEL).astype(BF16)

    u = jnp.dot(hb, win_ref[:, 0:D_POOL], preferred_element_type=F32)
    ubuf_ref[:, 0:HIST, :] = hist_ref[...]
    ubuf_ref[:, HIST:HIST + dec_seq, :] = u.reshape(bt, dec_seq, D_POOL)
    for g, w in enumerate(POOL_WINDOWS):
        lo, hi = g * POOL_CG, (g + 1) * POOL_CG
        u_g = ubuf_ref[:, HIST:HIST + dec_seq, lo:hi]
        acc = u_g
        for k in range(1, w):
            acc = acc + ubuf_ref[:, HIST - k:HIST - k + dec_seq, lo:hi]
        m_g = (acc / float(w) - u_g).reshape(n, POOL_CG)
        y_g = jnp.dot(m_g.astype(BF16), wpool_ref[g], preferred_element_type=F32)
        mix_ref[:, lo:hi] = y_g * pscale_ref[:, lo:hi]
    nb_ref[...] = ubuf_ref[:, dec_seq:dec_seq + HIST, :]

    o1, o2, o3, o4 = D_POOL, D_POOL + D_RET, D_POOL + 2 * D_RET, D_POOL + 3 * D_RET
    zq = jnp.dot(hb, win_ref[:, o1:o2], preferred_element_type=F32).reshape(bt, dec_seq, D_RET)
    zk = jnp.dot(hb, win_ref[:, o2:o3], preferred_element_type=F32).reshape(bt, dec_seq, D_RET)
    zv = jnp.dot(hb, win_ref[:, o3:o4], preferred_element_type=F32).reshape(bt, dec_seq, D_RET)
    cs, sn = cs_ref[...][None], sn_ref[...][None]
    for hh in range(RET_HEADS):
        sl = slice(hh * RET_DK, (hh + 1) * RET_DK)
        qr = _rope(zq[:, :, sl], cs, sn)
        kr = _rope(zk[:, :, sl], cs, sn) * K_SCALE
        v_h = zv[:, :, sl]
        s_prev = s0_ref[:, hh]
        scores = jnp.einsum("bid,bjd->bij", qr, kr, preferred_element_type=F32) * dmask_ref[hh][None]
        intra = jnp.einsum("bij,bje->bie", scores, v_h, preferred_element_type=F32)
        cross = jnp.einsum("bid,bde->bie", qr * qdec_ref[:, sl][None], s_prev, preferred_element_type=F32)
        kv = jnp.einsum("bjd,bje->bde", kr * kdec_ref[:, sl][None], v_h, preferred_element_type=F32)
        s_ref[:, hh] = gc_ref[0:1, sl][None] * s_prev + kv
        mix_ref[:, D_POOL + hh * RET_DK:D_POOL + (hh + 1) * RET_DK] = _rms(intra + cross).reshape(n, RET_DK)

    zg = jnp.dot(hb, win_ref[:, o4:], preferred_element_type=F32)
    mixb = (mix_ref[...] * (zg * jax.nn.sigmoid(zg))).astype(BF16)
    y = jnp.dot(mixb, wo_ref[...], preferred_element_type=F32)
    yn = (_rms(y) * gpost_ref[...]).reshape(bt, dec_seq, D_MODEL)
    xo_ref[...] = x_ref[...] + gres * yn


def _sample_layer(l, x, mod, g_pre, g_post, w_in, w_pool, pool_scale, w_o, tabs, state_ret, hist):
    b, dec_seq, d = x.shape
    bt = SAMPLE_BT
    cs, sn, dmask, qdec, kdec, gc = tabs
    layer = lambda *_: (l, 0, 0)
    kern = functools.partial(_sample_kernel, bt=bt, dec_seq=dec_seq)
    return pl.pallas_call(
        kern,
        grid=(b // bt,),
        in_specs=[
            pl.BlockSpec((bt, dec_seq, d), lambda i: (i, 0, 0)),
            pl.BlockSpec((None, bt, 1, 3 * d), lambda i: (l, i, 0, 0)),
            pl.BlockSpec((None, 1, d), layer),
            pl.BlockSpec((None, 1, d), layer),
            pl.BlockSpec((None, d, D_IN), layer),
            pl.BlockSpec((None, len(POOL_WINDOWS), POOL_CG, POOL_CG), lambda *_: (l, 0, 0, 0)),
            pl.BlockSpec((None, 1, D_POOL), layer),
            pl.BlockSpec((None, d, d), layer),
            _const_spec(cs.shape), _const_spec(sn.shape),
            _const_spec(dmask.shape), _const_spec(qdec.shape), _const_spec(kdec.shape), _const_spec(gc.shape),
            pl.BlockSpec((None, bt, RET_HEADS, RET_DK, RET_DK), lambda i: (l, i, 0, 0, 0)),
            pl.BlockSpec((None, bt, HIST, D_POOL), lambda i: (l, i, 0, 0)),
        ],
        out_specs=[
            pl.BlockSpec((bt, dec_seq, d), lambda i: (i, 0, 0)),
            pl.BlockSpec((bt, RET_HEADS, RET_DK, RET_DK), lambda i: (i, 0, 0, 0)),
            pl.BlockSpec((bt, HIST, D_POOL), lambda i: (i, 0, 0)),
        ],
        out_shape=[
            jax.ShapeDtypeStruct((b, dec_seq, d), F32),
            jax.ShapeDtypeStruct((b, RET_HEADS, RET_DK, RET_DK), F32),
            jax.ShapeDtypeStruct((b, HIST, D_POOL), F32),
        ],
        scratch_shapes=[
            pltpu.VMEM((bt, HIST + dec_seq, D_POOL), F32),
            pltpu.VMEM((bt * dec_seq, d), F32),
        ],
        compiler_params=pltpu.CompilerParams(
            dimension_semantics=("arbitrary",), vmem_limit_bytes=VMEM_LIMIT),
        name=f"sample_layer{l}",
    )(x, mod, g_pre, g_post, w_in, w_pool, pool_scale, w_o, cs, sn, dmask, qdec, kdec, gc, state_ret, hist)


def kernel(x_prompt, x_sample, c_prompt, c_sample, state_ret, state_pool, w_ada, b_ada,
           g_pre, g_post, w_in, w_pool, pool_scale, w_o):
    depth = w_in.shape[0]
    bp, seq, d = x_prompt.shape
    bs, dec_seq, _ = x_sample.shape
    past_len = 16384
    assert d == D_MODEL and seq % PROMPT_TILE == 0 and bs % SAMPLE_BT == 0 and dec_seq == 8

    (csp, snp, css, sns, dmp, qdp, kdp, gcp, dms, qds, kds, gcs) = _make_tables(seq, dec_seq, past_len)
    tabs_p = (csp, snp, dmp, qdp, kdp, gcp)
    tabs_s = (css, sns, dms, qds, kds, gcs)

    mod = _ada_mod(jnp.concatenate([c_prompt, c_sample], axis=0), w_ada, b_ada)
    mod_p = mod[:, :bp].reshape(depth, bp, 1, 3 * d)
    mod_s = mod[:, bp:].reshape(depth, bs, 1, 3 * d)

    w_in_b = w_in.astype(BF16)
    w_o_b = w_o.astype(BF16)
    w_pool_b = w_pool.astype(BF16)
    g_pre3 = g_pre.reshape(depth, 1, d)
    g_post3 = g_post.reshape(depth, 1, d)
    pscale3 = pool_scale.reshape(depth, 1, D_POOL)
    hist = jnp.pad(state_pool, ((0, 0), (0, 0), (1, 0), (0, 0)))

    hp, hs = x_prompt, x_sample
    ret_p, pool_p, ret_s, pool_s = [], [], [], []
    for l in range(depth):
        hp, sp, nbp = _prompt_layer(l, hp, mod_p, g_pre3, g_post3, w_in_b, w_pool_b, pscale3, w_o_b, tabs_p)
        hs, ss, nbs = _sample_layer(l, hs, mod_s, g_pre3, g_post3, w_in_b, w_pool_b, pscale3, w_o_b, tabs_s,
                                    state_ret, hist)
        ret_p.append(sp)
        pool_p.append(nbp[:, 1:])
        ret_s.append(ss)
        pool_s.append(nbs[:, 1:])
    return (hp, hs, jnp.stack(ret_p), jnp.stack(pool_p), jnp.stack(ret_s), jnp.stack(pool_s))
```

```python
import functools

import jax
import jax.numpy as jnp
from jax import lax
from jax.experimental import pallas as pl
from jax.experimental.pallas import tpu as pltpu

F32 = jnp.float32
BF16 = jnp.bfloat16

D_MODEL = 1024
D_POOL = 512
D_RET = 512
POOL_WINDOWS = (2, 4, 8, 16)
POOL_CG = 128
HIST = 16
RET_HEADS = 4
RET_DK = 128
RET_CHUNK = 128
ROPE_BASE = 10000.0
PAST_LEN = 16384
D_IN = D_POOL + 3 * D_RET + D_MODEL
EPS = 1e-6
K_SCALE = RET_DK ** -0.5

PROMPT_TILE = 512
SAMPLE_BT = 16
ADA_COLS = 768
VMEM_LIMIT = 56 * 1024 * 1024


def _const_spec(shape):
    return pl.BlockSpec(shape, lambda *_: (0,) * len(shape))


def _decay_tables(lg_lane, lg_row, c, dmask_ref, qdec_ref, kdec_ref, gc_ref):
    idx = lax.broadcasted_iota(jnp.int32, (c, D_RET), 0).astype(F32)
    qdec_ref[...] = jnp.exp((idx + 1.0) * lg_lane)
    kdec_ref[...] = jnp.exp((c - 1.0 - idx) * lg_lane)
    gc_ref[...] = jnp.exp(jnp.full((8, D_RET), float(c), F32) * lg_lane)
    ii = lax.broadcasted_iota(jnp.int32, (c, c), 0)
    jj = lax.broadcasted_iota(jnp.int32, (c, c), 1)
    diff = ii - jj
    for h in range(RET_HEADS):
        dec = jnp.exp(jnp.maximum(diff, 0).astype(F32) * lg_row[h])
        dmask_ref[h] = jnp.where(diff >= 0, dec, 0.0)


def _tables_kernel(inv_ref, csp_ref, snp_ref, css_ref, sns_ref,
                   dmp_ref, qdp_ref, kdp_ref, gcp_ref,
                   dms_ref, qds_ref, kds_ref, gcs_ref, *, seq, dec_seq, past_len):
    inv = inv_ref[...]
    lane = lax.broadcasted_iota(jnp.int32, (1, 2 * 64), 1)
    sign = jnp.where(lane < 64, -1.0, 1.0).astype(F32)

    def rope(n, start, cs_ref, sn_ref):
        pos = (lax.broadcasted_iota(jnp.int32, (n, 128), 0) + start).astype(F32)
        ang = pos * inv
        cs_ref[...] = jnp.cos(ang)
        sn_ref[...] = jnp.sin(ang) * sign

    rope(seq, 0, csp_ref, snp_ref)
    rope(dec_seq, past_len, css_ref, sns_ref)

    head_lane = (lax.broadcasted_iota(jnp.int32, (1, D_RET), 1) // RET_DK).astype(F32)
    lg_lane = jnp.log(1.0 - jnp.exp2(-5.0 - head_lane))
    for c, refs in ((RET_CHUNK, (dmp_ref, qdp_ref, kdp_ref, gcp_ref)),
                    (dec_seq, (dms_ref, qds_ref, kds_ref, gcs_ref))):
        lg_row = [jnp.log(1.0 - jnp.exp2(jnp.full((1, c), -5.0 - h, F32))) for h in range(RET_HEADS)]
        _decay_tables(lg_lane, lg_row, c, *refs)


def _make_tables(seq, dec_seq, past_len):
    half = RET_DK // 2
    inv = 1.0 / (ROPE_BASE ** (jnp.arange(half, dtype=F32) / half))
    inv2 = jnp.concatenate([inv, inv])[None, :]
    c, cs = RET_CHUNK, dec_seq
    out_shape = (
        jax.ShapeDtypeStruct((seq, 128), F32), jax.ShapeDtypeStruct((seq, 128), F32),
        jax.ShapeDtypeStruct((dec_seq, 128), F32), jax.ShapeDtypeStruct((dec_seq, 128), F32),
        jax.ShapeDtypeStruct((RET_HEADS, c, c), F32), jax.ShapeDtypeStruct((c, D_RET), F32),
        jax.ShapeDtypeStruct((c, D_RET), F32), jax.ShapeDtypeStruct((8, D_RET), F32),
        jax.ShapeDtypeStruct((RET_HEADS, cs, cs), F32), jax.ShapeDtypeStruct((cs, D_RET), F32),
        jax.ShapeDtypeStruct((cs, D_RET), F32), jax.ShapeDtypeStruct((8, D_RET), F32),
    )
    return pl.pallas_call(
        functools.partial(_tables_kernel, seq=seq, dec_seq=dec_seq, past_len=past_len),
        out_shape=out_shape,
        name="tables",
    )(inv2)


def _ada_kernel(c_ref, w_ref, b_ref, o_ref):
    c = c_ref[...]
    a = (c * jax.nn.sigmoid(c)).astype(BF16)
    o_ref[0] = jnp.dot(a, w_ref[0].astype(BF16), preferred_element_type=F32) + b_ref[0]


def _ada_mod(c_all, w_ada, b_ada):
    depth, d, d3 = w_ada.shape
    n = c_all.shape[0]
    return pl.pallas_call(
        _ada_kernel,
        grid=(depth, d3 // ADA_COLS),
        in_specs=[
            pl.BlockSpec((n, d), lambda l, j: (0, 0)),
            pl.BlockSpec((1, d, ADA_COLS), lambda l, j: (l, 0, j)),
            pl.BlockSpec((1, 1, ADA_COLS), lambda l, j: (l, 0, j)),
        ],
        out_specs=pl.BlockSpec((1, n, ADA_COLS), lambda l, j: (l, 0, j)),
        out_shape=jax.ShapeDtypeStruct((depth, n, d3), F32),
        compiler_params=pltpu.CompilerParams(dimension_semantics=("arbitrary", "arbitrary")),
        name="ada_mod",
    )(c_all, w_ada, b_ada.reshape(depth, 1, d3))


def _rope(x, cs, sn):
    return x * cs + pltpu.roll(x, RET_DK // 2, x.ndim - 1) * sn


def _rms(x):
    return x * lax.rsqrt(jnp.mean(x * x, axis=-1, keepdims=True) + EPS)


def _prompt_kernel(x_ref, mod_ref, gpre_ref, gpost_ref, win_ref, wpool_ref, pscale_ref, wo_ref,
                   cs_ref, sn_ref, dmask_ref, qdec_ref, kdec_ref, gc_ref,
                   xo_ref, s_ref, nb_ref,
                   hb_ref, ubuf_ref, pm_ref, q_ref, qd_ref, k_ref, kd_ref, v_ref, sc_ref, sprev_ref,
                   mix_ref, sg_ref, *, tile):
    t = pl.program_id(1)
    n_chunks = tile // RET_CHUNK
    halves = ((0, tile // 2), (tile // 2, tile))
    o1, o2, o3, o4 = D_POOL, D_POOL + D_RET, D_POOL + 2 * D_RET, D_POOL + 3 * D_RET
    heads = [slice(hh * RET_DK, (hh + 1) * RET_DK) for hh in range(RET_HEADS)]
    chunks = [slice(c * RET_CHUNK, (c + 1) * RET_CHUNK) for c in range(n_chunks)]

    @pl.when(t == 0)
    def _():
        s_ref[...] = jnp.zeros_like(s_ref)
        ubuf_ref[0:HIST, :] = jnp.zeros((HIST, D_POOL), F32)

    m = mod_ref[0]
    shift, scl, gres = m[:, 0:D_MODEL], m[:, D_MODEL:2 * D_MODEL], m[:, 2 * D_MODEL:]

    def proj(r0, r1, c0, c1):
        return jnp.dot(hb_ref[r0:r1, :], win_ref[:, c0:c1], preferred_element_type=F32)

    for r0, r1 in halves:
        h = _rms(x_ref[0, r0:r1, :]) * gpre_ref[...]
        h = h * (1.0 + scl) + shift
        hb_ref[r0:r1, :] = h.astype(BF16)
        ubuf_ref[HIST + r0:HIST + r1, :] = proj(r0, r1, 0, o1)

    row = lax.broadcasted_iota(jnp.int32, (tile, 1), 0) + t * tile
    for g, w in enumerate(POOL_WINDOWS):
        lo, hi = g * POOL_CG, (g + 1) * POOL_CG
        u_ext = ubuf_ref[:, lo:hi]
        acc = u_ext
        k = 1
        while k < w:
            acc = acc + pltpu.roll(acc, k, 0)
            k *= 2
        cnt = jnp.minimum(row + 1, w).astype(F32)
        pm_ref[:, lo:hi] = (acc[HIST:] / cnt - u_ext[HIST:]).astype(BF16)
    nb_ref[0] = ubuf_ref[tile:tile + HIST, :]
    ubuf_ref[0:HIST, :] = ubuf_ref[tile:tile + HIST, :]

    for r0, r1 in halves:
        zq = proj(r0, r1, o1, o2)
        zk = proj(r0, r1, o2, o3)
        cs, sn = cs_ref[r0:r1, :], sn_ref[r0:r1, :]
        for sl in heads:
            qr = _rope(zq[:, sl], cs, sn)
            kr = _rope(zk[:, sl], cs, sn) * K_SCALE
            q_ref[r0:r1, sl] = qr.astype(BF16)
            k_ref[r0:r1, sl] = kr.astype(BF16)
            for c in range(r0 // RET_CHUNK, r1 // RET_CHUNK):
                rs, ls = chunks[c], slice(c * RET_CHUNK - r0, (c + 1) * RET_CHUNK - r0)
                qd_ref[rs, sl] = (qr[ls] * qdec_ref[:, sl]).astype(BF16)
                kd_ref[rs, sl] = (kr[ls] * kdec_ref[:, sl]).astype(BF16)
        v_ref[r0:r1, :] = proj(r0, r1, o3, o4).astype(BF16)

    for g in range(len(POOL_WINDOWS)):
        lo, hi = g * POOL_CG, (g + 1) * POOL_CG
        y_g = jnp.dot(pm_ref[:, lo:hi], wpool_ref[g], preferred_element_type=F32)
        mix_ref[:, lo:hi] = y_g * pscale_ref[:, lo:hi]

    for rs in chunks:
        for hh, sl in enumerate(heads):
            scores = lax.dot_general(q_ref[rs, sl], k_ref[rs, sl], (((1,), (1,)), ((), ())),
                                     preferred_element_type=F32)
            sc_ref[rs, sl] = (scores * dmask_ref[hh]).astype(BF16)
    kvs = [[lax.dot_general(kd_ref[rs, sl], v_ref[rs, sl], (((0,), (0,)), ((), ())),
                            preferred_element_type=F32) for sl in heads] for rs in chunks]

    r0, r1 = halves[0]
    zg = proj(r0, r1, o4, D_IN)
    sg_ref[r0:r1, :] = zg * jax.nn.sigmoid(zg)

    for hh, sl in enumerate(heads):
        s = s_ref[0, hh]
        for c in range(n_chunks):
            sprev_ref[c * RET_HEADS + hh] = s.astype(BF16)
            s = gc_ref[0:1, sl] * s + kvs[c][hh]
        s_ref[0, hh] = s

    r0, r1 = halves[1]
    zg = proj(r0, r1, o4, D_IN)
    sg_ref[r0:r1, :] = zg * jax.nn.sigmoid(zg)

    for c, rs in enumerate(chunks):
        for hh, sl in enumerate(heads):
            lhs = jnp.concatenate([sc_ref[rs, sl], qd_ref[rs, sl]], axis=1)
            rhs = jnp.concatenate([v_ref[rs, sl], sprev_ref[c * RET_HEADS + hh]], axis=0)
            o_h = jnp.dot(lhs, rhs, preferred_element_type=F32)
            mix_ref[rs, D_POOL + hh * RET_DK:D_POOL + (hh + 1) * RET_DK] = _rms(o_h)

    for r0, r1 in halves:
        mixb = (mix_ref[r0:r1, :] * sg_ref[r0:r1, :]).astype(BF16)
        y = jnp.dot(mixb, wo_ref[...], preferred_element_type=F32)
        xo_ref[0, r0:r1, :] = x_ref[0, r0:r1, :] + gres * (_rms(y) * gpost_ref[...])


def _prompt_layer(l, x, mod, g_pre, g_post, w_in, w_pool, pool_scale, w_o, tabs):
    b, seq, d = x.shape
    tile = PROMPT_TILE
    cs, sn, dmask, qdec, kdec, gc = tabs
    layer = lambda *_: (l, 0, 0)
    kern = functools.partial(_prompt_kernel, tile=tile)
    return pl.pallas_call(
        kern,
        grid=(b, seq // tile),
        in_specs=[
            pl.BlockSpec((1, tile, d), lambda i, t: (i, t, 0)),
            pl.BlockSpec((None, 1, 1, 3 * d), lambda i, t: (l, i, 0, 0)),
            pl.BlockSpec((None, 1, d), layer),
            pl.BlockSpec((None, 1, d), layer),
            pl.BlockSpec((None, d, D_IN), layer),
            pl.BlockSpec((None, len(POOL_WINDOWS), POOL_CG, POOL_CG), lambda *_: (l, 0, 0, 0)),
            pl.BlockSpec((None, 1, D_POOL), layer),
            pl.BlockSpec((None, d, d), layer),
            pl.BlockSpec((tile, 128), lambda i, t: (t, 0)),
            pl.BlockSpec((tile, 128), lambda i, t: (t, 0)),
            _const_spec(dmask.shape), _const_spec(qdec.shape), _const_spec(kdec.shape), _const_spec(gc.shape),
        ],
        out_specs=[
            pl.BlockSpec((1, tile, d), lambda i, t: (i, t, 0)),
            pl.BlockSpec((1, RET_HEADS, RET_DK, RET_DK), lambda i, t: (i, 0, 0, 0)),
            pl.BlockSpec((1, HIST, D_POOL), lambda i, t: (i, 0, 0)),
        ],
        out_shape=[
            jax.ShapeDtypeStruct((b, seq, d), F32),
            jax.ShapeDtypeStruct((b, RET_HEADS, RET_DK, RET_DK), F32),
            jax.ShapeDtypeStruct((b, HIST, D_POOL), F32),
        ],
        scratch_shapes=[
            pltpu.VMEM((tile, d), BF16),
            pltpu.VMEM((HIST + tile, D_POOL), F32),
            pltpu.VMEM((tile, D_POOL), BF16),
            pltpu.VMEM((tile, D_RET), BF16),
            pltpu.VMEM((tile, D_RET), BF16),
            pltpu.VMEM((tile, D_RET), BF16),
            pltpu.VMEM((tile, D_RET), BF16),
            pltpu.VMEM((tile, D_RET), BF16),
            pltpu.VMEM((tile, D_RET), BF16),
            pltpu.VMEM((tile // RET_CHUNK * RET_HEADS, RET_DK, RET_DK), BF16),
            pltpu.VMEM((tile, d), F32),
            pltpu.VMEM((tile, d), F32),
        ],
        compiler_params=pltpu.CompilerParams(
            dimension_semantics=("arbitrary", "arbitrary"), vmem_limit_bytes=VMEM_LIMIT),
        name=f"prompt_layer{l}",
    )(x, mod, g_pre, g_post, w_in, w_pool, pool_scale, w_o, cs, sn, dmask, qdec, kdec, gc)


def _sample_kernel(x_ref, mod_ref, gpre_ref, gpost_ref, win_ref, wpool_ref, pscale_ref, wo_ref,
                   cs_ref, sn_ref, dmask_ref, qdec_ref, kdec_ref, gc_ref, s0_ref, hist_ref,
                   xo_ref, s_ref, nb_ref,
                   ubuf_ref, mix_ref, *, bt, dec_seq):
    n = bt * dec_seq
    m = mod_ref[...]
    shift, scl, gres = m[:, :, 0:D_MODEL], m[:, :, D_MODEL:2 * D_MODEL], m[:, :, 2 * D_MODEL:]

    x3 = x_ref[...]
    h3 = _rms(x3) * gpre_ref[...][None]
    h3 = h3 * (1.0 + scl) + shift
    hb = h3.reshape(n, D_MODEL).astype(BF16)

    u = jnp.dot(hb, win_ref[:, 0:D_POOL], preferred_element_type=F32)
    ubuf_ref[:, 0:HIST, :] = hist_ref[...]
    ubuf_ref[:, HIST:HIST + dec_seq, :] = u.reshape(bt, dec_seq, D_POOL)
    for g, w in enumerate(POOL_WINDOWS):
        lo, hi = g * POOL_CG, (g + 1) * POOL_CG
        u_g = ubuf_ref[:, HIST:HIST + dec_seq, lo:hi]
        acc = u_g
        for k in range(1, w):
            acc = acc + ubuf_ref[:, HIST - k:HIST - k + dec_seq, lo:hi]
        m_g = (acc / float(w) - u_g).reshape(n, POOL_CG)
        y_g = jnp.dot(m_g.astype(BF16), wpool_ref[g], preferred_element_type=F32)
        mix_ref[:, lo:hi] = y_g * pscale_ref[:, lo:hi]
    nb_ref[...] = ubuf_ref[:, dec_seq:dec_seq + HIST, :]

    o1, o2, o3, o4 = D_POOL, D_POOL + D_RET, D_POOL + 2 * D_RET, D_POOL + 3 * D_RET
    zq = jnp.dot(hb, win_ref[:, o1:o2], preferred_element_type=F32).reshape(bt, dec_seq, D_RET)
    zk = jnp.dot(hb, win_ref[:, o2:o3], preferred_element_type=F32).reshape(bt, dec_seq, D_RET)
    zv = jnp.dot(hb, win_ref[:, o3:o4], preferred_element_type=F32).reshape(bt, dec_seq, D_RET)
    cs, sn = cs_ref[...][None], sn_ref[...][None]
    for hh in range(RET_HEADS):
        sl = slice(hh * RET_DK, (hh + 1) * RET_DK)
        qr = _rope(zq[:, :, sl], cs, sn)
        kr = _rope(zk[:, :, sl], cs, sn) * K_SCALE
        v_h = zv[:, :, sl]
        s_prev = s0_ref[:, hh]
        scores = jnp.einsum("bid,bjd->bij", qr, kr, preferred_element_type=F32) * dmask_ref[hh][None]
        intra = jnp.einsum("bij,bje->bie", scores, v_h, preferred_element_type=F32)
        cross = jnp.einsum("bid,bde->bie", qr * qdec_ref[:, sl][None], s_prev, preferred_element_type=F32)
        kv = jnp.einsum("bjd,bje->bde", kr * kdec_ref[:, sl][None], v_h, preferred_element_type=F32)
        s_ref[:, hh] = gc_ref[0:1, sl][None] * s_prev + kv
        mix_ref[:, D_POOL + hh * RET_DK:D_POOL + (hh + 1) * RET_DK] = _rms(intra + cross).reshape(n, RET_DK)

    zg = jnp.dot(hb, win_ref[:, o4:], preferred_element_type=F32)
    mixb = (mix_ref[...] * (zg * jax.nn.sigmoid(zg))).astype(BF16)
    y = jnp.dot(mixb, wo_ref[...], preferred_element_type=F32)
    yn = (_rms(y) * gpost_ref[...]).reshape(bt, dec_seq, D_MODEL)
    xo_ref[...] = x_ref[...] + gres * yn


def _sample_layer(l, x, mod, g_pre, g_post, w_in, w_pool, pool_scale, w_o, tabs, state_ret, hist):
    b, dec_seq, d = x.shape
    bt = SAMPLE_BT
    cs, sn, dmask, qdec, kdec, gc = tabs
    layer = lambda *_: (l, 0, 0)
    kern = functools.partial(_sample_kernel, bt=bt, dec_seq=dec_seq)
    return pl.pallas_call(
        kern,
        grid=(b // bt,),
        in_specs=[
            pl.BlockSpec((bt, dec_seq, d), lambda i: (i, 0, 0)),
            pl.BlockSpec((None, bt, 1, 3 * d), lambda i: (l, i, 0, 0)),
            pl.BlockSpec((None, 1, d), layer),
            pl.BlockSpec((None, 1, d), layer),
            pl.BlockSpec((None, d, D_IN), layer),
            pl.BlockSpec((None, len(POOL_WINDOWS), POOL_CG, POOL_CG), lambda *_: (l, 0, 0, 0)),
            pl.BlockSpec((None, 1, D_POOL), layer),
            pl.BlockSpec((None, d, d), layer),
            _const_spec(cs.shape), _const_spec(sn.shape),
            _const_spec(dmask.shape), _const_spec(qdec.shape), _const_spec(kdec.shape), _const_spec(gc.shape),
            pl.BlockSpec((None, bt, RET_HEADS, RET_DK, RET_DK), lambda i: (l, i, 0, 0, 0)),
            pl.BlockSpec((None, bt, HIST, D_POOL), lambda i: (l, i, 0, 0)),
        ],
        out_specs=[
            pl.BlockSpec((bt, dec_seq, d), lambda i: (i, 0, 0)),
            pl.BlockSpec((bt, RET_HEADS, RET_DK, RET_DK), lambda i: (i, 0, 0, 0)),
            pl.BlockSpec((bt, HIST, D_POOL), lambda i: (i, 0, 0)),
        ],
        out_shape=[
            jax.ShapeDtypeStruct((b, dec_seq, d), F32),
            jax.ShapeDtypeStruct((b, RET_HEADS, RET_DK, RET_DK), F32),
            jax.ShapeDtypeStruct((b, HIST, D_POOL), F32),
        ],
        scratch_shapes=[
            pltpu.VMEM((bt, HIST + dec_seq, D_POOL), F32),
            pltpu.VMEM((bt * dec_seq, d), F32),
        ],
        compiler_params=pltpu.CompilerParams(
            dimension_semantics=("arbitrary",), vmem_limit_bytes=VMEM_LIMIT),
        name=f"sample_layer{l}",
    )(x, mod, g_pre, g_post, w_in, w_pool, pool_scale, w_o, cs, sn, dmask, qdec, kdec, gc, state_ret, hist)


def kernel(x_prompt, x_sample, c_prompt, c_sample, state_ret, state_pool, w_ada, b_ada,
           g_pre, g_post, w_in, w_pool, pool_scale, w_o):
    depth = w_in.shape[0]
    bp, seq, d = x_prompt.shape
    bs, dec_seq, _ = x_sample.shape
    assert d == D_MODEL and seq % PROMPT_TILE == 0 and bs % SAMPLE_BT == 0 and dec_seq == 8

    (csp, snp, css, sns, dmp, qdp, kdp, gcp, dms, qds, kds, gcs) = _make_tables(seq, dec_seq, PAST_LEN)
    tabs_p = (csp, snp, dmp, qdp, kdp, gcp)
    tabs_s = (css, sns, dms, qds, kds, gcs)

    mod = _ada_mod(jnp.concatenate([c_prompt, c_sample], axis=0), w_ada, b_ada)
    mod_p = mod[:, :bp].reshape(depth, bp, 1, 3 * d)
    mod_s = mod[:, bp:].reshape(depth, bs, 1, 3 * d)

    w_in_b = w_in.astype(BF16)
    w_o_b = w_o.astype(BF16)
    w_pool_b = w_pool.astype(BF16)
    g_pre3 = g_pre.reshape(depth, 1, d)
    g_post3 = g_post.reshape(depth, 1, d)
    pscale3 = pool_scale.reshape(depth, 1, D_POOL)
    hist = jnp.pad(state_pool, ((0, 0), (0, 0), (1, 0), (0, 0)))

    hp, hs = x_prompt, x_sample
    ret_p, pool_p, ret_s, pool_s = [], [], [], []
    for l in range(depth):
        hp, sp, nbp = _prompt_layer(l, hp, mod_p, g_pre3, g_post3, w_in_b, w_pool_b, pscale3, w_o_b, tabs_p)
        hs, ss, nbs = _sample_layer(l, hs, mod_s, g_pre3, g_post3, w_in_b, w_pool_b, pscale3, w_o_b, tabs_s,
                                    state_ret, hist)
        ret_p.append(sp)
        pool_p.append(nbp[:, 1:])
        ret_s.append(ss)
        pool_s.append(nbs[:, 1:])
    return (hp, hs, jnp.stack(ret_p), jnp.stack(pool_p), jnp.stack(ret_s), jnp.stack(pool_s))
```

```python
import functools

import jax
import jax.numpy as jnp
from jax import lax
from jax.experimental import pallas as pl
from jax.experimental.pallas import tpu as pltpu

F32 = jnp.float32
BF16 = jnp.bfloat16

D_MODEL = 1024
D_POOL = 512
D_RET = 512
POOL_WINDOWS = (2, 4, 8, 16)
POOL_CG = 128
HIST = 16
RET_HEADS = 4
RET_DK = 128
RET_CHUNK = 128
ROPE_BASE = 10000.0
PAST_LEN = 16384
D_IN = D_POOL + 3 * D_RET + D_MODEL
EPS = 1e-6
K_SCALE = RET_DK ** -0.5

PROMPT_TILE = 512
SAMPLE_BT = 16
ADA_COLS = 768
VMEM_LIMIT = 56 * 1024 * 1024


def _const_spec(shape):
    return pl.BlockSpec(shape, lambda *_: (0,) * len(shape))


def _decay_tables(lg_lane, lg_row, c, dmask_ref, qdec_ref, kdec_ref, gc_ref):
    idx = lax.broadcasted_iota(jnp.int32, (c, D_RET), 0).astype(F32)
    qdec_ref[...] = jnp.exp((idx + 1.0) * lg_lane)
    kdec_ref[...] = jnp.exp((c - 1.0 - idx) * lg_lane)
    gc_ref[...] = jnp.exp(jnp.full((8, D_RET), float(c), F32) * lg_lane)
    ii = lax.broadcasted_iota(jnp.int32, (c, c), 0)
    jj = lax.broadcasted_iota(jnp.int32, (c, c), 1)
    diff = ii - jj
    for h in range(RET_HEADS):
        dec = jnp.exp(jnp.maximum(diff, 0).astype(F32) * lg_row[h])
        dmask_ref[h] = jnp.where(diff >= 0, dec, 0.0)


def _tables_kernel(inv_ref, csp_ref, snp_ref, css_ref, sns_ref,
                   dmp_ref, qdp_ref, kdp_ref, gcp_ref,
                   dms_ref, qds_ref, kds_ref, gcs_ref, *, seq, dec_seq, past_len):
    inv = inv_ref[...]
    lane = lax.broadcasted_iota(jnp.int32, (1, 2 * 64), 1)
    sign = jnp.where(lane < 64, -1.0, 1.0).astype(F32)

    def rope(n, start, cs_ref, sn_ref):
        pos = (lax.broadcasted_iota(jnp.int32, (n, 128), 0) + start).astype(F32)
        ang = pos * inv
        cs_ref[...] = jnp.cos(ang)
        sn_ref[...] = jnp.sin(ang) * sign

    rope(seq, 0, csp_ref, snp_ref)
    rope(dec_seq, past_len, css_ref, sns_ref)

    head_lane = (lax.broadcasted_iota(jnp.int32, (1, D_RET), 1) // RET_DK).astype(F32)
    lg_lane = jnp.log(1.0 - jnp.exp2(-5.0 - head_lane))
    for c, refs in ((RET_CHUNK, (dmp_ref, qdp_ref, kdp_ref, gcp_ref)),
                    (dec_seq, (dms_ref, qds_ref, kds_ref, gcs_ref))):
        lg_row = [jnp.log(1.0 - jnp.exp2(jnp.full((1, c), -5.0 - h, F32))) for h in range(RET_HEADS)]
        _decay_tables(lg_lane, lg_row, c, *refs)


def _make_tables(seq, dec_seq, past_len):
    half = RET_DK // 2
    inv = 1.0 / (ROPE_BASE ** (jnp.arange(half, dtype=F32) / half))
    inv2 = jnp.concatenate([inv, inv])[None, :]
    c, cs = RET_CHUNK, dec_seq
    out_shape = (
        jax.ShapeDtypeStruct((seq, 128), F32), jax.ShapeDtypeStruct((seq, 128), F32),
        jax.ShapeDtypeStruct((dec_seq, 128), F32), jax.ShapeDtypeStruct((dec_seq, 128), F32),
        jax.ShapeDtypeStruct((RET_HEADS, c, c), F32), jax.ShapeDtypeStruct((c, D_RET), F32),
        jax.ShapeDtypeStruct((c, D_RET), F32), jax.ShapeDtypeStruct((8, D_RET), F32),
        jax.ShapeDtypeStruct((RET_HEADS, cs, cs), F32), jax.ShapeDtypeStruct((cs, D_RET), F32),
        jax.ShapeDtypeStruct((cs, D_RET), F32), jax.ShapeDtypeStruct((8, D_RET), F32),
    )
    return pl.pallas_call(
        functools.partial(_tables_kernel, seq=seq, dec_seq=dec_seq, past_len=past_len),
        out_shape=out_shape,
        name="tables",
    )(inv2)


def _ada_kernel(c_ref, w_ref, b_ref, o_ref):
    c = c_ref[...]
    a = (c * jax.nn.sigmoid(c)).astype(BF16)
    o_ref[0] = jnp.dot(a, w_ref[0].astype(BF16), preferred_element_type=F32) + b_ref[0]


def _ada_mod(c_all, w_ada, b_ada):
    depth, d, d3 = w_ada.shape
    n = c_all.shape[0]
    return pl.pallas_call(
        _ada_kernel,
        grid=(depth, d3 // ADA_COLS),
        in_specs=[
            pl.BlockSpec((n, d), lambda l, j: (0, 0)),
            pl.BlockSpec((1, d, ADA_COLS), lambda l, j: (l, 0, j)),
            pl.BlockSpec((1, 1, ADA_COLS), lambda l, j: (l, 0, j)),
        ],
        out_specs=pl.BlockSpec((1, n, ADA_COLS), lambda l, j: (l, 0, j)),
        out_shape=jax.ShapeDtypeStruct((depth, n, d3), F32),
        compiler_params=pltpu.CompilerParams(dimension_semantics=("arbitrary", "arbitrary")),
        name="ada_mod",
    )(c_all, w_ada, b_ada.reshape(depth, 1, d3))


def _rope(x, cs, sn):
    return x * cs + pltpu.roll(x, RET_DK // 2, x.ndim - 1) * sn


def _rms(x):
    return x * lax.rsqrt(jnp.mean(x * x, axis=-1, keepdims=True) + EPS)


def _prompt_kernel(x_ref, mod_ref, gpre_ref, gpost_ref, win_ref, wpool_ref, pscale_ref, wo_ref,
                   cs_ref, sn_ref, dmask_ref, qdec_ref, kdec_ref, gc_ref,
                   xo_ref, s_ref, nb_ref,
                   hb_ref, ubuf_ref, pm_ref, q_ref, qd_ref, k_ref, kd_ref, v_ref, sc_ref, sprev_ref,
                   mix_ref, sg_ref, *, tile):
    t = pl.program_id(1)
    n_chunks = tile // RET_CHUNK
    halves = ((0, tile // 2), (tile // 2, tile))
    o1, o2, o3, o4 = D_POOL, D_POOL + D_RET, D_POOL + 2 * D_RET, D_POOL + 3 * D_RET
    heads = [slice(hh * RET_DK, (hh + 1) * RET_DK) for hh in range(RET_HEADS)]
    chunks = [slice(c * RET_CHUNK, (c + 1) * RET_CHUNK) for c in range(n_chunks)]

    @pl.when(t == 0)
    def _():
        s_ref[...] = jnp.zeros_like(s_ref)
        ubuf_ref[0:HIST, :] = jnp.zeros((HIST, D_POOL), F32)

    m = mod_ref[0]
    shift, scl, gres = m[:, 0:D_MODEL], m[:, D_MODEL:2 * D_MODEL], m[:, 2 * D_MODEL:]

    def proj(r0, r1, c0, c1):
        return jnp.dot(hb_ref[r0:r1, :], win_ref[:, c0:c1], preferred_element_type=F32)

    for r0, r1 in halves:
        h = _rms(x_ref[0, r0:r1, :]) * gpre_ref[...]
        h = h * (1.0 + scl) + shift
        hb_ref[r0:r1, :] = h.astype(BF16)
        ubuf_ref[HIST + r0:HIST + r1, :] = proj(r0, r1, 0, o1)

    row = lax.broadcasted_iota(jnp.int32, (tile, 1), 0) + t * tile
    for g, w in enumerate(POOL_WINDOWS):
        lo, hi = g * POOL_CG, (g + 1) * POOL_CG
        u_ext = ubuf_ref[:, lo:hi]
        acc = u_ext
        k = 1
        while k < w:
            acc = acc + pltpu.roll(acc, k, 0)
            k *= 2
        cnt = jnp.minimum(row + 1, w).astype(F32)
        pm_ref[:, lo:hi] = (acc[HIST:] / cnt - u_ext[HIST:]).astype(BF16)
    nb_ref[0] = ubuf_ref[tile:tile + HIST, :]
    ubuf_ref[0:HIST, :] = ubuf_ref[tile:tile + HIST, :]

    for r0, r1 in halves:
        zq = proj(r0, r1, o1, o2)
        zk = proj(r0, r1, o2, o3)
        cs, sn = cs_ref[r0:r1, :], sn_ref[r0:r1, :]
        for sl in heads:
            qr = _rope(zq[:, sl], cs, sn)
            kr = _rope(zk[:, sl], cs, sn) * K_SCALE
            q_ref[r0:r1, sl] = qr.astype(BF16)
            k_ref[r0:r1, sl] = kr.astype(BF16)
            for c in range(r0 // RET_CHUNK, r1 // RET_CHUNK):
                rs, ls = chunks[c], slice(c * RET_CHUNK - r0, (c + 1) * RET_CHUNK - r0)
                qd_ref[rs, sl] = (qr[ls] * qdec_ref[:, sl]).astype(BF16)
                kd_ref[rs, sl] = (kr[ls] * kdec_ref[:, sl]).astype(BF16)
        v_ref[r0:r1, :] = proj(r0, r1, o3, o4).astype(BF16)

    for g in range(len(POOL_WINDOWS)):
        lo, hi = g * POOL_CG, (g + 1) * POOL_CG
        y_g = jnp.dot(pm_ref[:, lo:hi], wpool_ref[g], preferred_element_type=F32)
        mix_ref[:, lo:hi] = y_g * pscale_ref[:, lo:hi]

    for rs in chunks:
        for hh, sl in enumerate(heads):
            scores = lax.dot_general(q_ref[rs, sl], k_ref[rs, sl], (((1,), (1,)), ((), ())),
                                     preferred_element_type=F32)
            sc_ref[rs, sl] = (scores * dmask_ref[hh]).astype(BF16)
    kvs = [[lax.dot_general(kd_ref[rs, sl], v_ref[rs, sl], (((0,), (0,)), ((), ())),
                            preferred_element_type=F32) for sl in heads] for rs in chunks]

    r0, r1 = halves[0]
    zg = proj(r0, r1, o4, D_IN)
    sg_ref[r0:r1, :] = zg * jax.nn.sigmoid(zg)

    for hh, sl in enumerate(heads):
        s = s_ref[0, hh]
        for c in range(n_chunks):
            sprev_ref[c * RET_HEADS + hh] = s.astype(BF16)
            s = gc_ref[0:1, sl] * s + kvs[c][hh]
        s_ref[0, hh] = s

    r0, r1 = halves[1]
    zg = proj(r0, r1, o4, D_IN)
    sg_ref[r0:r1, :] = zg * jax.nn.sigmoid(zg)

    for c, rs in enumerate(chunks):
        for hh, sl in enumerate(heads):
            lhs = jnp.concatenate([sc_ref[rs, sl], qd_ref[rs, sl]], axis=1)
            rhs = jnp.concatenate([v_ref[rs, sl], sprev_ref[c * RET_HEADS + hh]], axis=0)
            o_h = jnp.dot(lhs, rhs, preferred_element_type=F32)
            mix_ref[rs, D_POOL + hh * RET_DK:D_POOL + (hh + 1) * RET_DK] = _rms(o_h)

    for r0, r1 in halves:
        mixb = (mix_ref[r0:r1, :] * sg_ref[r0:r1, :]).astype(BF16)
        y = jnp.dot(mixb, wo_ref[...], preferred_element_type=F32)
        xo_ref[0, r0:r1, :] = x_ref[0, r0:r1, :] + gres * (_rms(y) * gpost_ref[...])


def _prompt_layer(l, x, mod, g_pre, g_post, w_in, w_pool, pool_scale, w_o, tabs):
    b, seq, d = x.shape
    tile = PROMPT_TILE
    cs, sn, dmask, qdec, kdec, gc = tabs
    layer = lambda *_: (l, 0, 0)
    kern = functools.partial(_prompt_kernel, tile=tile)
    return pl.pallas_call(
        kern,
        grid=(b, seq // tile),
        in_specs=[
            pl.BlockSpec((1, tile, d), lambda i, t: (i, t, 0)),
            pl.BlockSpec((None, 1, 1, 3 * d), lambda i, t: (l, i, 0, 0)),
            pl.BlockSpec((None, 1, d), layer),
            pl.BlockSpec((None, 1, d), layer),
            pl.BlockSpec((None, d, D_IN), layer),
            pl.BlockSpec((None, len(POOL_WINDOWS), POOL_CG, POOL_CG), lambda *_: (l, 0, 0, 0)),
            pl.BlockSpec((None, 1, D_POOL), layer),
            pl.BlockSpec((None, d, d), layer),
            pl.BlockSpec((tile, 128), lambda i, t: (t, 0)),
            pl.BlockSpec((tile, 128), lambda i, t: (t, 0)),
            _const_spec(dmask.shape), _const_spec(qdec.shape), _const_spec(kdec.shape), _const_spec(gc.shape),
        ],
        out_specs=[
            pl.BlockSpec((1, tile, d), lambda i, t: (i, t, 0)),
            pl.BlockSpec((1, RET_HEADS, RET_DK, RET_DK), lambda i, t: (i, 0, 0, 0)),
            pl.BlockSpec((1, HIST, D_POOL), lambda i, t: (i, 0, 0)),
        ],
        out_shape=[
            jax.ShapeDtypeStruct((b, seq, d), F32),
            jax.ShapeDtypeStruct((b, RET_HEADS, RET_DK, RET_DK), F32),
            jax.ShapeDtypeStruct((b, HIST, D_POOL), F32),
        ],
        scratch_shapes=[
            pltpu.VMEM((tile, d), BF16),
            pltpu.VMEM((HIST + tile, D_POOL), F32),
            pltpu.VMEM((tile, D_POOL), BF16),
            pltpu.VMEM((tile, D_RET), BF16),
            pltpu.VMEM((tile, D_RET), BF16),
            pltpu.VMEM((tile, D_RET), BF16),
            pltpu.VMEM((tile, D_RET), BF16),
            pltpu.VMEM((tile, D_RET), BF16),
            pltpu.VMEM((tile, D_RET), BF16),
            pltpu.VMEM((tile // RET_CHUNK * RET_HEADS, RET_DK, RET_DK), BF16),
            pltpu.VMEM((tile, d), F32),
            pltpu.VMEM((tile, d), F32),
        ],
        compiler_params=pltpu.CompilerParams(
            dimension_semantics=("arbitrary", "arbitrary"), vmem_limit_bytes=VMEM_LIMIT),
        name=f"prompt_layer{l}",
    )(x, mod, g_pre, g_post, w_in, w_pool, pool_scale, w_o, cs, sn, dmask, qdec, kdec, gc)


def _sample_kernel(x_ref, mod_ref, gpre_ref, gpost_ref, win_ref, wpool_ref, pscale_ref, wo_ref,
                   cs_ref, sn_ref, dmask_ref, qdec_ref, kdec_ref, gc_ref, s0_ref, hist_ref,
                   xo_ref, s_ref, nb_ref,
                   ubuf_ref, mix_ref, *, bt, dec_seq):
    n = bt * dec_seq
    pool_buf = hist_ref.shape[1]
    rows = pl.ds(pl.multiple_of(pl.program_id(1) * bt, bt), bt)

    @pl.when(pl.program_id(0) == 0)
    def _():
        xo_ref[rows] = x_ref[...]

    m = mod_ref[...]
    shift, scl, gres = m[:, :, 0:D_MODEL], m[:, :, D_MODEL:2 * D_MODEL], m[:, :, 2 * D_MODEL:]

    x3 = xo_ref[rows]
    h3 = _rms(x3) * gpre_ref[...][None]
    h3 = h3 * (1.0 + scl) + shift
    hb = h3.reshape(n, D_MODEL).astype(BF16)

    u = jnp.dot(hb, win_ref[:, 0:D_POOL], preferred_element_type=F32)
    ubuf_ref[:, HIST - pool_buf:HIST, :] = hist_ref[...]
    ubuf_ref[:, HIST:HIST + dec_seq, :] = u.reshape(bt, dec_seq, D_POOL)
    for g, w in enumerate(POOL_WINDOWS):
        lo, hi = g * POOL_CG, (g + 1) * POOL_CG
        u_g = ubuf_ref[:, HIST:HIST + dec_seq, lo:hi]
        acc = u_g
        for k in range(1, w):
            acc = acc + ubuf_ref[:, HIST - k:HIST - k + dec_seq, lo:hi]
        m_g = (acc / float(w) - u_g).reshape(n, POOL_CG)
        y_g = jnp.dot(m_g.astype(BF16), wpool_ref[g], preferred_element_type=F32)
        mix_ref[:, lo:hi] = y_g * pscale_ref[:, lo:hi]
    nb_ref[...] = ubuf_ref[:, HIST + dec_seq - pool_buf:HIST + dec_seq, :]

    o1, o2, o3, o4 = D_POOL, D_POOL + D_RET, D_POOL + 2 * D_RET, D_POOL + 3 * D_RET
    zq = jnp.dot(hb, win_ref[:, o1:o2], preferred_element_type=F32).reshape(bt, dec_seq, D_RET)
    zk = jnp.dot(hb, win_ref[:, o2:o3], preferred_element_type=F32).reshape(bt, dec_seq, D_RET)
    zv = jnp.dot(hb, win_ref[:, o3:o4], preferred_element_type=F32).reshape(bt, dec_seq, D_RET)
    cs, sn = cs_ref[...][None], sn_ref[...][None]
    for hh in range(RET_HEADS):
        sl = slice(hh * RET_DK, (hh + 1) * RET_DK)
        qr = _rope(zq[:, :, sl], cs, sn)
        kr = _rope(zk[:, :, sl], cs, sn) * K_SCALE
        v_h = zv[:, :, sl]
        s_prev = s0_ref[:, hh]
        scores = jnp.einsum("bid,bjd->bij", qr, kr, preferred_element_type=F32) * dmask_ref[hh][None]
        intra = jnp.einsum("bij,bje->bie", scores, v_h, preferred_element_type=F32)
        cross = jnp.einsum("bid,bde->bie", qr * qdec_ref[:, sl][None], s_prev, preferred_element_type=F32)
        kv = jnp.einsum("bjd,bje->bde", kr * kdec_ref[:, sl][None], v_h, preferred_element_type=F32)
        s_ref[:, hh] = gc_ref[0:1, sl][None] * s_prev + kv
        mix_ref[:, D_POOL + hh * RET_DK:D_POOL + (hh + 1) * RET_DK] = _rms(intra + cross).reshape(n, RET_DK)

    zg = jnp.dot(hb, win_ref[:, o4:], preferred_element_type=F32)
    mixb = (mix_ref[...] * (zg * jax.nn.sigmoid(zg))).astype(BF16)
    y = jnp.dot(mixb, wo_ref[...], preferred_element_type=F32)
    yn = (_rms(y) * gpost_ref[...]).reshape(bt, dec_seq, D_MODEL)
    xo_ref[rows] = x3 + gres * yn


def _sample_layers(x, mod, g_pre, g_post, w_in, w_pool, pool_scale, w_o, tabs, state_ret, state_pool):
    b, dec_seq, d = x.shape
    depth, _, pool_buf, _ = state_pool.shape
    bt = SAMPLE_BT
    cs, sn, dmask, qdec, kdec, gc = tabs
    layer = lambda l, i: (l, 0, 0)
    kern = functools.partial(_sample_kernel, bt=bt, dec_seq=dec_seq)
    return pl.pallas_call(
        kern,
        grid=(depth, b // bt),
        in_specs=[
            pl.BlockSpec((bt, dec_seq, d), lambda l, i: (i, 0, 0)),
            pl.BlockSpec((None, bt, 1, 3 * d), lambda l, i: (l, i, 0, 0)),
            pl.BlockSpec((None, 1, d), layer),
            pl.BlockSpec((None, 1, d), layer),
            pl.BlockSpec((None, d, D_IN), layer),
            pl.BlockSpec((None, len(POOL_WINDOWS), POOL_CG, POOL_CG), lambda l, i: (l, 0, 0, 0)),
            pl.BlockSpec((None, 1, D_POOL), layer),
            pl.BlockSpec((None, d, d), layer),
            _const_spec(cs.shape), _const_spec(sn.shape),
            _const_spec(dmask.shape), _const_spec(qdec.shape), _const_spec(kdec.shape), _const_spec(gc.shape),
            pl.BlockSpec((None, bt, RET_HEADS, RET_DK, RET_DK), lambda l, i: (l, i, 0, 0, 0)),
            pl.BlockSpec((None, bt, pool_buf, D_POOL), lambda l, i: (l, i, 0, 0)),
        ],
        out_specs=[
            pl.BlockSpec((b, dec_seq, d), lambda l, i: (0, 0, 0)),
            pl.BlockSpec((None, bt, RET_HEADS, RET_DK, RET_DK), lambda l, i: (l, i, 0, 0, 0)),
            pl.BlockSpec((None, bt, pool_buf, D_POOL), lambda l, i: (l, i, 0, 0)),
        ],
        out_shape=[
            jax.ShapeDtypeStruct((b, dec_seq, d), F32),
            jax.ShapeDtypeStruct((depth, b, RET_HEADS, RET_DK, RET_DK), F32),
            jax.ShapeDtypeStruct((depth, b, pool_buf, D_POOL), F32),
        ],
        scratch_shapes=[
            pltpu.VMEM((bt, HIST + dec_seq, D_POOL), F32),
            pltpu.VMEM((bt * dec_seq, d), F32),
        ],
        compiler_params=pltpu.CompilerParams(
            dimension_semantics=("arbitrary", "arbitrary"), vmem_limit_bytes=VMEM_LIMIT),
        name="sample_layers",
    )(x, mod, g_pre, g_post, w_in, w_pool, pool_scale, w_o, cs, sn, dmask, qdec, kdec, gc, state_ret, state_pool)


def kernel(x_prompt, x_sample, c_prompt, c_sample, state_ret, state_pool, w_ada, b_ada,
           g_pre, g_post, w_in, w_pool, pool_scale, w_o):
    depth = w_in.shape[0]
    bp, seq, d = x_prompt.shape
    bs, dec_seq, _ = x_sample.shape
    assert d == D_MODEL and seq % PROMPT_TILE == 0 and bs % SAMPLE_BT == 0 and dec_seq == 8

    (csp, snp, css, sns, dmp, qdp, kdp, gcp, dms, qds, kds, gcs) = _make_tables(seq, dec_seq, PAST_LEN)
    tabs_p = (csp, snp, dmp, qdp, kdp, gcp)
    tabs_s = (css, sns, dms, qds, kds, gcs)

    mod = _ada_mod(jnp.concatenate([c_prompt, c_sample], axis=0), w_ada, b_ada)
    mod_p = mod[:, :bp].reshape(depth, bp, 1, 3 * d)
    mod_s = mod[:, bp:].reshape(depth, bs, 1, 3 * d)

    w_in_b = w_in.astype(BF16)
    w_o_b = w_o.astype(BF16)
    w_pool_b = w_pool.astype(BF16)
    g_pre3 = g_pre.reshape(depth, 1, d)
    g_post3 = g_post.reshape(depth, 1, d)
    pscale3 = pool_scale.reshape(depth, 1, D_POOL)
    hs, ret_s, pool_s = _sample_layers(x_sample, mod_s, g_pre3, g_post3, w_in_b, w_pool_b, pscale3, w_o_b,
                                       tabs_s, state_ret, state_pool)
    hp = x_prompt
    ret_p, pool_p = [], []
    for l in range(depth):
        hp, sp, nbp = _prompt_layer(l, hp, mod_p, g_pre3, g_post3, w_in_b, w_pool_b, pscale3, w_o_b, tabs_p)
        ret_p.append(sp)
        pool_p.append(nbp[:, 1:])
    return (hp, hs, jnp.stack(ret_p), jnp.stack(pool_p), ret_s, pool_s)
```

```python
import functools

import jax
import jax.numpy as jnp
from jax import lax
from jax.experimental import pallas as pl
from jax.experimental.pallas import tpu as pltpu

F32 = jnp.float32
BF16 = jnp.bfloat16

D_MODEL = 1024
D_POOL = 512
D_RET = 512
POOL_WINDOWS = (2, 4, 8, 16)
POOL_CG = 128
HIST = 16
RET_HEADS = 4
RET_DK = 128
RET_CHUNK = 128
ROPE_BASE = 10000.0
PAST_LEN = 16384
D_IN = D_POOL + 3 * D_RET + D_MODEL
EPS = 1e-6
K_SCALE = RET_DK ** -0.5

PROMPT_TILE = 1024
SAMPLE_BT = 16
ADA_COLS = 768
VMEM_LIMIT = 56 * 1024 * 1024


def _const_spec(shape):
    return pl.BlockSpec(shape, lambda *_: (0,) * len(shape))


def _decay_tables(lg_lane, lg_row, c, dmask_ref, qdec_ref, kdec_ref, gc_ref):
    idx = lax.broadcasted_iota(jnp.int32, (c, D_RET), 0).astype(F32)
    qdec_ref[...] = jnp.exp((idx + 1.0) * lg_lane)
    kdec_ref[...] = jnp.exp((c - 1.0 - idx) * lg_lane)
    gc_ref[...] = jnp.exp(jnp.full((8, D_RET), float(c), F32) * lg_lane)
    ii = lax.broadcasted_iota(jnp.int32, (c, c), 0)
    jj = lax.broadcasted_iota(jnp.int32, (c, c), 1)
    diff = ii - jj
    for h in range(RET_HEADS):
        dec = jnp.exp(jnp.maximum(diff, 0).astype(F32) * lg_row[h])
        dmask_ref[h] = jnp.where(diff >= 0, dec, 0.0)


def _tables_kernel(inv_ref, csp_ref, snp_ref, css_ref, sns_ref,
                   dmp_ref, qdp_ref, kdp_ref, gcp_ref,
                   dms_ref, qds_ref, kds_ref, gcs_ref, *, seq, dec_seq, past_len):
    inv = inv_ref[...]
    lane = lax.broadcasted_iota(jnp.int32, (1, 2 * 64), 1)
    sign = jnp.where(lane < 64, -1.0, 1.0).astype(F32)

    def rope(n, start, cs_ref, sn_ref):
        pos = (lax.broadcasted_iota(jnp.int32, (n, 128), 0) + start).astype(F32)
        ang = pos * inv
        cs_ref[...] = jnp.cos(ang)
        sn_ref[...] = jnp.sin(ang) * sign

    rope(seq, 0, csp_ref, snp_ref)
    rope(dec_seq, past_len, css_ref, sns_ref)

    head_lane = (lax.broadcasted_iota(jnp.int32, (1, D_RET), 1) // RET_DK).astype(F32)
    lg_lane = jnp.log(1.0 - jnp.exp2(-5.0 - head_lane))
    for c, refs in ((RET_CHUNK, (dmp_ref, qdp_ref, kdp_ref, gcp_ref)),
                    (dec_seq, (dms_ref, qds_ref, kds_ref, gcs_ref))):
        lg_row = [jnp.log(1.0 - jnp.exp2(jnp.full((1, c), -5.0 - h, F32))) for h in range(RET_HEADS)]
        _decay_tables(lg_lane, lg_row, c, *refs)


def _make_tables(seq, dec_seq, past_len):
    half = RET_DK // 2
    inv = 1.0 / (ROPE_BASE ** (jnp.arange(half, dtype=F32) / half))
    inv2 = jnp.concatenate([inv, inv])[None, :]
    c, cs = RET_CHUNK, dec_seq
    out_shape = (
        jax.ShapeDtypeStruct((seq, 128), F32), jax.ShapeDtypeStruct((seq, 128), F32),
        jax.ShapeDtypeStruct((dec_seq, 128), F32), jax.ShapeDtypeStruct((dec_seq, 128), F32),
        jax.ShapeDtypeStruct((RET_HEADS, c, c), F32), jax.ShapeDtypeStruct((c, D_RET), F32),
        jax.ShapeDtypeStruct((c, D_RET), F32), jax.ShapeDtypeStruct((8, D_RET), F32),
        jax.ShapeDtypeStruct((RET_HEADS, cs, cs), F32), jax.ShapeDtypeStruct((cs, D_RET), F32),
        jax.ShapeDtypeStruct((cs, D_RET), F32), jax.ShapeDtypeStruct((8, D_RET), F32),
    )
    return pl.pallas_call(
        functools.partial(_tables_kernel, seq=seq, dec_seq=dec_seq, past_len=past_len),
        out_shape=out_shape,
        name="tables",
    )(inv2)


def _ada_kernel(cs_ref, cp_ref, w_ref, b_ref, os_ref, op_ref):
    w = w_ref[0].astype(BF16)
    for c_ref, o_ref in ((cs_ref, os_ref), (cp_ref, op_ref)):
        c = c_ref[...]
        a = (c * jax.nn.sigmoid(c)).astype(BF16)
        o_ref[0] = jnp.dot(a, w, preferred_element_type=F32) + b_ref[0]


def _ada_mod(c_sample, c_prompt, w_ada, b_ada):
    depth, d, d3 = w_ada.shape
    bs, bp = c_sample.shape[0], c_prompt.shape[0]
    return pl.pallas_call(
        _ada_kernel,
        grid=(depth, d3 // ADA_COLS),
        in_specs=[
            pl.BlockSpec((bs, d), lambda l, j: (0, 0)),
            pl.BlockSpec((bp, d), lambda l, j: (0, 0)),
            pl.BlockSpec((1, d, ADA_COLS), lambda l, j: (l, 0, j)),
            pl.BlockSpec((1, 1, ADA_COLS), lambda l, j: (l, 0, j)),
        ],
        out_specs=[pl.BlockSpec((1, bs, ADA_COLS), lambda l, j: (l, 0, j)),
                   pl.BlockSpec((1, bp, ADA_COLS), lambda l, j: (l, 0, j))],
        out_shape=[jax.ShapeDtypeStruct((depth, bs, d3), F32),
                   jax.ShapeDtypeStruct((depth, bp, d3), F32)],
        compiler_params=pltpu.CompilerParams(dimension_semantics=("arbitrary", "arbitrary")),
        name="ada_mod",
    )(c_sample, c_prompt, w_ada, b_ada.reshape(depth, 1, d3))


def _rope(x, cs, sn):
    return x * cs + pltpu.roll(x, RET_DK // 2, x.ndim - 1) * sn


def _rms(x):
    return x * lax.rsqrt(jnp.mean(x * x, axis=-1, keepdims=True) + EPS)


def _prompt_kernel(x_ref, mod_ref, gpre_ref, gpost_ref, win_ref, wpool_ref, pscale_ref, wo_ref,
                   cs_ref, sn_ref, dmask_ref, qdec_ref, kdec_ref, gc_ref,
                   xo_ref, s_ref, nb_ref,
                   hb_ref, ubuf_ref, pm_ref, q_ref, qd_ref, k_ref, kd_ref, v_ref, sc_ref, sprev_ref,
                   mix_ref, sg_ref, *, tile):
    t = pl.program_id(1)
    n_chunks = tile // RET_CHUNK
    halves = ((0, tile // 2), (tile // 2, tile))
    head_blocks = ((0, RET_CHUNK), (RET_CHUNK, tile // 2), (tile // 2, tile))
    tail_blocks = ((0, tile // 2), (tile // 2, tile - RET_CHUNK), (tile - RET_CHUNK, tile))
    o1, o2, o3, o4 = D_POOL, D_POOL + D_RET, D_POOL + 2 * D_RET, D_POOL + 3 * D_RET
    heads = [slice(hh * RET_DK, (hh + 1) * RET_DK) for hh in range(RET_HEADS)]
    chunks = [slice(c * RET_CHUNK, (c + 1) * RET_CHUNK) for c in range(n_chunks)]

    @pl.when(t == 0)
    def _():
        s_ref[...] = jnp.zeros_like(s_ref)
        ubuf_ref[0:HIST, :] = jnp.zeros((HIST, D_POOL), F32)

    m = mod_ref[pl.ds(pl.program_id(0), 1), :]
    shift, scl, gres = m[:, 0:D_MODEL], m[:, D_MODEL:2 * D_MODEL], m[:, 2 * D_MODEL:]

    def proj(r0, r1, c0, c1):
        return jnp.dot(hb_ref[r0:r1, :], win_ref[:, c0:c1], preferred_element_type=F32)

    for r0, r1 in head_blocks:
        h = _rms(x_ref[0, r0:r1, :]) * gpre_ref[...]
        h = h * (1.0 + scl) + shift
        hb_ref[r0:r1, :] = h.astype(BF16)
        ubuf_ref[HIST + r0:HIST + r1, :] = proj(r0, r1, 0, o1)

    row = lax.broadcasted_iota(jnp.int32, (tile, 1), 0) + t * tile
    for g, w in enumerate(POOL_WINDOWS):
        lo, hi = g * POOL_CG, (g + 1) * POOL_CG
        u_ext = ubuf_ref[:, lo:hi]
        acc = u_ext
        k = 1
        while k < w:
            acc = acc + pltpu.roll(acc, k, 0)
            k *= 2
        cnt = jnp.minimum(row + 1, w).astype(F32)
        pm_ref[:, lo:hi] = (acc[HIST:] / cnt - u_ext[HIST:]).astype(BF16)
    nb_ref[0] = ubuf_ref[tile:tile + HIST, :]
    ubuf_ref[0:HIST, :] = ubuf_ref[tile:tile + HIST, :]

    for r0, r1 in halves:
        zq = proj(r0, r1, o1, o2)
        zk = proj(r0, r1, o2, o3)
        cs, sn = cs_ref[r0:r1, :], sn_ref[r0:r1, :]
        for sl in heads:
            qr = _rope(zq[:, sl], cs, sn)
            kr = _rope(zk[:, sl], cs, sn) * K_SCALE
            q_ref[r0:r1, sl] = qr.astype(BF16)
            k_ref[r0:r1, sl] = kr.astype(BF16)
            for c in range(r0 // RET_CHUNK, r1 // RET_CHUNK):
                rs, ls = chunks[c], slice(c * RET_CHUNK - r0, (c + 1) * RET_CHUNK - r0)
                qd_ref[rs, sl] = (qr[ls] * qdec_ref[:, sl]).astype(BF16)
                kd_ref[rs, sl] = (kr[ls] * kdec_ref[:, sl]).astype(BF16)
        v_ref[r0:r1, :] = proj(r0, r1, o3, o4).astype(BF16)

    for g in range(len(POOL_WINDOWS)):
        lo, hi = g * POOL_CG, (g + 1) * POOL_CG
        y_g = jnp.dot(pm_ref[:, lo:hi], wpool_ref[g], preferred_element_type=F32)
        mix_ref[:, lo:hi] = y_g * pscale_ref[:, lo:hi]

    for rs in chunks:
        for hh, sl in enumerate(heads):
            scores = lax.dot_general(q_ref[rs, sl], k_ref[rs, sl], (((1,), (1,)), ((), ())),
                                     preferred_element_type=F32)
            sc_ref[rs, sl] = (scores * dmask_ref[hh]).astype(BF16)
    kvs = [[lax.dot_general(kd_ref[rs, sl], v_ref[rs, sl], (((0,), (0,)), ((), ())),
                            preferred_element_type=F32) for sl in heads] for rs in chunks]

    r0, r1 = halves[0]
    zg = proj(r0, r1, o4, D_IN)
    sg_ref[r0:r1, :] = zg * jax.nn.sigmoid(zg)

    for hh, sl in enumerate(heads):
        s = s_ref[0, hh]
        for c in range(n_chunks):
            sprev_ref[c * RET_HEADS + hh] = s.astype(BF16)
            s = gc_ref[0:1, sl] * s + kvs[c][hh]
        s_ref[0, hh] = s

    r0, r1 = halves[1]
    zg = proj(r0, r1, o4, D_IN)
    sg_ref[r0:r1, :] = zg * jax.nn.sigmoid(zg)

    for c, rs in enumerate(chunks):
        for hh, sl in enumerate(heads):
            lhs = jnp.concatenate([sc_ref[rs, sl], qd_ref[rs, sl]], axis=1)
            rhs = jnp.concatenate([v_ref[rs, sl], sprev_ref[c * RET_HEADS + hh]], axis=0)
            o_h = jnp.dot(lhs, rhs, preferred_element_type=F32)
            mix_ref[rs, D_POOL + hh * RET_DK:D_POOL + (hh + 1) * RET_DK] = _rms(o_h)

    for r0, r1 in tail_blocks:
        mixb = (mix_ref[r0:r1, :] * sg_ref[r0:r1, :]).astype(BF16)
        y = jnp.dot(mixb, wo_ref[...], preferred_element_type=F32)
        xo_ref[0, r0:r1, :] = x_ref[0, r0:r1, :] + gres * (_rms(y) * gpost_ref[...])


def _prompt_layer(l, x, mod, g_pre, g_post, w_in, w_pool, pool_scale, w_o, tabs):
    b, seq, d = x.shape
    tile = PROMPT_TILE
    cs, sn, dmask, qdec, kdec, gc = tabs
    layer = lambda *_: (l, 0, 0)
    kern = functools.partial(_prompt_kernel, tile=tile)
    return pl.pallas_call(
        kern,
        grid=(b, seq // tile),
        in_specs=[
            pl.BlockSpec((1, tile, d), lambda i, t: (i, t, 0)),
            pl.BlockSpec((None, b, 3 * d), layer),
            pl.BlockSpec((None, 1, d), layer),
            pl.BlockSpec((None, 1, d), layer),
            pl.BlockSpec((None, d, D_IN), layer, pipeline_mode=pl.Buffered(1)),
            pl.BlockSpec((None, len(POOL_WINDOWS), POOL_CG, POOL_CG), lambda *_: (l, 0, 0, 0)),
            pl.BlockSpec((None, 1, D_POOL), layer),
            pl.BlockSpec((None, d, d), layer, pipeline_mode=pl.Buffered(1)),
            pl.BlockSpec((tile, 128), lambda i, t: (t, 0)),
            pl.BlockSpec((tile, 128), lambda i, t: (t, 0)),
            _const_spec(dmask.shape), _const_spec(qdec.shape), _const_spec(kdec.shape), _const_spec(gc.shape),
        ],
        out_specs=[
            pl.BlockSpec((1, tile, d), lambda i, t: (i, t, 0)),
            pl.BlockSpec((1, RET_HEADS, RET_DK, RET_DK), lambda i, t: (i, 0, 0, 0)),
            pl.BlockSpec((1, HIST, D_POOL), lambda i, t: (i, 0, 0)),
        ],
        out_shape=[
            jax.ShapeDtypeStruct((b, seq, d), F32),
            jax.ShapeDtypeStruct((b, RET_HEADS, RET_DK, RET_DK), F32),
            jax.ShapeDtypeStruct((b, HIST, D_POOL), F32),
        ],
        scratch_shapes=[
            pltpu.VMEM((tile, d), BF16),
            pltpu.VMEM((HIST + tile, D_POOL), F32),
            pltpu.VMEM((tile, D_POOL), BF16),
            pltpu.VMEM((tile, D_RET), BF16),
            pltpu.VMEM((tile, D_RET), BF16),
            pltpu.VMEM((tile, D_RET), BF16),
            pltpu.VMEM((tile, D_RET), BF16),
            pltpu.VMEM((tile, D_RET), BF16),
            pltpu.VMEM((tile, D_RET), BF16),
            pltpu.VMEM((tile // RET_CHUNK * RET_HEADS, RET_DK, RET_DK), BF16),
            pltpu.VMEM((tile, d), F32),
            pltpu.VMEM((tile, d), F32),
        ],
        compiler_params=pltpu.CompilerParams(
            dimension_semantics=("arbitrary", "arbitrary"), vmem_limit_bytes=VMEM_LIMIT),
        name=f"prompt_layer{l}",
    )(x, mod, g_pre, g_post, w_in, w_pool, pool_scale, w_o, cs, sn, dmask, qdec, kdec, gc)


def _sample_kernel(x_ref, mod_ref, gpre_ref, gpost_ref, win_ref, wpool_ref, pscale_ref, wo_ref,
                   cs_ref, sn_ref, dmask_ref, qdec_ref, kdec_ref, gc_ref, s0_ref, hist_ref,
                   xo_ref, s_ref, nb_ref,
                   ubuf_ref, mix_ref, *, bt, dec_seq):
    n = bt * dec_seq
    pool_buf = hist_ref.shape[1]
    rows = pl.ds(pl.multiple_of(pl.program_id(1) * bt, bt), bt)

    @pl.when(pl.program_id(0) == 0)
    def _():
        xo_ref[rows] = x_ref[...]

    m = mod_ref[...]
    shift, scl, gres = m[:, :, 0:D_MODEL], m[:, :, D_MODEL:2 * D_MODEL], m[:, :, 2 * D_MODEL:]

    x3 = xo_ref[rows]
    h3 = _rms(x3) * gpre_ref[...][None]
    h3 = h3 * (1.0 + scl) + shift
    hb = h3.reshape(n, D_MODEL).astype(BF16)

    u = jnp.dot(hb, win_ref[:, 0:D_POOL], preferred_element_type=F32)
    ubuf_ref[:, HIST - pool_buf:HIST, :] = hist_ref[...]
    ubuf_ref[:, HIST:HIST + dec_seq, :] = u.reshape(bt, dec_seq, D_POOL)
    for g, w in enumerate(POOL_WINDOWS):
        lo, hi = g * POOL_CG, (g + 1) * POOL_CG
        u_g = ubuf_ref[:, HIST:HIST + dec_seq, lo:hi]
        acc = u_g
        for k in range(1, w):
            acc = acc + ubuf_ref[:, HIST - k:HIST - k + dec_seq, lo:hi]
        m_g = (acc / float(w) - u_g).reshape(n, POOL_CG)
        y_g = jnp.dot(m_g.astype(BF16), wpool_ref[g], preferred_element_type=F32)
        mix_ref[:, lo:hi] = y_g * pscale_ref[:, lo:hi]
    nb_ref[...] = ubuf_ref[:, HIST + dec_seq - pool_buf:HIST + dec_seq, :]

    o1, o2, o3, o4 = D_POOL, D_POOL + D_RET, D_POOL + 2 * D_RET, D_POOL + 3 * D_RET
    zq = jnp.dot(hb, win_ref[:, o1:o2], preferred_element_type=F32).reshape(bt, dec_seq, D_RET)
    zk = jnp.dot(hb, win_ref[:, o2:o3], preferred_element_type=F32).reshape(bt, dec_seq, D_RET)
    zv = jnp.dot(hb, win_ref[:, o3:o4], preferred_element_type=F32).reshape(bt, dec_seq, D_RET)
    cs, sn = cs_ref[...][None], sn_ref[...][None]
    for hh in range(RET_HEADS):
        sl = slice(hh * RET_DK, (hh + 1) * RET_DK)
        qr = _rope(zq[:, :, sl], cs, sn)
        kr = _rope(zk[:, :, sl], cs, sn) * K_SCALE
        v_h = zv[:, :, sl]
        s_prev = s0_ref[:, hh]
        scores = jnp.einsum("bid,bjd->bij", qr, kr, preferred_element_type=F32) * dmask_ref[hh][None]
        intra = jnp.einsum("bij,bje->bie", scores, v_h, preferred_element_type=F32)
        cross = jnp.einsum("bid,bde->bie", qr * qdec_ref[:, sl][None], s_prev, preferred_element_type=F32)
        kv = jnp.einsum("bjd,bje->bde", kr * kdec_ref[:, sl][None], v_h, preferred_element_type=F32)
        s_ref[:, hh] = gc_ref[0:1, sl][None] * s_prev + kv
        mix_ref[:, D_POOL + hh * RET_DK:D_POOL + (hh + 1) * RET_DK] = _rms(intra + cross).reshape(n, RET_DK)

    zg = jnp.dot(hb, win_ref[:, o4:], preferred_element_type=F32)
    mixb = (mix_ref[...] * (zg * jax.nn.sigmoid(zg))).astype(BF16)
    y = jnp.dot(mixb, wo_ref[...], preferred_element_type=F32)
    yn = (_rms(y) * gpost_ref[...]).reshape(bt, dec_seq, D_MODEL)
    xo_ref[rows] = x3 + gres * yn


def _sample_layers(x, mod, g_pre, g_post, w_in, w_pool, pool_scale, w_o, tabs, state_ret, state_pool):
    b, dec_seq, d = x.shape
    depth, _, pool_buf, _ = state_pool.shape
    bt = SAMPLE_BT
    cs, sn, dmask, qdec, kdec, gc = tabs
    layer = lambda l, i: (l, 0, 0)
    kern = functools.partial(_sample_kernel, bt=bt, dec_seq=dec_seq)
    return pl.pallas_call(
        kern,
        grid=(depth, b // bt),
        in_specs=[
            pl.BlockSpec((bt, dec_seq, d), lambda l, i: (i, 0, 0)),
            pl.BlockSpec((None, bt, 1, 3 * d), lambda l, i: (l, i, 0, 0)),
            pl.BlockSpec((None, 1, d), layer),
            pl.BlockSpec((None, 1, d), layer),
            pl.BlockSpec((None, d, D_IN), layer),
            pl.BlockSpec((None, len(POOL_WINDOWS), POOL_CG, POOL_CG), lambda l, i: (l, 0, 0, 0)),
            pl.BlockSpec((None, 1, D_POOL), layer),
            pl.BlockSpec((None, d, d), layer),
            _const_spec(cs.shape), _const_spec(sn.shape),
            _const_spec(dmask.shape), _const_spec(qdec.shape), _const_spec(kdec.shape), _const_spec(gc.shape),
            pl.BlockSpec((None, bt, RET_HEADS, RET_DK, RET_DK), lambda l, i: (l, i, 0, 0, 0)),
            pl.BlockSpec((None, bt, pool_buf, D_POOL), lambda l, i: (l, i, 0, 0)),
        ],
        out_specs=[
            pl.BlockSpec((b, dec_seq, d), lambda l, i: (0, 0, 0)),
            pl.BlockSpec((None, bt, RET_HEADS, RET_DK, RET_DK), lambda l, i: (l, i, 0, 0, 0)),
            pl.BlockSpec((None, bt, pool_buf, D_POOL), lambda l, i: (l, i, 0, 0)),
        ],
        out_shape=[
            jax.ShapeDtypeStruct((b, dec_seq, d), F32),
            jax.ShapeDtypeStruct((depth, b, RET_HEADS, RET_DK, RET_DK), F32),
            jax.ShapeDtypeStruct((depth, b, pool_buf, D_POOL), F32),
        ],
        scratch_shapes=[
            pltpu.VMEM((bt, HIST + dec_seq, D_POOL), F32),
            pltpu.VMEM((bt * dec_seq, d), F32),
        ],
        compiler_params=pltpu.CompilerParams(
            dimension_semantics=("arbitrary", "arbitrary"), vmem_limit_bytes=VMEM_LIMIT),
        name="sample_layers",
    )(x, mod, g_pre, g_post, w_in, w_pool, pool_scale, w_o, cs, sn, dmask, qdec, kdec, gc, state_ret, state_pool)


def kernel(x_prompt, x_sample, c_prompt, c_sample, state_ret, state_pool, w_ada, b_ada,
           g_pre, g_post, w_in, w_pool, pool_scale, w_o):
    depth = w_in.shape[0]
    bp, seq, d = x_prompt.shape
    bs, dec_seq, _ = x_sample.shape
    assert d == D_MODEL and seq % PROMPT_TILE == 0 and bs % SAMPLE_BT == 0 and dec_seq == 8

    (csp, snp, css, sns, dmp, qdp, kdp, gcp, dms, qds, kds, gcs) = _make_tables(seq, dec_seq, PAST_LEN)
    tabs_p = (csp, snp, dmp, qdp, kdp, gcp)
    tabs_s = (css, sns, dms, qds, kds, gcs)

    mod_s, mod_p = _ada_mod(c_sample, c_prompt, w_ada, b_ada)
    mod_s = mod_s.reshape(depth, bs, 1, 3 * d)

    w_in_b = w_in.astype(BF16)
    w_o_b = w_o.astype(BF16)
    w_pool_b = w_pool.astype(BF16)
    g_pre3 = g_pre.reshape(depth, 1, d)
    g_post3 = g_post.reshape(depth, 1, d)
    pscale3 = pool_scale.reshape(depth, 1, D_POOL)
    hs, ret_s, pool_s = _sample_layers(x_sample, mod_s, g_pre3, g_post3, w_in_b, w_pool_b, pscale3, w_o_b,
                                       tabs_s, state_ret, state_pool)
    hp = x_prompt
    ret_p, pool_p = [], []
    for l in range(depth):
        hp, sp, nbp = _prompt_layer(l, hp, mod_p, g_pre3, g_post3, w_in_b, w_pool_b, pscale3, w_o_b, tabs_p)
        ret_p.append(sp)
        pool_p.append(nbp[:, 1:])
    return (hp, hs, jnp.stack(ret_p), jnp.stack(pool_p), ret_s, pool_s)
```

```python
import functools

import jax
import jax.numpy as jnp
from jax import lax
from jax.experimental import pallas as pl
from jax.experimental.pallas import tpu as pltpu

F32 = jnp.float32
BF16 = jnp.bfloat16

D_MODEL = 1024
D_POOL = 512
D_RET = 512
POOL_WINDOWS = (2, 4, 8, 16)
POOL_CG = 128
HIST = 16
RET_HEADS = 4
RET_DK = 128
RET_CHUNK = 128
ROPE_BASE = 10000.0
PAST_LEN = 16384
D_IN = D_POOL + 3 * D_RET + D_MODEL
EPS = 1e-6
K_SCALE = RET_DK ** -0.5

PROMPT_TILE = 1024
PROMPT_BLOCK = 256
SAMPLE_BT = 16
ADA_COLS = 768
VMEM_LIMIT = 56 * 1024 * 1024


def _const_spec(shape):
    return pl.BlockSpec(shape, lambda *_: (0,) * len(shape))


def _decay_tables(lg_lane, lg_row, c, dmask_ref, qdec_ref, kdec_ref, gc_ref):
    idx = lax.broadcasted_iota(jnp.int32, (c, D_RET), 0).astype(F32)
    qdec_ref[...] = jnp.exp((idx + 1.0) * lg_lane)
    kdec_ref[...] = jnp.exp((c - 1.0 - idx) * lg_lane)
    gc_ref[...] = jnp.exp(jnp.full((8, D_RET), float(c), F32) * lg_lane)
    ii = lax.broadcasted_iota(jnp.int32, (c, c), 0)
    jj = lax.broadcasted_iota(jnp.int32, (c, c), 1)
    diff = ii - jj
    for h in range(RET_HEADS):
        dec = jnp.exp(jnp.maximum(diff, 0).astype(F32) * lg_row[h])
        dmask_ref[h] = jnp.where(diff >= 0, dec, 0.0)


def _tables_kernel(inv_ref, csp_ref, snp_ref, css_ref, sns_ref,
                   dmp_ref, qdp_ref, kdp_ref, gcp_ref,
                   dms_ref, qds_ref, kds_ref, gcs_ref, *, seq, dec_seq, past_len):
    inv = inv_ref[...]
    lane = lax.broadcasted_iota(jnp.int32, (1, 2 * 64), 1)
    sign = jnp.where(lane < 64, -1.0, 1.0).astype(F32)

    def rope(n, start, cs_ref, sn_ref):
        pos = (lax.broadcasted_iota(jnp.int32, (n, 128), 0) + start).astype(F32)
        ang = pos * inv
        cs_ref[...] = jnp.cos(ang)
        sn_ref[...] = jnp.sin(ang) * sign

    rope(seq, 0, csp_ref, snp_ref)
    rope(dec_seq, past_len, css_ref, sns_ref)

    head_lane = (lax.broadcasted_iota(jnp.int32, (1, D_RET), 1) // RET_DK).astype(F32)
    lg_lane = jnp.log(1.0 - jnp.exp2(-5.0 - head_lane))
    for c, refs in ((RET_CHUNK, (dmp_ref, qdp_ref, kdp_ref, gcp_ref)),
                    (dec_seq, (dms_ref, qds_ref, kds_ref, gcs_ref))):
        lg_row = [jnp.log(1.0 - jnp.exp2(jnp.full((1, c), -5.0 - h, F32))) for h in range(RET_HEADS)]
        _decay_tables(lg_lane, lg_row, c, *refs)


def _make_tables(seq, dec_seq, past_len):
    half = RET_DK // 2
    inv = 1.0 / (ROPE_BASE ** (jnp.arange(half, dtype=F32) / half))
    inv2 = jnp.concatenate([inv, inv])[None, :]
    c, cs = RET_CHUNK, dec_seq
    out_shape = (
        jax.ShapeDtypeStruct((seq, 128), F32), jax.ShapeDtypeStruct((seq, 128), F32),
        jax.ShapeDtypeStruct((dec_seq, 128), F32), jax.ShapeDtypeStruct((dec_seq, 128), F32),
        jax.ShapeDtypeStruct((RET_HEADS, c, c), F32), jax.ShapeDtypeStruct((c, D_RET), F32),
        jax.ShapeDtypeStruct((c, D_RET), F32), jax.ShapeDtypeStruct((8, D_RET), F32),
        jax.ShapeDtypeStruct((RET_HEADS, cs, cs), F32), jax.ShapeDtypeStruct((cs, D_RET), F32),
        jax.ShapeDtypeStruct((cs, D_RET), F32), jax.ShapeDtypeStruct((8, D_RET), F32),
    )
    return pl.pallas_call(
        functools.partial(_tables_kernel, seq=seq, dec_seq=dec_seq, past_len=past_len),
        out_shape=out_shape,
        name="tables",
    )(inv2)


def _ada_kernel(cs_ref, cp_ref, w_ref, b_ref, os_ref, op_ref):
    w = w_ref[0].astype(BF16)
    for c_ref, o_ref in ((cs_ref, os_ref), (cp_ref, op_ref)):
        c = c_ref[...]
        a = (c * jax.nn.sigmoid(c)).astype(BF16)
        o_ref[0] = jnp.dot(a, w, preferred_element_type=F32) + b_ref[0]


def _ada_mod(c_sample, c_prompt, w_ada, b_ada):
    depth, d, d3 = w_ada.shape
    bs, bp = c_sample.shape[0], c_prompt.shape[0]
    return pl.pallas_call(
        _ada_kernel,
        grid=(depth, d3 // ADA_COLS),
        in_specs=[
            pl.BlockSpec((bs, d), lambda l, j: (0, 0)),
            pl.BlockSpec((bp, d), lambda l, j: (0, 0)),
            pl.BlockSpec((1, d, ADA_COLS), lambda l, j: (l, 0, j)),
            pl.BlockSpec((1, 1, ADA_COLS), lambda l, j: (l, 0, j)),
        ],
        out_specs=[pl.BlockSpec((1, bs, ADA_COLS), lambda l, j: (l, 0, j)),
                   pl.BlockSpec((1, bp, ADA_COLS), lambda l, j: (l, 0, j))],
        out_shape=[jax.ShapeDtypeStruct((depth, bs, d3), F32),
                   jax.ShapeDtypeStruct((depth, bp, d3), F32)],
        compiler_params=pltpu.CompilerParams(dimension_semantics=("arbitrary", "arbitrary")),
        name="ada_mod",
    )(c_sample, c_prompt, w_ada, b_ada.reshape(depth, 1, d3))


def _rope(x, cs, sn):
    return x * cs + pltpu.roll(x, RET_DK // 2, x.ndim - 1) * sn


def _rms(x):
    return x * lax.rsqrt(jnp.mean(x * x, axis=-1, keepdims=True) + EPS)


def _prompt_kernel(x_ref, mod_ref, gpre_ref, gpost_ref, win_ref, wpool_ref, pscale_ref, wo_ref,
                   cs_ref, sn_ref, dmask_ref, qdec_ref, kdec_ref, gc_ref,
                   xo_ref, s_ref, nb_ref,
                   hb_ref, ubuf_ref, q_ref, qd_ref, k_ref, kd_ref, v_ref, sc_ref, sprev_ref, mix_ref, sg_ref,
                   *, tile):
    t = pl.program_id(1)
    n_chunks = tile // RET_CHUNK
    blocks = [(r0, r0 + PROMPT_BLOCK) for r0 in range(0, tile, PROMPT_BLOCK)]
    o1, o2, o3, o4 = D_POOL, D_POOL + D_RET, D_POOL + 2 * D_RET, D_POOL + 3 * D_RET
    heads = [slice(hh * RET_DK, (hh + 1) * RET_DK) for hh in range(RET_HEADS)]
    chunks = [slice(c * RET_CHUNK, (c + 1) * RET_CHUNK) for c in range(n_chunks)]

    @pl.when(t == 0)
    def _():
        s_ref[...] = jnp.zeros_like(s_ref)
        ubuf_ref[0:HIST, :] = jnp.zeros((HIST, D_POOL), F32)

    m = mod_ref[pl.ds(pl.program_id(0), 1), :]
    shift, scl, gres = m[:, 0:D_MODEL], m[:, D_MODEL:2 * D_MODEL], m[:, 2 * D_MODEL:]
    pre_gain = gpre_ref[...] * (1.0 + scl)
    post_gain = gres * gpost_ref[...]

    def proj(r0, r1, c0, c1):
        return jnp.dot(hb_ref[r0:r1, :], win_ref[:, c0:c1], preferred_element_type=F32)

    zqk, kvs = {}, {}

    def pre_norm(j):
        r0, r1 = blocks[j]
        hb_ref[r0:r1, :] = (_rms(x_ref[0, r0:r1, :]) * pre_gain + shift).astype(BF16)

    def project(j):
        r0, r1 = blocks[j]
        ubuf_ref[HIST + r0:HIST + r1, :] = proj(r0, r1, 0, o1)
        zqk[j] = (proj(r0, r1, o1, o2), proj(r0, r1, o2, o3))
        v_ref[r0:r1, :] = proj(r0, r1, o3, o4).astype(BF16)

    def mixer_inputs(j):
        r0, r1 = blocks[j]
        row = lax.broadcasted_iota(jnp.int32, (r1 - r0, 1), 0) + (t * tile + r0)
        pooled = []
        for g, w in enumerate(POOL_WINDOWS):
            lo, hi = g * POOL_CG, (g + 1) * POOL_CG
            u_ext = ubuf_ref[r0:r1 + HIST, lo:hi]
            acc = u_ext
            k = 1
            while k < w:
                acc = acc + pltpu.roll(acc, k, 0)
                k *= 2
            cnt = jnp.minimum(row + 1, w).astype(F32)
            pooled.append((acc[HIST:] / cnt - u_ext[HIST:]).astype(BF16))
        zq, zk = zqk.pop(j)
        cs, sn = cs_ref[r0:r1, :], sn_ref[r0:r1, :]
        for sl in heads:
            qr = _rope(zq[:, sl], cs, sn)
            kr = _rope(zk[:, sl], cs, sn) * K_SCALE
            q_ref[r0:r1, sl] = qr.astype(BF16)
            k_ref[r0:r1, sl] = kr.astype(BF16)
            for c in range(r0 // RET_CHUNK, r1 // RET_CHUNK):
                rs, ls = chunks[c], slice(c * RET_CHUNK - r0, (c + 1) * RET_CHUNK - r0)
                qd_ref[rs, sl] = (qr[ls] * qdec_ref[:, sl]).astype(BF16)
                kd_ref[rs, sl] = (kr[ls] * kdec_ref[:, sl]).astype(BF16)
        return pooled

    def small_matmuls(j, pooled):
        r0, r1 = blocks[j]
        for g in range(len(POOL_WINDOWS)):
            lo, hi = g * POOL_CG, (g + 1) * POOL_CG
            y_g = jnp.dot(pooled[g], wpool_ref[g], preferred_element_type=F32)
            mix_ref[r0:r1, lo:hi] = y_g * pscale_ref[:, lo:hi]
        for c in range(r0 // RET_CHUNK, r1 // RET_CHUNK):
            rs = chunks[c]
            for hh, sl in enumerate(heads):
                scores = lax.dot_general(q_ref[rs, sl], k_ref[rs, sl], (((1,), (1,)), ((), ())),
                                         preferred_element_type=F32)
                sc_ref[rs, sl] = (scores * dmask_ref[hh]).astype(BF16)
                kvs[c, hh] = lax.dot_general(kd_ref[rs, sl], v_ref[rs, sl], (((0,), (0,)), ((), ())),
                                             preferred_element_type=F32)

    def gate(j):
        r0, r1 = blocks[j]
        zg = proj(r0, r1, o4, D_IN)
        sg_ref[r0:r1, :] = zg * jax.nn.sigmoid(zg)

    def state_recurrence():
        for hh, sl in enumerate(heads):
            s = s_ref[0, hh]
            for c in range(n_chunks):
                sprev_ref[c * RET_HEADS + hh] = s.astype(BF16)
                s = gc_ref[0:1, sl] * s + kvs.pop((c, hh))
            s_ref[0, hh] = s

    def retention_out(j):
        r0, r1 = blocks[j]
        for c in range(r0 // RET_CHUNK, r1 // RET_CHUNK):
            rs = chunks[c]
            for hh, sl in enumerate(heads):
                lhs = jnp.concatenate([sc_ref[rs, sl], qd_ref[rs, sl]], axis=1)
                rhs = jnp.concatenate([v_ref[rs, sl], sprev_ref[c * RET_HEADS + hh]], axis=0)
                o_h = jnp.dot(lhs, rhs, preferred_element_type=F32)
                mix_ref[rs, D_POOL + hh * RET_DK:D_POOL + (hh + 1) * RET_DK] = _rms(o_h)

    def out_proj(j):
        r0, r1 = blocks[j]
        mixb = (mix_ref[r0:r1, :] * sg_ref[r0:r1, :]).astype(BF16)
        y = jnp.dot(mixb, wo_ref[...], preferred_element_type=F32)
        xo_ref[0, r0:r1, :] = x_ref[0, r0:r1, :] + _rms(y) * post_gain

    n = len(blocks)
    pre_norm(0)
    project(0)
    for j in range(n):
        if j + 1 < n:
            pre_norm(j + 1)
        pooled = mixer_inputs(j)
        if j + 1 < n:
            project(j + 1)
        else:
            nb_ref[0] = ubuf_ref[tile:tile + HIST, :]
            ubuf_ref[0:HIST, :] = ubuf_ref[tile:tile + HIST, :]
            gate(0)
        small_matmuls(j, pooled)
    gate(1)
    state_recurrence()

    gates_left = list(range(2, n))
    for j in range(n):
        retention_out(j)
        if gates_left:
            gate(gates_left.pop(0))
        if j >= 1:
            out_proj(j - 1)
    out_proj(n - 1)


def _prompt_layer(l, x, mod, g_pre, g_post, w_in, w_pool, pool_scale, w_o, tabs):
    b, seq, d = x.shape
    tile = PROMPT_TILE
    cs, sn, dmask, qdec, kdec, gc = tabs
    layer = lambda *_: (l, 0, 0)
    kern = functools.partial(_prompt_kernel, tile=tile)
    return pl.pallas_call(
        kern,
        grid=(b, seq // tile),
        in_specs=[
            pl.BlockSpec((1, tile, d), lambda i, t: (i, t, 0)),
            pl.BlockSpec((None, b, 3 * d), layer),
            pl.BlockSpec((None, 1, d), layer),
            pl.BlockSpec((None, 1, d), layer),
            pl.BlockSpec((None, d, D_IN), layer, pipeline_mode=pl.Buffered(1)),
            pl.BlockSpec((None, len(POOL_WINDOWS), POOL_CG, POOL_CG), lambda *_: (l, 0, 0, 0)),
            pl.BlockSpec((None, 1, D_POOL), layer),
            pl.BlockSpec((None, d, d), layer, pipeline_mode=pl.Buffered(1)),
            pl.BlockSpec((tile, 128), lambda i, t: (t, 0)),
            pl.BlockSpec((tile, 128), lambda i, t: (t, 0)),
            _const_spec(dmask.shape), _const_spec(qdec.shape), _const_spec(kdec.shape), _const_spec(gc.shape),
        ],
        out_specs=[
            pl.BlockSpec((1, tile, d), lambda i, t: (i, t, 0)),
            pl.BlockSpec((1, RET_HEADS, RET_DK, RET_DK), lambda i, t: (i, 0, 0, 0)),
            pl.BlockSpec((1, HIST, D_POOL), lambda i, t: (i, 0, 0)),
        ],
        out_shape=[
            jax.ShapeDtypeStruct((b, seq, d), F32),
            jax.ShapeDtypeStruct((b, RET_HEADS, RET_DK, RET_DK), F32),
            jax.ShapeDtypeStruct((b, HIST, D_POOL), F32),
        ],
        scratch_shapes=[
            pltpu.VMEM((tile, d), BF16),
            pltpu.VMEM((HIST + tile, D_POOL), F32),
            pltpu.VMEM((tile, D_RET), BF16),
            pltpu.VMEM((tile, D_RET), BF16),
            pltpu.VMEM((tile, D_RET), BF16),
            pltpu.VMEM((tile, D_RET), BF16),
            pltpu.VMEM((tile, D_RET), BF16),
            pltpu.VMEM((tile, D_RET), BF16),
            pltpu.VMEM((tile // RET_CHUNK * RET_HEADS, RET_DK, RET_DK), BF16),
            pltpu.VMEM((tile, d), F32),
            pltpu.VMEM((tile, d), F32),
        ],
        compiler_params=pltpu.CompilerParams(
            dimension_semantics=("arbitrary", "arbitrary"), vmem_limit_bytes=VMEM_LIMIT),
        name=f"prompt_layer{l}",
    )(x, mod, g_pre, g_post, w_in, w_pool, pool_scale, w_o, cs, sn, dmask, qdec, kdec, gc)


def _sample_kernel(x_ref, mod_ref, gpre_ref, gpost_ref, win_ref, wpool_ref, pscale_ref, wo_ref,
                   cs_ref, sn_ref, dmask_ref, qdec_ref, kdec_ref, gc_ref, s0_ref, hist_ref,
                   xo_ref, s_ref, nb_ref,
                   ubuf_ref, mix_ref, *, bt, dec_seq):
    n = bt * dec_seq
    pool_buf = hist_ref.shape[1]
    rows = pl.ds(pl.multiple_of(pl.program_id(1) * bt, bt), bt)

    @pl.when(pl.program_id(0) == 0)
    def _():
        xo_ref[rows] = x_ref[...]

    m = mod_ref[...]
    shift, scl, gres = m[:, :, 0:D_MODEL], m[:, :, D_MODEL:2 * D_MODEL], m[:, :, 2 * D_MODEL:]

    x3 = xo_ref[rows]
    h3 = _rms(x3) * gpre_ref[...][None]
    h3 = h3 * (1.0 + scl) + shift
    hb = h3.reshape(n, D_MODEL).astype(BF16)

    u = jnp.dot(hb, win_ref[:, 0:D_POOL], preferred_element_type=F32)
    ubuf_ref[:, HIST - pool_buf:HIST, :] = hist_ref[...]
    ubuf_ref[:, HIST:HIST + dec_seq, :] = u.reshape(bt, dec_seq, D_POOL)
    for g, w in enumerate(POOL_WINDOWS):
        lo, hi = g * POOL_CG, (g + 1) * POOL_CG
        u_g = ubuf_ref[:, HIST:HIST + dec_seq, lo:hi]
        acc = u_g
        for k in range(1, w):
            acc = acc + ubuf_ref[:, HIST - k:HIST - k + dec_seq, lo:hi]
        m_g = (acc / float(w) - u_g).reshape(n, POOL_CG)
        y_g = jnp.dot(m_g.astype(BF16), wpool_ref[g], preferred_element_type=F32)
        mix_ref[:, lo:hi] = y_g * pscale_ref[:, lo:hi]
    nb_ref[...] = ubuf_ref[:, HIST + dec_seq - pool_buf:HIST + dec_seq, :]

    o1, o2, o3, o4 = D_POOL, D_POOL + D_RET, D_POOL + 2 * D_RET, D_POOL + 3 * D_RET
    zq = jnp.dot(hb, win_ref[:, o1:o2], preferred_element_type=F32).reshape(bt, dec_seq, D_RET)
    zk = jnp.dot(hb, win_ref[:, o2:o3], preferred_element_type=F32).reshape(bt, dec_seq, D_RET)
    zv = jnp.dot(hb, win_ref[:, o3:o4], preferred_element_type=F32).reshape(bt, dec_seq, D_RET)
    cs, sn = cs_ref[...][None], sn_ref[...][None]
    for hh in range(RET_HEADS):
        sl = slice(hh * RET_DK, (hh + 1) * RET_DK)
        qr = _rope(zq[:, :, sl], cs, sn)
        kr = _rope(zk[:, :, sl], cs, sn) * K_SCALE
        v_h = zv[:, :, sl]
        s_prev = s0_ref[:, hh]
        scores = jnp.einsum("bid,bjd->bij", qr, kr, preferred_element_type=F32) * dmask_ref[hh][None]
        intra = jnp.einsum("bij,bje->bie", scores, v_h, preferred_element_type=F32)
        cross = jnp.einsum("bid,bde->bie", qr * qdec_ref[:, sl][None], s_prev, preferred_element_type=F32)
        kv = jnp.einsum("bjd,bje->bde", kr * kdec_ref[:, sl][None], v_h, preferred_element_type=F32)
        s_ref[:, hh] = gc_ref[0:1, sl][None] * s_prev + kv
        mix_ref[:, D_POOL + hh * RET_DK:D_POOL + (hh + 1) * RET_DK] = _rms(intra + cross).reshape(n, RET_DK)

    zg = jnp.dot(hb, win_ref[:, o4:], preferred_element_type=F32)
    mixb = (mix_ref[...] * (zg * jax.nn.sigmoid(zg))).astype(BF16)
    y = jnp.dot(mixb, wo_ref[...], preferred_element_type=F32)
    yn = (_rms(y) * gpost_ref[...]).reshape(bt, dec_seq, D_MODEL)
    xo_ref[rows] = x3 + gres * yn


def _sample_layers(x, mod, g_pre, g_post, w_in, w_pool, pool_scale, w_o, tabs, state_ret, state_pool):
    b, dec_seq, d = x.shape
    depth, _, pool_buf, _ = state_pool.shape
    bt = SAMPLE_BT
    cs, sn, dmask, qdec, kdec, gc = tabs
    layer = lambda l, i: (l, 0, 0)
    kern = functools.partial(_sample_kernel, bt=bt, dec_seq=dec_seq)
    return pl.pallas_call(
        kern,
        grid=(depth, b // bt),
        in_specs=[
            pl.BlockSpec((bt, dec_seq, d), lambda l, i: (i, 0, 0)),
            pl.BlockSpec((None, bt, 1, 3 * d), lambda l, i: (l, i, 0, 0)),
            pl.BlockSpec((None, 1, d), layer),
            pl.BlockSpec((None, 1, d), layer),
            pl.BlockSpec((None, d, D_IN), layer),
            pl.BlockSpec((None, len(POOL_WINDOWS), POOL_CG, POOL_CG), lambda l, i: (l, 0, 0, 0)),
            pl.BlockSpec((None, 1, D_POOL), layer),
            pl.BlockSpec((None, d, d), layer),
            _const_spec(cs.shape), _const_spec(sn.shape),
            _const_spec(dmask.shape), _const_spec(qdec.shape), _const_spec(kdec.shape), _const_spec(gc.shape),
            pl.BlockSpec((None, bt, RET_HEADS, RET_DK, RET_DK), lambda l, i: (l, i, 0, 0, 0)),
            pl.BlockSpec((None, bt, pool_buf, D_POOL), lambda l, i: (l, i, 0, 0)),
        ],
        out_specs=[
            pl.BlockSpec((b, dec_seq, d), lambda l, i: (0, 0, 0)),
            pl.BlockSpec((None, bt, RET_HEADS, RET_DK, RET_DK), lambda l, i: (l, i, 0, 0, 0)),
            pl.BlockSpec((None, bt, pool_buf, D_POOL), lambda l, i: (l, i, 0, 0)),
        ],
        out_shape=[
            jax.ShapeDtypeStruct((b, dec_seq, d), F32),
            jax.ShapeDtypeStruct((depth, b, RET_HEADS, RET_DK, RET_DK), F32),
            jax.ShapeDtypeStruct((depth, b, pool_buf, D_POOL), F32),
        ],
        scratch_shapes=[
            pltpu.VMEM((bt, HIST + dec_seq, D_POOL), F32),
            pltpu.VMEM((bt * dec_seq, d), F32),
        ],
        compiler_params=pltpu.CompilerParams(
            dimension_semantics=("arbitrary", "arbitrary"), vmem_limit_bytes=VMEM_LIMIT),
        name="sample_layers",
    )(x, mod, g_pre, g_post, w_in, w_pool, pool_scale, w_o, cs, sn, dmask, qdec, kdec, gc, state_ret, state_pool)


def kernel(x_prompt, x_sample, c_prompt, c_sample, state_ret, state_pool, w_ada, b_ada,
           g_pre, g_post, w_in, w_pool, pool_scale, w_o):
    depth = w_in.shape[0]
    bp, seq, d = x_prompt.shape
    bs, dec_seq, _ = x_sample.shape
    assert d == D_MODEL and seq % PROMPT_TILE == 0 and bs % SAMPLE_BT == 0 and dec_seq == 8

    (csp, snp, css, sns, dmp, qdp, kdp, gcp, dms, qds, kds, gcs) = _make_tables(seq, dec_seq, PAST_LEN)
    tabs_p = (csp, snp, dmp, qdp, kdp, gcp)
    tabs_s = (css, sns, dms, qds, kds, gcs)

    mod_s, mod_p = _ada_mod(c_sample, c_prompt, w_ada, b_ada)
    mod_s = mod_s.reshape(depth, bs, 1, 3 * d)

    w_in_b = w_in.astype(BF16)
    w_o_b = w_o.astype(BF16)
    w_pool_b = w_pool.astype(BF16)
    g_pre3 = g_pre.reshape(depth, 1, d)
    g_post3 = g_post.reshape(depth, 1, d)
    pscale3 = pool_scale.reshape(depth, 1, D_POOL)
    hs, ret_s, pool_s = _sample_layers(x_sample, mod_s, g_pre3, g_post3, w_in_b, w_pool_b, pscale3, w_o_b,
                                       tabs_s, state_ret, state_pool)
    hp = x_prompt
    ret_p, pool_p = [], []
    for l in range(depth):
        hp, sp, nbp = _prompt_layer(l, hp, mod_p, g_pre3, g_post3, w_in_b, w_pool_b, pscale3, w_o_b, tabs_p)
        ret_p.append(sp)
        pool_p.append(nbp[:, 1:])
    return (hp, hs, jnp.stack(ret_p), jnp.stack(pool_p), ret_s, pool_s)
```

```python
import functools

import jax
import jax.numpy as jnp
from jax import lax
from jax.experimental import pallas as pl
from jax.experimental.pallas import tpu as pltpu

F32 = jnp.float32
BF16 = jnp.bfloat16

D_MODEL = 1024
D_POOL = 512
D_RET = 512
POOL_WINDOWS = (2, 4, 8, 16)
POOL_CG = 128
HIST = 16
RET_HEADS = 4
RET_DK = 128
RET_CHUNK = 128
ROPE_BASE = 10000.0
PAST_LEN = 16384
D_IN = D_POOL + 3 * D_RET + D_MODEL
EPS = 1e-6
K_SCALE = RET_DK ** -0.5

PROMPT_TILE = 1024
PROMPT_BLOCK = 256
SAMPLE_BT = 16
ADA_COLS = 1536
VMEM_LIMIT = 56 * 1024 * 1024


def _const_spec(shape):
    return pl.BlockSpec(shape, lambda *_: (0,) * len(shape))


def _decay_tables(lg_lane, lg_row, c, dmask_ref, qdec_ref, kdec_ref, gc_ref):
    idx = lax.broadcasted_iota(jnp.int32, (c, D_RET), 0).astype(F32)
    qdec_ref[...] = jnp.exp((idx + 1.0) * lg_lane)
    kdec_ref[...] = jnp.exp((c - 1.0 - idx) * lg_lane)
    gc_ref[...] = jnp.exp(jnp.full((8, D_RET), float(c), F32) * lg_lane)
    ii = lax.broadcasted_iota(jnp.int32, (c, c), 0)
    jj = lax.broadcasted_iota(jnp.int32, (c, c), 1)
    diff = ii - jj
    for h in range(RET_HEADS):
        dec = jnp.exp(jnp.maximum(diff, 0).astype(F32) * lg_row[h])
        dmask_ref[h] = jnp.where(diff >= 0, dec, 0.0)


def _tables_kernel(inv_ref, csp_ref, snp_ref, css_ref, sns_ref,
                   dmp_ref, qdp_ref, kdp_ref, gcp_ref,
                   dms_ref, qds_ref, kds_ref, gcs_ref, *, seq, dec_seq, past_len):
    inv = inv_ref[...]
    lane = lax.broadcasted_iota(jnp.int32, (1, 2 * 64), 1)
    sign = jnp.where(lane < 64, -1.0, 1.0).astype(F32)

    def rope(n, start, cs_ref, sn_ref):
        pos = (lax.broadcasted_iota(jnp.int32, (n, 128), 0) + start).astype(F32)
        ang = pos * inv
        cs_ref[...] = jnp.cos(ang)
        sn_ref[...] = jnp.sin(ang) * sign

    rope(seq, 0, csp_ref, snp_ref)
    rope(dec_seq, past_len, css_ref, sns_ref)

    head_lane = (lax.broadcasted_iota(jnp.int32, (1, D_RET), 1) // RET_DK).astype(F32)
    lg_lane = jnp.log(1.0 - jnp.exp2(-5.0 - head_lane))
    for c, refs in ((RET_CHUNK, (dmp_ref, qdp_ref, kdp_ref, gcp_ref)),
                    (dec_seq, (dms_ref, qds_ref, kds_ref, gcs_ref))):
        lg_row = [jnp.log(1.0 - jnp.exp2(jnp.full((1, c), -5.0 - h, F32))) for h in range(RET_HEADS)]
        _decay_tables(lg_lane, lg_row, c, *refs)


def _make_tables(seq, dec_seq, past_len):
    half = RET_DK // 2
    inv = 1.0 / (ROPE_BASE ** (jnp.arange(half, dtype=F32) / half))
    inv2 = jnp.concatenate([inv, inv])[None, :]
    c, cs = RET_CHUNK, dec_seq
    out_shape = (
        jax.ShapeDtypeStruct((seq, 128), F32), jax.ShapeDtypeStruct((seq, 128), F32),
        jax.ShapeDtypeStruct((dec_seq, 128), F32), jax.ShapeDtypeStruct((dec_seq, 128), F32),
        jax.ShapeDtypeStruct((RET_HEADS, c, c), F32), jax.ShapeDtypeStruct((c, D_RET), F32),
        jax.ShapeDtypeStruct((c, D_RET), F32), jax.ShapeDtypeStruct((8, D_RET), F32),
        jax.ShapeDtypeStruct((RET_HEADS, cs, cs), F32), jax.ShapeDtypeStruct((cs, D_RET), F32),
        jax.ShapeDtypeStruct((cs, D_RET), F32), jax.ShapeDtypeStruct((8, D_RET), F32),
    )
    return pl.pallas_call(
        functools.partial(_tables_kernel, seq=seq, dec_seq=dec_seq, past_len=past_len),
        out_shape=out_shape,
        name="tables",
    )(inv2)


def _ada_kernel(cs_ref, cp_ref, w_ref, b_ref, os_ref, op_ref):
    w = w_ref[0].astype(BF16)
    for c_ref, o_ref in ((cs_ref, os_ref), (cp_ref, op_ref)):
        c = c_ref[...]
        a = (c * jax.nn.sigmoid(c)).astype(BF16)
        o_ref[0] = jnp.dot(a, w, preferred_element_type=F32) + b_ref[0]


def _ada_mod(c_sample, c_prompt, w_ada, b_ada):
    depth, d, d3 = w_ada.shape
    bs, bp = c_sample.shape[0], c_prompt.shape[0]
    return pl.pallas_call(
        _ada_kernel,
        grid=(depth, d3 // ADA_COLS),
        in_specs=[
            pl.BlockSpec((bs, d), lambda l, j: (0, 0)),
            pl.BlockSpec((bp, d), lambda l, j: (0, 0)),
            pl.BlockSpec((1, d, ADA_COLS), lambda l, j: (l, 0, j)),
            pl.BlockSpec((1, 1, ADA_COLS), lambda l, j: (l, 0, j)),
        ],
        out_specs=[pl.BlockSpec((1, bs, ADA_COLS), lambda l, j: (l, 0, j)),
                   pl.BlockSpec((1, bp, ADA_COLS), lambda l, j: (l, 0, j))],
        out_shape=[jax.ShapeDtypeStruct((depth, bs, d3), F32),
                   jax.ShapeDtypeStruct((depth, bp, d3), F32)],
        compiler_params=pltpu.CompilerParams(dimension_semantics=("arbitrary", "arbitrary")),
        name="ada_mod",
    )(c_sample, c_prompt, w_ada, b_ada.reshape(depth, 1, d3))


def _rope(x, cs, sn):
    return x * cs + pltpu.roll(x, RET_DK // 2, x.ndim - 1) * sn


def _rms(x):
    return x * lax.rsqrt(jnp.mean(x * x, axis=-1, keepdims=True) + EPS)


def _prompt_kernel(x_ref, mod_ref, gpre_ref, gpost_ref, win_ref, wpool_ref, pscale_ref, wo_ref,
                   cs_ref, sn_ref, dmask_ref, qdec_ref, kdec_ref, gc_ref,
                   xo_ref, s_ref, nb_ref,
                   hb_ref, ubuf_ref, q_ref, qd_ref, k_ref, kd_ref, v_ref, sc_ref, sprev_ref, mix_ref, sg_ref,
                   *, tile):
    t = pl.program_id(1)
    n_chunks = tile // RET_CHUNK
    blocks = [(r0, r0 + PROMPT_BLOCK) for r0 in range(0, tile, PROMPT_BLOCK)]
    o1, o2, o3, o4 = D_POOL, D_POOL + D_RET, D_POOL + 2 * D_RET, D_POOL + 3 * D_RET
    heads = [slice(hh * RET_DK, (hh + 1) * RET_DK) for hh in range(RET_HEADS)]
    chunks = [slice(c * RET_CHUNK, (c + 1) * RET_CHUNK) for c in range(n_chunks)]

    @pl.when(t == 0)
    def _():
        s_ref[...] = jnp.zeros_like(s_ref)
        ubuf_ref[0:HIST, :] = jnp.zeros((HIST, D_POOL), F32)

    m = mod_ref[pl.ds(pl.program_id(0), 1), :]
    shift, scl, gres = m[:, 0:D_MODEL], m[:, D_MODEL:2 * D_MODEL], m[:, 2 * D_MODEL:]
    pre_gain = gpre_ref[...] * (1.0 + scl)
    post_gain = gres * gpost_ref[...]

    def proj(r0, r1, c0, c1):
        return jnp.dot(hb_ref[r0:r1, :], win_ref[:, c0:c1], preferred_element_type=F32)

    zqk, kvs = {}, {}

    def pre_norm(j):
        r0, r1 = blocks[j]
        hb_ref[r0:r1, :] = (_rms(x_ref[0, r0:r1, :]) * pre_gain + shift).astype(BF16)

    def project(j):
        r0, r1 = blocks[j]
        ubuf_ref[HIST + r0:HIST + r1, :] = proj(r0, r1, 0, o1)
        zqk[j] = (proj(r0, r1, o1, o2), proj(r0, r1, o2, o3))
        v_ref[r0:r1, :] = proj(r0, r1, o3, o4).astype(BF16)

    def mixer_inputs(j):
        r0, r1 = blocks[j]
        row = lax.broadcasted_iota(jnp.int32, (r1 - r0, 1), 0) + (t * tile + r0)
        pooled = []
        for g, w in enumerate(POOL_WINDOWS):
            lo, hi = g * POOL_CG, (g + 1) * POOL_CG
            u_ext = ubuf_ref[r0:r1 + HIST, lo:hi]
            acc = u_ext
            k = 1
            while k < w:
                acc = acc + pltpu.roll(acc, k, 0)
                k *= 2
            cnt = jnp.minimum(row + 1, w).astype(F32)
            pooled.append((acc[HIST:] / cnt - u_ext[HIST:]).astype(BF16))
        zq, zk = zqk.pop(j)
        cs, sn = cs_ref[r0:r1, :], sn_ref[r0:r1, :]
        for sl in heads:
            qr = _rope(zq[:, sl], cs, sn)
            kr = _rope(zk[:, sl], cs, sn) * K_SCALE
            q_ref[r0:r1, sl] = qr.astype(BF16)
            k_ref[r0:r1, sl] = kr.astype(BF16)
            for c in range(r0 // RET_CHUNK, r1 // RET_CHUNK):
                rs, ls = chunks[c], slice(c * RET_CHUNK - r0, (c + 1) * RET_CHUNK - r0)
                qd_ref[rs, sl] = (qr[ls] * qdec_ref[:, sl]).astype(BF16)
                kd_ref[rs, sl] = (kr[ls] * kdec_ref[:, sl]).astype(BF16)
        return pooled

    def small_matmuls(j, pooled):
        r0, r1 = blocks[j]
        for g in range(len(POOL_WINDOWS)):
            lo, hi = g * POOL_CG, (g + 1) * POOL_CG
            y_g = jnp.dot(pooled[g], wpool_ref[g], preferred_element_type=F32)
            mix_ref[r0:r1, lo:hi] = y_g * pscale_ref[:, lo:hi]
        for c in range(r0 // RET_CHUNK, r1 // RET_CHUNK):
            rs = chunks[c]
            for hh, sl in enumerate(heads):
                scores = lax.dot_general(q_ref[rs, sl], k_ref[rs, sl], (((1,), (1,)), ((), ())),
                                         preferred_element_type=F32)
                sc_ref[rs, sl] = (scores * dmask_ref[hh]).astype(BF16)
                kvs[c, hh] = lax.dot_general(kd_ref[rs, sl], v_ref[rs, sl], (((0,), (0,)), ((), ())),
                                             preferred_element_type=F32)

    def gate(j):
        r0, r1 = blocks[j]
        zg = proj(r0, r1, o4, D_IN)
        sg_ref[r0:r1, :] = zg * jax.nn.sigmoid(zg)

    def state_recurrence():
        for hh, sl in enumerate(heads):
            s = s_ref[0, hh]
            for c in range(n_chunks):
                sprev_ref[c * RET_HEADS + hh] = s.astype(BF16)
                s = gc_ref[0:1, sl] * s + kvs.pop((c, hh))
            s_ref[0, hh] = s

    def retention_out(j):
        r0, r1 = blocks[j]
        for c in range(r0 // RET_CHUNK, r1 // RET_CHUNK):
            rs = chunks[c]
            for hh, sl in enumerate(heads):
                lhs = jnp.concatenate([sc_ref[rs, sl], qd_ref[rs, sl]], axis=1)
                rhs = jnp.concatenate([v_ref[rs, sl], sprev_ref[c * RET_HEADS + hh]], axis=0)
                o_h = jnp.dot(lhs, rhs, preferred_element_type=F32)
                mix_ref[rs, D_POOL + hh * RET_DK:D_POOL + (hh + 1) * RET_DK] = _rms(o_h)

    def out_proj(j):
        r0, r1 = blocks[j]
        mixb = (mix_ref[r0:r1, :] * sg_ref[r0:r1, :]).astype(BF16)
        y = jnp.dot(mixb, wo_ref[...], preferred_element_type=F32)
        xo_ref[0, r0:r1, :] = x_ref[0, r0:r1, :] + _rms(y) * post_gain

    n = len(blocks)
    pre_norm(0)
    project(0)
    for j in range(n):
        if j + 1 < n:
            pre_norm(j + 1)
        pooled = mixer_inputs(j)
        if j + 1 < n:
            project(j + 1)
        else:
            nb_ref[0] = ubuf_ref[tile:tile + HIST, :]
            ubuf_ref[0:HIST, :] = ubuf_ref[tile:tile + HIST, :]
            gate(0)
        small_matmuls(j, pooled)
    gate(1)
    state_recurrence()

    gates_left = list(range(2, n))
    for j in range(n):
        retention_out(j)
        if gates_left:
            gate(gates_left.pop(0))
        if j >= 1:
            out_proj(j - 1)
    out_proj(n - 1)


def _prompt_layer(l, x, mod, g_pre, g_post, w_in, w_pool, pool_scale, w_o, tabs):
    b, seq, d = x.shape
    tile = PROMPT_TILE
    cs, sn, dmask, qdec, kdec, gc = tabs
    layer = lambda *_: (l, 0, 0)
    kern = functools.partial(_prompt_kernel, tile=tile)
    return pl.pallas_call(
        kern,
        grid=(b, seq // tile),
        in_specs=[
            pl.BlockSpec((1, tile, d), lambda i, t: (i, t, 0)),
            pl.BlockSpec((None, b, 3 * d), layer),
            pl.BlockSpec((None, 1, d), layer),
            pl.BlockSpec((None, 1, d), layer),
            pl.BlockSpec((None, d, D_IN), layer, pipeline_mode=pl.Buffered(1)),
            pl.BlockSpec((None, len(POOL_WINDOWS), POOL_CG, POOL_CG), lambda *_: (l, 0, 0, 0)),
            pl.BlockSpec((None, 1, D_POOL), layer),
            pl.BlockSpec((None, d, d), layer, pipeline_mode=pl.Buffered(1)),
            pl.BlockSpec((tile, 128), lambda i, t: (t, 0)),
            pl.BlockSpec((tile, 128), lambda i, t: (t, 0)),
            _const_spec(dmask.shape), _const_spec(qdec.shape), _const_spec(kdec.shape), _const_spec(gc.shape),
        ],
        out_specs=[
            pl.BlockSpec((1, tile, d), lambda i, t: (i, t, 0)),
            pl.BlockSpec((1, RET_HEADS, RET_DK, RET_DK), lambda i, t: (i, 0, 0, 0)),
            pl.BlockSpec((1, HIST, D_POOL), lambda i, t: (i, 0, 0)),
        ],
        out_shape=[
            jax.ShapeDtypeStruct((b, seq, d), F32),
            jax.ShapeDtypeStruct((b, RET_HEADS, RET_DK, RET_DK), F32),
            jax.ShapeDtypeStruct((b, HIST, D_POOL), F32),
        ],
        scratch_shapes=[
            pltpu.VMEM((tile, d), BF16),
            pltpu.VMEM((HIST + tile, D_POOL), F32),
            pltpu.VMEM((tile, D_RET), BF16),
            pltpu.VMEM((tile, D_RET), BF16),
            pltpu.VMEM((tile, D_RET), BF16),
            pltpu.VMEM((tile, D_RET), BF16),
            pltpu.VMEM((tile, D_RET), BF16),
            pltpu.VMEM((tile, D_RET), BF16),
            pltpu.VMEM((tile // RET_CHUNK * RET_HEADS, RET_DK, RET_DK), BF16),
            pltpu.VMEM((tile, d), F32),
            pltpu.VMEM((tile, d), F32),
        ],
        compiler_params=pltpu.CompilerParams(
            dimension_semantics=("arbitrary", "arbitrary"), vmem_limit_bytes=VMEM_LIMIT),
        name=f"prompt_layer{l}",
    )(x, mod, g_pre, g_post, w_in, w_pool, pool_scale, w_o, cs, sn, dmask, qdec, kdec, gc)


def _sample_kernel(x_ref, mod_ref, gpre_ref, gpost_ref, win_ref, wpool_ref, pscale_ref, wo_ref,
                   cs_ref, sn_ref, dmask_ref, qdec_ref, kdec_ref, gc_ref, s0_ref, hist_ref,
                   xo_ref, s_ref, nb_ref,
                   ubuf_ref, mix_ref, sg_ref, *, bt, dec_seq):
    n = bt * dec_seq
    pool_buf = hist_ref.shape[1]
    rows = pl.ds(pl.multiple_of(pl.program_id(1) * bt, bt), bt)

    @pl.when(pl.program_id(0) == 0)
    def _():
        xo_ref[rows] = x_ref[...]

    m = mod_ref[...][:, None, :]
    shift, scl, gres = m[:, :, 0:D_MODEL], m[:, :, D_MODEL:2 * D_MODEL], m[:, :, 2 * D_MODEL:]

    x3 = xo_ref[rows]
    h3 = _rms(x3) * (gpre_ref[...][None] * (1.0 + scl)) + shift
    hb = h3.reshape(n, D_MODEL).astype(BF16)

    o1, o2, o3, o4 = D_POOL, D_POOL + D_RET, D_POOL + 2 * D_RET, D_POOL + 3 * D_RET
    heads = [slice(hh * RET_DK, (hh + 1) * RET_DK) for hh in range(RET_HEADS)]
    u = jnp.dot(hb, win_ref[:, 0:o1], preferred_element_type=F32)
    zq = jnp.dot(hb, win_ref[:, o1:o2], preferred_element_type=F32).reshape(bt, dec_seq, D_RET)
    zk = jnp.dot(hb, win_ref[:, o2:o3], preferred_element_type=F32).reshape(bt, dec_seq, D_RET)
    zv = jnp.dot(hb, win_ref[:, o3:o4], preferred_element_type=F32).reshape(bt, dec_seq, D_RET)
    zg = jnp.dot(hb, win_ref[:, o4:], preferred_element_type=F32)

    ubuf_ref[:, HIST - pool_buf:HIST, :] = hist_ref[...]
    ubuf_ref[:, HIST:HIST + dec_seq, :] = u.reshape(bt, dec_seq, D_POOL)
    pooled = []
    for g, w in enumerate(POOL_WINDOWS):
        lo, hi = g * POOL_CG, (g + 1) * POOL_CG
        u_g = ubuf_ref[:, HIST:HIST + dec_seq, lo:hi]
        acc = u_g
        for k in range(1, w):
            acc = acc + ubuf_ref[:, HIST - k:HIST - k + dec_seq, lo:hi]
        pooled.append((acc / float(w) - u_g).reshape(n, POOL_CG).astype(BF16))
    nb_ref[...] = ubuf_ref[:, HIST + dec_seq - pool_buf:HIST + dec_seq, :]

    cs, sn = cs_ref[...][None], sn_ref[...][None]
    qr = [_rope(zq[:, :, sl], cs, sn) for sl in heads]
    kr = [_rope(zk[:, :, sl], cs, sn) * K_SCALE for sl in heads]
    sg_ref[...] = zg * jax.nn.sigmoid(zg)

    for g in range(len(POOL_WINDOWS)):
        lo, hi = g * POOL_CG, (g + 1) * POOL_CG
        mix_ref[:, lo:hi] = jnp.dot(pooled[g], wpool_ref[g], preferred_element_type=F32) * pscale_ref[:, lo:hi]
    scores = [jnp.einsum("bid,bjd->bij", qr[hh], kr[hh], preferred_element_type=F32) * dmask_ref[hh][None]
              for hh in range(RET_HEADS)]
    kv, cross = [], []
    for hh, sl in enumerate(heads):
        kv.append(jnp.einsum("bjd,bje->bde", (kr[hh] * kdec_ref[:, sl][None]).astype(BF16),
                             zv[:, :, sl].astype(BF16), preferred_element_type=F32))
        cross.append(jnp.einsum("bid,bde->bie", qr[hh] * qdec_ref[:, sl][None], s0_ref[:, hh],
                                preferred_element_type=F32))
    intra = [jnp.einsum("bij,bje->bie", scores[hh], zv[:, :, sl], preferred_element_type=F32)
             for hh, sl in enumerate(heads)]
    for hh, sl in enumerate(heads):
        s_ref[:, hh] = gc_ref[0:1, sl][None] * s0_ref[:, hh] + kv[hh]
        mix_ref[:, D_POOL + hh * RET_DK:D_POOL + (hh + 1) * RET_DK] = _rms(intra[hh] + cross[hh]).reshape(n, RET_DK)

    mixb = (mix_ref[...] * sg_ref[...]).astype(BF16)
    y = jnp.dot(mixb, wo_ref[...], preferred_element_type=F32)
    xo_ref[rows] = x3 + (gres * gpost_ref[...][None]) * _rms(y).reshape(bt, dec_seq, D_MODEL)


def _sample_layers(x, mod, g_pre, g_post, w_in, w_pool, pool_scale, w_o, tabs, state_ret, state_pool):
    b, dec_seq, d = x.shape
    depth, _, pool_buf, _ = state_pool.shape
    bt = SAMPLE_BT
    cs, sn, dmask, qdec, kdec, gc = tabs
    layer = lambda l, i: (l, 0, 0)
    kern = functools.partial(_sample_kernel, bt=bt, dec_seq=dec_seq)
    return pl.pallas_call(
        kern,
        grid=(depth, b // bt),
        in_specs=[
            pl.BlockSpec((bt, dec_seq, d), lambda l, i: (i, 0, 0)),
            pl.BlockSpec((None, bt, 3 * d), lambda l, i: (l, i, 0)),
            pl.BlockSpec((None, 1, d), layer),
            pl.BlockSpec((None, 1, d), layer),
            pl.BlockSpec((None, d, D_IN), layer),
            pl.BlockSpec((None, len(POOL_WINDOWS), POOL_CG, POOL_CG), lambda l, i: (l, 0, 0, 0)),
            pl.BlockSpec((None, 1, D_POOL), layer),
            pl.BlockSpec((None, d, d), layer),
            _const_spec(cs.shape), _const_spec(sn.shape),
            _const_spec(dmask.shape), _const_spec(qdec.shape), _const_spec(kdec.shape), _const_spec(gc.shape),
            pl.BlockSpec((None, bt, RET_HEADS, RET_DK, RET_DK), lambda l, i: (l, i, 0, 0, 0)),
            pl.BlockSpec((None, bt, pool_buf, D_POOL), lambda l, i: (l, i, 0, 0)),
        ],
        out_specs=[
            pl.BlockSpec((b, dec_seq, d), lambda l, i: (0, 0, 0)),
            pl.BlockSpec((None, bt, RET_HEADS, RET_DK, RET_DK), lambda l, i: (l, i, 0, 0, 0)),
            pl.BlockSpec((None, bt, pool_buf, D_POOL), lambda l, i: (l, i, 0, 0)),
        ],
        out_shape=[
            jax.ShapeDtypeStruct((b, dec_seq, d), F32),
            jax.ShapeDtypeStruct((depth, b, RET_HEADS, RET_DK, RET_DK), F32),
            jax.ShapeDtypeStruct((depth, b, pool_buf, D_POOL), F32),
        ],
        scratch_shapes=[
            pltpu.VMEM((bt, HIST + dec_seq, D_POOL), F32),
            pltpu.VMEM((bt * dec_seq, d), F32),
            pltpu.VMEM((bt * dec_seq, d), F32),
        ],
        compiler_params=pltpu.CompilerParams(
            dimension_semantics=("arbitrary", "arbitrary"), vmem_limit_bytes=VMEM_LIMIT),
        name="sample_layers",
    )(x, mod, g_pre, g_post, w_in, w_pool, pool_scale, w_o, cs, sn, dmask, qdec, kdec, gc, state_ret, state_pool)


def kernel(x_prompt, x_sample, c_prompt, c_sample, state_ret, state_pool, w_ada, b_ada,
           g_pre, g_post, w_in, w_pool, pool_scale, w_o):
    depth = w_in.shape[0]
    bp, seq, d = x_prompt.shape
    bs, dec_seq, _ = x_sample.shape
    assert d == D_MODEL and seq % PROMPT_TILE == 0 and bs % SAMPLE_BT == 0 and dec_seq == 8

    (csp, snp, css, sns, dmp, qdp, kdp, gcp, dms, qds, kds, gcs) = _make_tables(seq, dec_seq, PAST_LEN)
    tabs_p = (csp, snp, dmp, qdp, kdp, gcp)
    tabs_s = (css, sns, dms, qds, kds, gcs)

    mod_s, mod_p = _ada_mod(c_sample, c_prompt, w_ada, b_ada)

    w_in_b = w_in.astype(BF16)
    w_o_b = w_o.astype(BF16)
    w_pool_b = w_pool.astype(BF16)
    g_pre3 = g_pre.reshape(depth, 1, d)
    g_post3 = g_post.reshape(depth, 1, d)
    pscale3 = pool_scale.reshape(depth, 1, D_POOL)
    hs, ret_s, pool_s = _sample_layers(x_sample, mod_s, g_pre3, g_post3, w_in_b, w_pool_b, pscale3, w_o_b,
                                       tabs_s, state_ret, state_pool)
    hp = x_prompt
    ret_p, pool_p = [], []
    for l in range(depth):
        hp, sp, nbp = _prompt_layer(l, hp, mod_p, g_pre3, g_post3, w_in_b, w_pool_b, pscale3, w_o_b, tabs_p)
        ret_p.append(sp)
        pool_p.append(nbp[:, 1:])
    return (hp, hs, jnp.stack(ret_p), jnp.stack(pool_p), ret_s, pool_s)
```

```python
import functools

import jax
import jax.numpy as jnp
from jax import lax
from jax.experimental import pallas as pl
from jax.experimental.pallas import tpu as pltpu

F32 = jnp.float32
BF16 = jnp.bfloat16

D_MODEL = 1024
D_POOL = 512
D_RET = 512
POOL_WINDOWS = (2, 4, 8, 16)
POOL_CG = 128
HIST = 16
RET_HEADS = 4
RET_DK = 128
RET_CHUNK = 128
ROPE_BASE = 10000.0
PAST_LEN = 16384
D_IN = D_POOL + 3 * D_RET + D_MODEL
EPS = 1e-6
K_SCALE = RET_DK ** -0.5

PROMPT_TILE = 1024
PROMPT_BLOCK = 256
SAMPLE_BT = 16
ADA_COLS = 1536
VMEM_LIMIT = 56 * 1024 * 1024


def _const_spec(shape):
    return pl.BlockSpec(shape, lambda *_: (0,) * len(shape))


def _decay_tables(lg_lane, lg_row, c, dmask_ref, qdec_ref, kdec_ref, gc_ref):
    idx = lax.broadcasted_iota(jnp.int32, (c, D_RET), 0).astype(F32)
    qdec_ref[...] = jnp.exp((idx + 1.0) * lg_lane)
    kdec_ref[...] = jnp.exp((c - 1.0 - idx) * lg_lane)
    gc_ref[...] = jnp.exp(jnp.full((8, D_RET), float(c), F32) * lg_lane)
    ii = lax.broadcasted_iota(jnp.int32, (c, c), 0)
    jj = lax.broadcasted_iota(jnp.int32, (c, c), 1)
    diff = ii - jj
    for h in range(RET_HEADS):
        dec = jnp.exp(jnp.maximum(diff, 0).astype(F32) * lg_row[h])
        dmask_ref[h] = jnp.where(diff >= 0, dec, 0.0)


def _tables_kernel(inv_ref, csp_ref, snp_ref, css_ref, sns_ref,
                   dmp_ref, qdp_ref, kdp_ref, gcp_ref,
                   dms_ref, qds_ref, kds_ref, gcs_ref, *, seq, dec_seq, past_len):
    inv = inv_ref[...]
    lane = lax.broadcasted_iota(jnp.int32, (1, 2 * 64), 1)
    sign = jnp.where(lane < 64, -1.0, 1.0).astype(F32)

    def rope(n, start, cs_ref, sn_ref):
        pos = (lax.broadcasted_iota(jnp.int32, (n, 128), 0) + start).astype(F32)
        ang = pos * inv
        cs_ref[...] = jnp.cos(ang)
        sn_ref[...] = jnp.sin(ang) * sign

    rope(seq, 0, csp_ref, snp_ref)
    rope(dec_seq, past_len, css_ref, sns_ref)

    head_lane = (lax.broadcasted_iota(jnp.int32, (1, D_RET), 1) // RET_DK).astype(F32)
    lg_lane = jnp.log(1.0 - jnp.exp2(-5.0 - head_lane))
    for c, refs in ((RET_CHUNK, (dmp_ref, qdp_ref, kdp_ref, gcp_ref)),
                    (dec_seq, (dms_ref, qds_ref, kds_ref, gcs_ref))):
        lg_row = [jnp.log(1.0 - jnp.exp2(jnp.full((1, c), -5.0 - h, F32))) for h in range(RET_HEADS)]
        _decay_tables(lg_lane, lg_row, c, *refs)


def _make_tables(seq, dec_seq, past_len):
    half = RET_DK // 2
    inv = 1.0 / (ROPE_BASE ** (jnp.arange(half, dtype=F32) / half))
    inv2 = jnp.concatenate([inv, inv])[None, :]
    c, cs = RET_CHUNK, dec_seq
    out_shape = (
        jax.ShapeDtypeStruct((seq, 128), F32), jax.ShapeDtypeStruct((seq, 128), F32),
        jax.ShapeDtypeStruct((dec_seq, 128), F32), jax.ShapeDtypeStruct((dec_seq, 128), F32),
        jax.ShapeDtypeStruct((RET_HEADS, c, c), F32), jax.ShapeDtypeStruct((c, D_RET), F32),
        jax.ShapeDtypeStruct((c, D_RET), F32), jax.ShapeDtypeStruct((8, D_RET), F32),
        jax.ShapeDtypeStruct((RET_HEADS, cs, cs), F32), jax.ShapeDtypeStruct((cs, D_RET), F32),
        jax.ShapeDtypeStruct((cs, D_RET), F32), jax.ShapeDtypeStruct((8, D_RET), F32),
    )
    return pl.pallas_call(
        functools.partial(_tables_kernel, seq=seq, dec_seq=dec_seq, past_len=past_len),
        out_shape=out_shape,
        name="tables",
    )(inv2)


def _ada_kernel(cs_ref, cp_ref, w_ref, b_ref, os_ref, op_ref):
    w = w_ref[0].astype(BF16)
    for c_ref, o_ref in ((cs_ref, os_ref), (cp_ref, op_ref)):
        c = c_ref[...]
        a = (c * jax.nn.sigmoid(c)).astype(BF16)
        o_ref[0] = jnp.dot(a, w, preferred_element_type=F32) + b_ref[0]


def _ada_mod(c_sample, c_prompt, w_ada, b_ada):
    depth, d, d3 = w_ada.shape
    bs, bp = c_sample.shape[0], c_prompt.shape[0]
    return pl.pallas_call(
        _ada_kernel,
        grid=(depth, d3 // ADA_COLS),
        in_specs=[
            pl.BlockSpec((bs, d), lambda l, j: (0, 0)),
            pl.BlockSpec((bp, d), lambda l, j: (0, 0)),
            pl.BlockSpec((1, d, ADA_COLS), lambda l, j: (l, 0, j)),
            pl.BlockSpec((1, 1, ADA_COLS), lambda l, j: (l, 0, j)),
        ],
        out_specs=[pl.BlockSpec((1, bs, ADA_COLS), lambda l, j: (l, 0, j)),
                   pl.BlockSpec((1, bp, ADA_COLS), lambda l, j: (l, 0, j))],
        out_shape=[jax.ShapeDtypeStruct((depth, bs, d3), F32),
                   jax.ShapeDtypeStruct((depth, bp, d3), F32)],
        compiler_params=pltpu.CompilerParams(dimension_semantics=("arbitrary", "arbitrary")),
        name="ada_mod",
    )(c_sample, c_prompt, w_ada, b_ada.reshape(depth, 1, d3))


def _rope(x, cs, sn):
    return x * cs + pltpu.roll(x, RET_DK // 2, x.ndim - 1) * sn


def _rms(x):
    return x * lax.rsqrt(jnp.mean(x * x, axis=-1, keepdims=True) + EPS)


def _prompt_kernel(x_ref, mod_ref, gpre_ref, gpost_ref, win_ref, wpool_ref, pscale_ref, wo_ref,
                   cs_ref, sn_ref, dmask_ref, qdec_ref, kdec_ref, gc_ref,
                   xo_ref, s_ref, nb_ref,
                   hb_ref, ubuf_ref, q_ref, qd_ref, k_ref, kd_ref, v_ref, sc_ref, sprev_ref, mix_ref, sg_ref,
                   *, tile):
    t = pl.program_id(1)
    n_chunks = tile // RET_CHUNK
    blocks = [(r0, r0 + PROMPT_BLOCK) for r0 in range(0, tile, PROMPT_BLOCK)]
    o1, o2, o3, o4 = D_POOL, D_POOL + D_RET, D_POOL + 2 * D_RET, D_POOL + 3 * D_RET
    heads = [slice(hh * RET_DK, (hh + 1) * RET_DK) for hh in range(RET_HEADS)]
    chunks = [slice(c * RET_CHUNK, (c + 1) * RET_CHUNK) for c in range(n_chunks)]

    @pl.when(t == 0)
    def _():
        s_ref[...] = jnp.zeros_like(s_ref)
        ubuf_ref[0:HIST, :] = jnp.zeros((HIST, D_POOL), F32)

    m = mod_ref[pl.ds(pl.program_id(0), 1), :]
    shift, scl, gres = m[:, 0:D_MODEL], m[:, D_MODEL:2 * D_MODEL], m[:, 2 * D_MODEL:]
    pre_gain = gpre_ref[...] * (1.0 + scl)
    post_gain = gres * gpost_ref[...]

    def proj(r0, r1, c0, c1):
        return jnp.dot(hb_ref[r0:r1, :], win_ref[:, c0:c1], preferred_element_type=F32)

    zqk, kvs = {}, {}

    def pre_norm(j):
        r0, r1 = blocks[j]
        hb_ref[r0:r1, :] = (_rms(x_ref[0, r0:r1, :]) * pre_gain + shift).astype(BF16)

    def project(j):
        r0, r1 = blocks[j]
        ubuf_ref[HIST + r0:HIST + r1, :] = proj(r0, r1, 0, o1)
        zqk[j] = (proj(r0, r1, o1, o2), proj(r0, r1, o2, o3))
        v_ref[r0:r1, :] = proj(r0, r1, o3, o4).astype(BF16)

    def mixer_inputs(j):
        r0, r1 = blocks[j]
        row = lax.broadcasted_iota(jnp.int32, (r1 - r0, 1), 0) + (t * tile + r0)
        pooled = []
        for g, w in enumerate(POOL_WINDOWS):
            lo, hi = g * POOL_CG, (g + 1) * POOL_CG
            u_ext = ubuf_ref[r0:r1 + HIST, lo:hi]
            acc = u_ext
            k = 1
            while k < w:
                acc = acc + pltpu.roll(acc, k, 0)
                k *= 2
            cnt = jnp.minimum(row + 1, w).astype(F32)
            pooled.append((acc[HIST:] / cnt - u_ext[HIST:]).astype(BF16))
        zq, zk = zqk.pop(j)
        cs, sn = cs_ref[r0:r1, :], sn_ref[r0:r1, :]
        for sl in heads:
            qr = _rope(zq[:, sl], cs, sn)
            kr = _rope(zk[:, sl], cs, sn) * K_SCALE
            q_ref[r0:r1, sl] = qr.astype(BF16)
            k_ref[r0:r1, sl] = kr.astype(BF16)
            for c in range(r0 // RET_CHUNK, r1 // RET_CHUNK):
                rs, ls = chunks[c], slice(c * RET_CHUNK - r0, (c + 1) * RET_CHUNK - r0)
                qd_ref[rs, sl] = (qr[ls] * qdec_ref[:, sl]).astype(BF16)
                kd_ref[rs, sl] = (kr[ls] * kdec_ref[:, sl]).astype(BF16)
        return pooled

    def small_matmuls(j, pooled):
        r0, r1 = blocks[j]
        for g in range(len(POOL_WINDOWS)):
            lo, hi = g * POOL_CG, (g + 1) * POOL_CG
            y_g = jnp.dot(pooled[g], wpool_ref[g], preferred_element_type=F32)
            mix_ref[r0:r1, lo:hi] = y_g * pscale_ref[:, lo:hi]
        for c in range(r0 // RET_CHUNK, r1 // RET_CHUNK):
            rs = chunks[c]
            for hh, sl in enumerate(heads):
                scores = lax.dot_general(q_ref[rs, sl], k_ref[rs, sl], (((1,), (1,)), ((), ())),
                                         preferred_element_type=F32)
                sc_ref[rs, sl] = (scores * dmask_ref[hh]).astype(BF16)
                kvs[c, hh] = lax.dot_general(kd_ref[rs, sl], v_ref[rs, sl], (((0,), (0,)), ((), ())),
                                             preferred_element_type=F32)

    def gate(j):
        r0, r1 = blocks[j]
        zg = proj(r0, r1, o4, D_IN)
        sg_ref[r0:r1, :] = zg * jax.nn.sigmoid(zg)

    def state_recurrence():
        for hh, sl in enumerate(heads):
            s = s_ref[0, hh]
            for c in range(n_chunks):
                sprev_ref[c * RET_HEADS + hh] = s.astype(BF16)
                s = gc_ref[0:1, sl] * s + kvs.pop((c, hh))
            s_ref[0, hh] = s

    def retention_out(j):
        r0, r1 = blocks[j]
        for c in range(r0 // RET_CHUNK, r1 // RET_CHUNK):
            rs = chunks[c]
            for hh, sl in enumerate(heads):
                lhs = jnp.concatenate([sc_ref[rs, sl], qd_ref[rs, sl]], axis=1)
                rhs = jnp.concatenate([v_ref[rs, sl], sprev_ref[c * RET_HEADS + hh]], axis=0)
                o_h = jnp.dot(lhs, rhs, preferred_element_type=F32)
                mix_ref[rs, D_POOL + hh * RET_DK:D_POOL + (hh + 1) * RET_DK] = _rms(o_h)

    def out_proj(j):
        r0, r1 = blocks[j]
        mixb = (mix_ref[r0:r1, :] * sg_ref[r0:r1, :]).astype(BF16)
        y = jnp.dot(mixb, wo_ref[...], preferred_element_type=F32)
        xo_ref[0, r0:r1, :] = x_ref[0, r0:r1, :] + _rms(y) * post_gain

    n = len(blocks)
    pre_norm(0)
    project(0)
    for j in range(n):
        if j + 1 < n:
            pre_norm(j + 1)
        pooled = mixer_inputs(j)
        if j + 1 < n:
            project(j + 1)
        else:
            nb_ref[0] = ubuf_ref[tile:tile + HIST, :]
            ubuf_ref[0:HIST, :] = ubuf_ref[tile:tile + HIST, :]
            gate(0)
        small_matmuls(j, pooled)
    gate(1)
    state_recurrence()

    gates_left = list(range(2, n))
    for j in range(n):
        retention_out(j)
        if gates_left:
            gate(gates_left.pop(0))
        if j >= 1:
            out_proj(j - 1)
    out_proj(n - 1)


def _prompt_layer(l, x, mod, g_pre, g_post, w_in, w_pool, pool_scale, w_o, tabs):
    b, seq, d = x.shape
    tile = PROMPT_TILE
    cs, sn, dmask, qdec, kdec, gc = tabs
    layer = lambda *_: (l, 0, 0)
    kern = functools.partial(_prompt_kernel, tile=tile)
    return pl.pallas_call(
        kern,
        grid=(b, seq // tile),
        in_specs=[
            pl.BlockSpec((1, tile, d), lambda i, t: (i, t, 0)),
            pl.BlockSpec((None, b, 3 * d), layer),
            pl.BlockSpec((None, 1, d), layer),
            pl.BlockSpec((None, 1, d), layer),
            pl.BlockSpec((None, d, D_IN), layer, pipeline_mode=pl.Buffered(1)),
            pl.BlockSpec((None, len(POOL_WINDOWS), POOL_CG, POOL_CG), lambda *_: (l, 0, 0, 0)),
            pl.BlockSpec((None, 1, D_POOL), layer),
            pl.BlockSpec((None, d, d), layer, pipeline_mode=pl.Buffered(1)),
            pl.BlockSpec((tile, 128), lambda i, t: (t, 0)),
            pl.BlockSpec((tile, 128), lambda i, t: (t, 0)),
            _const_spec(dmask.shape), _const_spec(qdec.shape), _const_spec(kdec.shape), _const_spec(gc.shape),
        ],
        out_specs=[
            pl.BlockSpec((1, tile, d), lambda i, t: (i, t, 0)),
            pl.BlockSpec((1, RET_HEADS, RET_DK, RET_DK), lambda i, t: (i, 0, 0, 0)),
            pl.BlockSpec((1, HIST, D_POOL), lambda i, t: (i, 0, 0)),
        ],
        out_shape=[
            jax.ShapeDtypeStruct((b, seq, d), F32),
            jax.ShapeDtypeStruct((b, RET_HEADS, RET_DK, RET_DK), F32),
            jax.ShapeDtypeStruct((b, HIST, D_POOL), F32),
        ],
        scratch_shapes=[
            pltpu.VMEM((tile, d), BF16),
            pltpu.VMEM((HIST + tile, D_POOL), F32),
            pltpu.VMEM((tile, D_RET), BF16),
            pltpu.VMEM((tile, D_RET), BF16),
            pltpu.VMEM((tile, D_RET), BF16),
            pltpu.VMEM((tile, D_RET), BF16),
            pltpu.VMEM((tile, D_RET), BF16),
            pltpu.VMEM((tile, D_RET), BF16),
            pltpu.VMEM((tile // RET_CHUNK * RET_HEADS, RET_DK, RET_DK), BF16),
            pltpu.VMEM((tile, d), F32),
            pltpu.VMEM((tile, d), F32),
        ],
        compiler_params=pltpu.CompilerParams(
            dimension_semantics=("arbitrary", "arbitrary"), vmem_limit_bytes=VMEM_LIMIT),
        name=f"prompt_layer{l}",
    )(x, mod, g_pre, g_post, w_in, w_pool, pool_scale, w_o, cs, sn, dmask, qdec, kdec, gc)


def _sample_kernel(x_ref, mod_ref, gpre_ref, gpost_ref, win_ref, wpool_ref, pscale_ref, wo_ref,
                   cs_ref, sn_ref, dmask_ref, qdec_ref, kdec_ref, gc_ref, s0_ref, hist_ref,
                   xo_ref, s_ref, nb_ref,
                   usc_ref, mix_ref, sg_ref, *, bt, dec_seq):
    n = bt * dec_seq
    pool_buf = hist_ref.shape[0]
    rows = pl.ds(pl.multiple_of(pl.program_id(1) * bt, bt), bt)

    @pl.when(pl.program_id(0) == 0)
    def _():
        xo_ref[rows] = x_ref[...]

    m = mod_ref[...][:, None, :]
    shift, scl, gres = m[:, :, 0:D_MODEL], m[:, :, D_MODEL:2 * D_MODEL], m[:, :, 2 * D_MODEL:]

    x3 = xo_ref[rows]
    h3 = _rms(x3) * (gpre_ref[...][None] * (1.0 + scl)) + shift
    hb = h3.reshape(n, D_MODEL).astype(BF16)

    o1, o2, o3, o4 = D_POOL, D_POOL + D_RET, D_POOL + 2 * D_RET, D_POOL + 3 * D_RET
    heads = [slice(hh * RET_DK, (hh + 1) * RET_DK) for hh in range(RET_HEADS)]
    u = jnp.dot(hb, win_ref[:, 0:o1], preferred_element_type=F32)
    zq = jnp.dot(hb, win_ref[:, o1:o2], preferred_element_type=F32).reshape(bt, dec_seq, D_RET)
    zk = jnp.dot(hb, win_ref[:, o2:o3], preferred_element_type=F32).reshape(bt, dec_seq, D_RET)
    zv = jnp.dot(hb, win_ref[:, o3:o4], preferred_element_type=F32).reshape(bt, dec_seq, D_RET)
    zg = jnp.dot(hb, win_ref[:, o4:], preferred_element_type=F32)

    pooled = []
    for g, w in enumerate(POOL_WINDOWS):
        lo, hi = g * POOL_CG, (g + 1) * POOL_CG
        usc_ref[g] = u[:, lo:hi]
        rows_g = [hist_ref[r, :, lo:hi] for r in range(pool_buf)]
        rows_g += [usc_ref[g, pl.ds(i, bt, stride=dec_seq), :] for i in range(dec_seq)]
        for i in range(dec_seq):
            cur = pool_buf + i
            acc = rows_g[cur]
            for k in range(1, w):
                acc = acc + rows_g[cur - k]
            usc_ref[g, pl.ds(i, bt, stride=dec_seq), :] = acc / float(w) - rows_g[cur]
        pooled.append(usc_ref[g].astype(BF16))
        for r in range(pool_buf):
            nb_ref[r, :, lo:hi] = rows_g[dec_seq + r]

    cs, sn = cs_ref[...][None], sn_ref[...][None]
    qr = [_rope(zq[:, :, sl], cs, sn) for sl in heads]
    kr = [_rope(zk[:, :, sl], cs, sn) * K_SCALE for sl in heads]
    sg_ref[...] = zg * jax.nn.sigmoid(zg)

    for g in range(len(POOL_WINDOWS)):
        lo, hi = g * POOL_CG, (g + 1) * POOL_CG
        mix_ref[:, lo:hi] = jnp.dot(pooled[g], wpool_ref[g], preferred_element_type=F32) * pscale_ref[:, lo:hi]
    scores = [jnp.einsum("bid,bjd->bij", qr[hh], kr[hh], preferred_element_type=F32) * dmask_ref[hh][None]
              for hh in range(RET_HEADS)]
    kv, cross = [], []
    for hh, sl in enumerate(heads):
        kv.append(jnp.einsum("bjd,bje->bde", (kr[hh] * kdec_ref[:, sl][None]).astype(BF16),
                             zv[:, :, sl].astype(BF16), preferred_element_type=F32))
        cross.append(jnp.einsum("bid,bde->bie", qr[hh] * qdec_ref[:, sl][None], s0_ref[:, hh],
                                preferred_element_type=F32))
    intra = [jnp.einsum("bij,bje->bie", scores[hh], zv[:, :, sl], preferred_element_type=F32)
             for hh, sl in enumerate(heads)]
    for hh, sl in enumerate(heads):
        s_ref[:, hh] = gc_ref[0:1, sl][None] * s0_ref[:, hh] + kv[hh]
        mix_ref[:, D_POOL + hh * RET_DK:D_POOL + (hh + 1) * RET_DK] = _rms(intra[hh] + cross[hh]).reshape(n, RET_DK)

    mixb = (mix_ref[...] * sg_ref[...]).astype(BF16)
    y = jnp.dot(mixb, wo_ref[...], preferred_element_type=F32)
    xo_ref[rows] = x3 + (gres * gpost_ref[...][None]) * _rms(y).reshape(bt, dec_seq, D_MODEL)


def _sample_layers(x, mod, g_pre, g_post, w_in, w_pool, pool_scale, w_o, tabs, state_ret, state_pool):
    b, dec_seq, d = x.shape
    depth, pool_buf, _, _ = state_pool.shape
    bt = SAMPLE_BT
    cs, sn, dmask, qdec, kdec, gc = tabs
    layer = lambda l, i: (l, 0, 0)
    kern = functools.partial(_sample_kernel, bt=bt, dec_seq=dec_seq)
    return pl.pallas_call(
        kern,
        grid=(depth, b // bt),
        in_specs=[
            pl.BlockSpec((bt, dec_seq, d), lambda l, i: (i, 0, 0)),
            pl.BlockSpec((None, bt, 3 * d), lambda l, i: (l, i, 0)),
            pl.BlockSpec((None, 1, d), layer),
            pl.BlockSpec((None, 1, d), layer),
            pl.BlockSpec((None, d, D_IN), layer),
            pl.BlockSpec((None, len(POOL_WINDOWS), POOL_CG, POOL_CG), lambda l, i: (l, 0, 0, 0)),
            pl.BlockSpec((None, 1, D_POOL), layer),
            pl.BlockSpec((None, d, d), layer),
            _const_spec(cs.shape), _const_spec(sn.shape),
            _const_spec(dmask.shape), _const_spec(qdec.shape), _const_spec(kdec.shape), _const_spec(gc.shape),
            pl.BlockSpec((None, bt, RET_HEADS, RET_DK, RET_DK), lambda l, i: (l, i, 0, 0, 0)),
            pl.BlockSpec((None, pool_buf, bt, D_POOL), lambda l, i: (l, 0, i, 0)),
        ],
        out_specs=[
            pl.BlockSpec((b, dec_seq, d), lambda l, i: (0, 0, 0)),
            pl.BlockSpec((None, bt, RET_HEADS, RET_DK, RET_DK), lambda l, i: (l, i, 0, 0, 0)),
            pl.BlockSpec((None, pool_buf, bt, D_POOL), lambda l, i: (l, 0, i, 0)),
        ],
        out_shape=[
            jax.ShapeDtypeStruct((b, dec_seq, d), F32),
            jax.ShapeDtypeStruct((depth, b, RET_HEADS, RET_DK, RET_DK), F32),
            jax.ShapeDtypeStruct((depth, pool_buf, b, D_POOL), F32),
        ],
        scratch_shapes=[
            pltpu.VMEM((len(POOL_WINDOWS), bt * dec_seq, POOL_CG), F32),
            pltpu.VMEM((bt * dec_seq, d), F32),
            pltpu.VMEM((bt * dec_seq, d), F32),
        ],
        compiler_params=pltpu.CompilerParams(
            dimension_semantics=("arbitrary", "arbitrary"), vmem_limit_bytes=VMEM_LIMIT),
        name="sample_layers",
    )(x, mod, g_pre, g_post, w_in, w_pool, pool_scale, w_o, cs, sn, dmask, qdec, kdec, gc, state_ret, state_pool)


def kernel(x_prompt, x_sample, c_prompt, c_sample, state_ret, state_pool, w_ada, b_ada,
           g_pre, g_post, w_in, w_pool, pool_scale, w_o):
    depth = w_in.shape[0]
    bp, seq, d = x_prompt.shape
    bs, dec_seq, _ = x_sample.shape
    assert d == D_MODEL and seq % PROMPT_TILE == 0 and bs % SAMPLE_BT == 0 and dec_seq == 8

    (csp, snp, css, sns, dmp, qdp, kdp, gcp, dms, qds, kds, gcs) = _make_tables(seq, dec_seq, PAST_LEN)
    tabs_p = (csp, snp, dmp, qdp, kdp, gcp)
    tabs_s = (css, sns, dms, qds, kds, gcs)

    mod_s, mod_p = _ada_mod(c_sample, c_prompt, w_ada, b_ada)

    w_in_b = w_in.astype(BF16)
    w_o_b = w_o.astype(BF16)
    w_pool_b = w_pool.astype(BF16)
    g_pre3 = g_pre.reshape(depth, 1, d)
    g_post3 = g_post.reshape(depth, 1, d)
    pscale3 = pool_scale.reshape(depth, 1, D_POOL)
    hs, ret_s, pool_s = _sample_layers(x_sample, mod_s, g_pre3, g_post3, w_in_b, w_pool_b, pscale3, w_o_b,
                                       tabs_s, state_ret, jnp.transpose(state_pool, (0, 2, 1, 3)))
    pool_s = jnp.transpose(pool_s, (0, 2, 1, 3))
    hp = x_prompt
    ret_p, pool_p = [], []
    for l in range(depth):
        hp, sp, nbp = _prompt_layer(l, hp, mod_p, g_pre3, g_post3, w_in_b, w_pool_b, pscale3, w_o_b, tabs_p)
        ret_p.append(sp)
        pool_p.append(nbp[:, 1:])
    return (hp, hs, jnp.stack(ret_p), jnp.stack(pool_p), ret_s, pool_s)
```

```python
import functools

import jax
import jax.numpy as jnp
from jax import lax
from jax.experimental import pallas as pl
from jax.experimental.pallas import tpu as pltpu

F32 = jnp.float32
BF16 = jnp.bfloat16

D_MODEL = 1024
D_POOL = 512
D_RET = 512
POOL_WINDOWS = (2, 4, 8, 16)
POOL_CG = 128
HIST = 16
RET_HEADS = 4
RET_DK = 128
RET_CHUNK = 128
ROPE_BASE = 10000.0
PAST_LEN = 16384
ROPE_ROWS = 128
D_IN = D_POOL + 3 * D_RET + D_MODEL
EPS = 1e-6
K_SCALE = RET_DK ** -0.5

PROMPT_TILE = 1024
PROMPT_BLOCK = 256
SAMPLE_BT = 16
PREP_ROWS = 256
VMEM_LIMIT = 56 * 1024 * 1024


def _const_spec(shape):
    return pl.BlockSpec(shape, lambda *_: (0,) * len(shape))


def _decay_tables(lg_lane, lg_row, c, dmask_ref, qdec_ref, kdec_ref, gc_ref):
    idx = lax.broadcasted_iota(jnp.int32, (c, D_RET), 0).astype(F32)
    qdec_ref[...] = jnp.exp((idx + 1.0) * lg_lane)
    kdec_ref[...] = jnp.exp((c - 1.0 - idx) * lg_lane)
    gc_ref[...] = jnp.exp(jnp.full((8, D_RET), float(c), F32) * lg_lane)
    ii = lax.broadcasted_iota(jnp.int32, (c, c), 0)
    jj = lax.broadcasted_iota(jnp.int32, (c, c), 1)
    diff = ii - jj
    for h in range(RET_HEADS):
        dec = jnp.exp(jnp.maximum(diff, 0).astype(F32) * lg_row[h])
        dmask_ref[h] = jnp.where(diff >= 0, dec, 0.0)


def _tables_kernel(inv_ref, csp_ref, snp_ref, css_ref, sns_ref,
                   dmp_ref, qdp_ref, kdp_ref, gcp_ref,
                   dms_ref, qds_ref, kds_ref, gcs_ref, *, seq, dec_seq, past_len):
    inv = inv_ref[...]
    lane = lax.broadcasted_iota(jnp.int32, (1, 2 * 64), 1)
    sign = jnp.where(lane < 64, -1.0, 1.0).astype(F32)

    def angles(n, start, step):
        pos = (lax.broadcasted_iota(jnp.int32, (n, 128), 0) * step + start).astype(F32)
        return pos * inv

    ang = angles(dec_seq, past_len, 1)
    css_ref[...] = jnp.cos(ang)
    sns_ref[...] = jnp.sin(ang) * sign
    lo = angles(ROPE_ROWS, 0, 1)
    hi = angles(seq // ROPE_ROWS, 0, ROPE_ROWS)
    c_lo, s_lo, c_hi, s_hi = jnp.cos(lo), jnp.sin(lo), jnp.cos(hi), jnp.sin(hi)
    for r in range(seq // ROPE_ROWS):
        rows = slice(r * ROPE_ROWS, (r + 1) * ROPE_ROWS)
        csp_ref[rows, :] = c_hi[r:r + 1] * c_lo - s_hi[r:r + 1] * s_lo
        snp_ref[rows, :] = (s_hi[r:r + 1] * c_lo + c_hi[r:r + 1] * s_lo) * sign

    head_lane = (lax.broadcasted_iota(jnp.int32, (1, D_RET), 1) // RET_DK).astype(F32)
    lg_lane = jnp.log(1.0 - jnp.exp2(-5.0 - head_lane))
    for c, refs in ((RET_CHUNK, (dmp_ref, qdp_ref, kdp_ref, gcp_ref)),
                    (dec_seq, (dms_ref, qds_ref, kds_ref, gcs_ref))):
        lg_row = [jnp.log(1.0 - jnp.exp2(jnp.full((1, c), -5.0 - h, F32))) for h in range(RET_HEADS)]
        _decay_tables(lg_lane, lg_row, c, *refs)


def _make_tables(seq, dec_seq, past_len):
    half = RET_DK // 2
    inv = 1.0 / (ROPE_BASE ** (jnp.arange(half, dtype=F32) / half))
    inv2 = jnp.concatenate([inv, inv])[None, :]
    c, cs = RET_CHUNK, dec_seq
    out_shape = (
        jax.ShapeDtypeStruct((seq, 128), F32), jax.ShapeDtypeStruct((seq, 128), F32),
        jax.ShapeDtypeStruct((dec_seq, 128), F32), jax.ShapeDtypeStruct((dec_seq, 128), F32),
        jax.ShapeDtypeStruct((RET_HEADS, c, c), F32), jax.ShapeDtypeStruct((c, D_RET), F32),
        jax.ShapeDtypeStruct((c, D_RET), F32), jax.ShapeDtypeStruct((8, D_RET), F32),
        jax.ShapeDtypeStruct((RET_HEADS, cs, cs), F32), jax.ShapeDtypeStruct((cs, D_RET), F32),
        jax.ShapeDtypeStruct((cs, D_RET), F32), jax.ShapeDtypeStruct((8, D_RET), F32),
    )
    return pl.pallas_call(
        functools.partial(_tables_kernel, seq=seq, dec_seq=dec_seq, past_len=past_len),
        out_shape=out_shape,
        name="tables",
    )(inv2)


def _prep_kernel(cs_ref, cp_ref, wada_ref, bada_ref, win_ref, wo_ref, wpool_ref,
                 mods_ref, modp_ref, winb_ref, wob_ref, wpoolb_ref):
    j = pl.program_id(1)
    w = wada_ref[0].astype(BF16)
    for c_ref, o_ref in ((cs_ref, mods_ref), (cp_ref, modp_ref)):
        c = c_ref[...]
        part = jnp.dot((c * jax.nn.sigmoid(c)).astype(BF16), w, preferred_element_type=F32)

        @pl.when(j == 0)
        def _():
            o_ref[0] = part + bada_ref[0]

        @pl.when(j > 0)
        def _():
            o_ref[0] += part
    winb_ref[0] = win_ref[0].astype(BF16)
    wob_ref[0] = wo_ref[0].astype(BF16)
    wpoolb_ref[0] = wpool_ref[0].astype(BF16)


def _prep(c_sample, c_prompt, w_ada, b_ada, w_in, w_o, w_pool):
    depth, d, d3 = w_ada.shape
    bs, bp = c_sample.shape[0], c_prompt.shape[0]
    rows = lambda l, j: (l, j, 0)
    whole = lambda l, j: (l, 0, 0)
    return pl.pallas_call(
        _prep_kernel,
        grid=(depth, d // PREP_ROWS),
        in_specs=[
            pl.BlockSpec((bs, PREP_ROWS), lambda l, j: (0, j)),
            pl.BlockSpec((bp, PREP_ROWS), lambda l, j: (0, j)),
            pl.BlockSpec((1, PREP_ROWS, d3), rows),
            pl.BlockSpec((1, 1, d3), whole),
            pl.BlockSpec((1, PREP_ROWS, w_in.shape[2]), rows),
            pl.BlockSpec((1, PREP_ROWS, w_o.shape[2]), rows),
            pl.BlockSpec((1,) + w_pool.shape[1:], lambda l, j: (l, 0, 0, 0)),
        ],
        out_specs=[
            pl.BlockSpec((1, bs, d3), whole),
            pl.BlockSpec((1, bp, d3), whole),
            pl.BlockSpec((1, PREP_ROWS, w_in.shape[2]), rows),
            pl.BlockSpec((1, PREP_ROWS, w_o.shape[2]), rows),
            pl.BlockSpec((1,) + w_pool.shape[1:], lambda l, j: (l, 0, 0, 0)),
        ],
        out_shape=[
            jax.ShapeDtypeStruct((depth, bs, d3), F32),
            jax.ShapeDtypeStruct((depth, bp, d3), F32),
            jax.ShapeDtypeStruct(w_in.shape, BF16),
            jax.ShapeDtypeStruct(w_o.shape, BF16),
            jax.ShapeDtypeStruct(w_pool.shape, BF16),
        ],
        compiler_params=pltpu.CompilerParams(
            dimension_semantics=("arbitrary", "arbitrary"), vmem_limit_bytes=VMEM_LIMIT),
        name="prep",
    )(c_sample, c_prompt, w_ada, b_ada.reshape(depth, 1, d3), w_in, w_o, w_pool)


def _rope(x, cs, sn):
    return x * cs + pltpu.roll(x, RET_DK // 2, x.ndim - 1) * sn


def _rms(x):
    return x * lax.rsqrt(jnp.mean(x * x, axis=-1, keepdims=True) + EPS)


def _prompt_kernel(x_ref, mod_ref, gpre_ref, gpost_ref, win_ref, wpool_ref, pscale_ref, wo_ref,
                   cs_ref, sn_ref, dmask_ref, qdec_ref, kdec_ref, gc_ref,
                   xo_ref, s_ref, nb_ref,
                   hb_ref, ubuf_ref, q_ref, qd_ref, k_ref, kd_ref, v_ref, sc_ref, sprev_ref, mix_ref, sg_ref,
                   *, tile):
    t = pl.program_id(1)
    n_chunks = tile // RET_CHUNK
    blocks = [(r0, r0 + PROMPT_BLOCK) for r0 in range(0, tile, PROMPT_BLOCK)]
    o1, o2, o3, o4 = D_POOL, D_POOL + D_RET, D_POOL + 2 * D_RET, D_POOL + 3 * D_RET
    heads = [slice(hh * RET_DK, (hh + 1) * RET_DK) for hh in range(RET_HEADS)]
    chunks = [slice(c * RET_CHUNK, (c + 1) * RET_CHUNK) for c in range(n_chunks)]

    @pl.when(t == 0)
    def _():
        s_ref[...] = jnp.zeros_like(s_ref)
        ubuf_ref[0:HIST, :] = jnp.zeros((HIST, D_POOL), F32)

    m = mod_ref[pl.ds(pl.program_id(0), 1), :]
    shift, scl, gres = m[:, 0:D_MODEL], m[:, D_MODEL:2 * D_MODEL], m[:, 2 * D_MODEL:]
    pre_gain = gpre_ref[...] * (1.0 + scl)
    post_gain = gres * gpost_ref[...]

    def proj(r0, r1, c0, c1):
        return jnp.dot(hb_ref[r0:r1, :], win_ref[:, c0:c1], preferred_element_type=F32)

    zqk, kvs = {}, {}

    def pre_norm(j):
        r0, r1 = blocks[j]
        hb_ref[r0:r1, :] = (_rms(x_ref[0, r0:r1, :]) * pre_gain + shift).astype(BF16)

    def project(j):
        r0, r1 = blocks[j]
        ubuf_ref[HIST + r0:HIST + r1, :] = proj(r0, r1, 0, o1)
        zqk[j] = (proj(r0, r1, o1, o2), proj(r0, r1, o2, o3))
        v_ref[r0:r1, :] = proj(r0, r1, o3, o4).astype(BF16)

    def mixer_inputs(j):
        r0, r1 = blocks[j]
        row = lax.broadcasted_iota(jnp.int32, (r1 - r0, 1), 0) + (t * tile + r0)
        pooled = []
        for g, w in enumerate(POOL_WINDOWS):
            lo, hi = g * POOL_CG, (g + 1) * POOL_CG
            u_ext = ubuf_ref[r0:r1 + HIST, lo:hi]
            acc = u_ext
            k = 1
            while k < w:
                acc = acc + pltpu.roll(acc, k, 0)
                k *= 2
            cnt = jnp.minimum(row + 1, w).astype(F32)
            pooled.append((acc[HIST:] / cnt - u_ext[HIST:]).astype(BF16))
        zq, zk = zqk.pop(j)
        cs, sn = cs_ref[r0:r1, :], sn_ref[r0:r1, :]
        for sl in heads:
            qr = _rope(zq[:, sl], cs, sn)
            kr = _rope(zk[:, sl], cs, sn) * K_SCALE
            q_ref[r0:r1, sl] = qr.astype(BF16)
            k_ref[r0:r1, sl] = kr.astype(BF16)
            for c in range(r0 // RET_CHUNK, r1 // RET_CHUNK):
                rs, ls = chunks[c], slice(c * RET_CHUNK - r0, (c + 1) * RET_CHUNK - r0)
                qd_ref[rs, sl] = (qr[ls] * qdec_ref[:, sl]).astype(BF16)
                kd_ref[rs, sl] = (kr[ls] * kdec_ref[:, sl]).astype(BF16)
        return pooled

    def small_matmuls(j, pooled):
        r0, r1 = blocks[j]
        for g in range(len(POOL_WINDOWS)):
            lo, hi = g * POOL_CG, (g + 1) * POOL_CG
            y_g = jnp.dot(pooled[g], wpool_ref[g], preferred_element_type=F32)
            mix_ref[r0:r1, lo:hi] = y_g * pscale_ref[:, lo:hi]
        for c in range(r0 // RET_CHUNK, r1 // RET_CHUNK):
            rs = chunks[c]
            for hh, sl in enumerate(heads):
                scores = lax.dot_general(q_ref[rs, sl], k_ref[rs, sl], (((1,), (1,)), ((), ())),
                                         preferred_element_type=F32)
                sc_ref[rs, sl] = (scores * dmask_ref[hh]).astype(BF16)
                kvs[c, hh] = lax.dot_general(kd_ref[rs, sl], v_ref[rs, sl], (((0,), (0,)), ((), ())),
                                             preferred_element_type=F32)

    def gate(j):
        r0, r1 = blocks[j]
        zg = proj(r0, r1, o4, D_IN)
        sg_ref[r0:r1, :] = zg * jax.nn.sigmoid(zg)

    def state_recurrence():
        for hh, sl in enumerate(heads):
            s = s_ref[0, hh]
            for c in range(n_chunks):
                sprev_ref[c * RET_HEADS + hh] = s.astype(BF16)
                s = gc_ref[0:1, sl] * s + kvs.pop((c, hh))
            s_ref[0, hh] = s

    def retention_out(j):
        r0, r1 = blocks[j]
        for c in range(r0 // RET_CHUNK, r1 // RET_CHUNK):
            rs = chunks[c]
            for hh, sl in enumerate(heads):
                lhs = jnp.concatenate([sc_ref[rs, sl], qd_ref[rs, sl]], axis=1)
                rhs = jnp.concatenate([v_ref[rs, sl], sprev_ref[c * RET_HEADS + hh]], axis=0)
                o_h = jnp.dot(lhs, rhs, preferred_element_type=F32)
                mix_ref[rs, D_POOL + hh * RET_DK:D_POOL + (hh + 1) * RET_DK] = _rms(o_h)

    def out_proj(j):
        r0, r1 = blocks[j]
        mixb = (mix_ref[r0:r1, :] * sg_ref[r0:r1, :]).astype(BF16)
        y = jnp.dot(mixb, wo_ref[...], preferred_element_type=F32)
        xo_ref[0, r0:r1, :] = x_ref[0, r0:r1, :] + _rms(y) * post_gain

    n = len(blocks)
    pre_norm(0)
    project(0)
    for j in range(n):
        if j + 1 < n:
            pre_norm(j + 1)
        pooled = mixer_inputs(j)
        if j + 1 < n:
            project(j + 1)
        else:
            nb_ref[0] = ubuf_ref[tile:tile + HIST, :]
            ubuf_ref[0:HIST, :] = ubuf_ref[tile:tile + HIST, :]
            gate(0)
        small_matmuls(j, pooled)
    gate(1)
    state_recurrence()

    gates_left = list(range(2, n))
    for j in range(n):
        retention_out(j)
        if gates_left:
            gate(gates_left.pop(0))
        if j >= 1:
            out_proj(j - 1)
    out_proj(n - 1)


def _prompt_layer(l, x, mod, g_pre, g_post, w_in, w_pool, pool_scale, w_o, tabs, stacked):
    depth = w_in.shape[0]
    b, seq, d = x.shape
    tile = PROMPT_TILE
    cs, sn, dmask, qdec, kdec, gc = tabs
    layer = lambda *_: (l, 0, 0)
    n_in = 14
    carried = () if stacked is None else tuple(stacked)

    def kern(*refs):
        _prompt_kernel(*refs[:n_in], *refs[n_in + len(carried):], tile=tile)

    return pl.pallas_call(
        kern,
        grid=(b, seq // tile),
        in_specs=[
            pl.BlockSpec((1, tile, d), lambda i, t: (i, t, 0)),
            pl.BlockSpec((None, b, 3 * d), layer),
            pl.BlockSpec((None, 1, d), layer),
            pl.BlockSpec((None, 1, d), layer),
            pl.BlockSpec((None, d, D_IN), layer, pipeline_mode=pl.Buffered(1)),
            pl.BlockSpec((None, len(POOL_WINDOWS), POOL_CG, POOL_CG), lambda *_: (l, 0, 0, 0)),
            pl.BlockSpec((None, 1, D_POOL), layer),
            pl.BlockSpec((None, d, d), layer, pipeline_mode=pl.Buffered(1)),
            pl.BlockSpec((tile, 128), lambda i, t: (t, 0)),
            pl.BlockSpec((tile, 128), lambda i, t: (t, 0)),
            _const_spec(dmask.shape), _const_spec(qdec.shape), _const_spec(kdec.shape), _const_spec(gc.shape),
        ] + [pl.BlockSpec(memory_space=pl.ANY)] * len(carried),
        out_specs=[
            pl.BlockSpec((1, tile, d), lambda i, t: (i, t, 0)),
            pl.BlockSpec((None, 1, RET_HEADS, RET_DK, RET_DK), lambda i, t: (l, i, 0, 0, 0)),
            pl.BlockSpec((None, 1, HIST, D_POOL), lambda i, t: (l, i, 0, 0)),
        ],
        out_shape=[
            jax.ShapeDtypeStruct((b, seq, d), F32),
            jax.ShapeDtypeStruct((depth, b, RET_HEADS, RET_DK, RET_DK), F32),
            jax.ShapeDtypeStruct((depth, b, HIST, D_POOL), F32),
        ],
        input_output_aliases={n_in + i: 1 + i for i in range(len(carried))},
        scratch_shapes=[
            pltpu.VMEM((tile, d), BF16),
            pltpu.VMEM((HIST + tile, D_POOL), F32),
            pltpu.VMEM((tile, D_RET), BF16),
            pltpu.VMEM((tile, D_RET), BF16),
            pltpu.VMEM((tile, D_RET), BF16),
            pltpu.VMEM((tile, D_RET), BF16),
            pltpu.VMEM((tile, D_RET), BF16),
            pltpu.VMEM((tile, D_RET), BF16),
            pltpu.VMEM((tile // RET_CHUNK * RET_HEADS, RET_DK, RET_DK), BF16),
            pltpu.VMEM((tile, d), F32),
            pltpu.VMEM((tile, d), F32),
        ],
        compiler_params=pltpu.CompilerParams(
            dimension_semantics=("arbitrary", "arbitrary"), vmem_limit_bytes=VMEM_LIMIT),
        name=f"prompt_layer{l}",
    )(x, mod, g_pre, g_post, w_in, w_pool, pool_scale, w_o, cs, sn, dmask, qdec, kdec, gc, *carried)


def _sample_kernel(x_ref, mod_ref, gpre_ref, gpost_ref, win_ref, wpool_ref, pscale_ref, wo_ref,
                   cs_ref, sn_ref, dmask_ref, qdec_ref, kdec_ref, gc_ref, s0_ref, hist_ref,
                   xo_ref, s_ref, nb_ref,
                   usc_ref, mix_ref, sg_ref, *, bt, dec_seq):
    n = bt * dec_seq
    pool_buf = hist_ref.shape[0]
    rows = pl.ds(pl.multiple_of(pl.program_id(1) * bt, bt), bt)

    @pl.when(pl.program_id(0) == 0)
    def _():
        xo_ref[rows] = x_ref[...]

    m = mod_ref[...][:, None, :]
    shift, scl, gres = m[:, :, 0:D_MODEL], m[:, :, D_MODEL:2 * D_MODEL], m[:, :, 2 * D_MODEL:]

    x3 = xo_ref[rows]
    h3 = _rms(x3) * (gpre_ref[...][None] * (1.0 + scl)) + shift
    hb = h3.reshape(n, D_MODEL).astype(BF16)

    o1, o2, o3, o4 = D_POOL, D_POOL + D_RET, D_POOL + 2 * D_RET, D_POOL + 3 * D_RET
    heads = [slice(hh * RET_DK, (hh + 1) * RET_DK) for hh in range(RET_HEADS)]
    u = jnp.dot(hb, win_ref[:, 0:o1], preferred_element_type=F32)
    zq = jnp.dot(hb, win_ref[:, o1:o2], preferred_element_type=F32).reshape(bt, dec_seq, D_RET)
    zk = jnp.dot(hb, win_ref[:, o2:o3], preferred_element_type=F32).reshape(bt, dec_seq, D_RET)
    zv = jnp.dot(hb, win_ref[:, o3:o4], preferred_element_type=F32).reshape(bt, dec_seq, D_RET)
    zg = jnp.dot(hb, win_ref[:, o4:], preferred_element_type=F32)

    pooled = []
    for g, w in enumerate(POOL_WINDOWS):
        lo, hi = g * POOL_CG, (g + 1) * POOL_CG
        usc_ref[g] = u[:, lo:hi]
        rows_g = [hist_ref[r, :, lo:hi] for r in range(pool_buf)]
        rows_g += [usc_ref[g, pl.ds(i, bt, stride=dec_seq), :] for i in range(dec_seq)]
        for i in range(dec_seq):
            cur = pool_buf + i
            acc = rows_g[cur]
            for k in range(1, w):
                acc = acc + rows_g[cur - k]
            usc_ref[g, pl.ds(i, bt, stride=dec_seq), :] = acc / float(w) - rows_g[cur]
        pooled.append(usc_ref[g].astype(BF16))
        for r in range(pool_buf):
            nb_ref[r, :, lo:hi] = rows_g[dec_seq + r]

    cs, sn = cs_ref[...][None], sn_ref[...][None]
    qr = [_rope(zq[:, :, sl], cs, sn) for sl in heads]
    kr = [_rope(zk[:, :, sl], cs, sn) * K_SCALE for sl in heads]
    sg_ref[...] = zg * jax.nn.sigmoid(zg)

    for g in range(len(POOL_WINDOWS)):
        lo, hi = g * POOL_CG, (g + 1) * POOL_CG
        mix_ref[:, lo:hi] = jnp.dot(pooled[g], wpool_ref[g], preferred_element_type=F32) * pscale_ref[:, lo:hi]
    scores = [jnp.einsum("bid,bjd->bij", qr[hh], kr[hh], preferred_element_type=F32) * dmask_ref[hh][None]
              for hh in range(RET_HEADS)]
    kv, cross = [], []
    for hh, sl in enumerate(heads):
        kv.append(jnp.einsum("bjd,bje->bde", (kr[hh] * kdec_ref[:, sl][None]).astype(BF16),
                             zv[:, :, sl].astype(BF16), preferred_element_type=F32))
        cross.append(jnp.einsum("bid,bde->bie", qr[hh] * qdec_ref[:, sl][None], s0_ref[:, hh],
                                preferred_element_type=F32))
    intra = [jnp.einsum("bij,bje->bie", scores[hh], zv[:, :, sl], preferred_element_type=F32)
             for hh, sl in enumerate(heads)]
    for hh, sl in enumerate(heads):
        s_ref[:, hh] = gc_ref[0:1, sl][None] * s0_ref[:, hh] + kv[hh]
        mix_ref[:, D_POOL + hh * RET_DK:D_POOL + (hh + 1) * RET_DK] = _rms(intra[hh] + cross[hh]).reshape(n, RET_DK)

    mixb = (mix_ref[...] * sg_ref[...]).astype(BF16)
    y = jnp.dot(mixb, wo_ref[...], preferred_element_type=F32)
    xo_ref[rows] = x3 + (gres * gpost_ref[...][None]) * _rms(y).reshape(bt, dec_seq, D_MODEL)


def _sample_layers(x, mod, g_pre, g_post, w_in, w_pool, pool_scale, w_o, tabs, state_ret, state_pool):
    b, dec_seq, d = x.shape
    depth, pool_buf, _, _ = state_pool.shape
    bt = SAMPLE_BT
    cs, sn, dmask, qdec, kdec, gc = tabs
    layer = lambda l, i: (l, 0, 0)
    kern = functools.partial(_sample_kernel, bt=bt, dec_seq=dec_seq)
    return pl.pallas_call(
        kern,
        grid=(depth, b // bt),
        in_specs=[
            pl.BlockSpec((bt, dec_seq, d), lambda l, i: (i, 0, 0)),
            pl.BlockSpec((None, bt, 3 * d), lambda l, i: (l, i, 0)),
            pl.BlockSpec((None, 1, d), layer),
            pl.BlockSpec((None, 1, d), layer),
            pl.BlockSpec((None, d, D_IN), layer),
            pl.BlockSpec((None, len(POOL_WINDOWS), POOL_CG, POOL_CG), lambda l, i: (l, 0, 0, 0)),
            pl.BlockSpec((None, 1, D_POOL), layer),
            pl.BlockSpec((None, d, d), layer),
            _const_spec(cs.shape), _const_spec(sn.shape),
            _const_spec(dmask.shape), _const_spec(qdec.shape), _const_spec(kdec.shape), _const_spec(gc.shape),
            pl.BlockSpec((None, bt, RET_HEADS, RET_DK, RET_DK), lambda l, i: (l, i, 0, 0, 0)),
            pl.BlockSpec((None, pool_buf, bt, D_POOL), lambda l, i: (l, 0, i, 0)),
        ],
        out_specs=[
            pl.BlockSpec((b, dec_seq, d), lambda l, i: (0, 0, 0)),
            pl.BlockSpec((None, bt, RET_HEADS, RET_DK, RET_DK), lambda l, i: (l, i, 0, 0, 0)),
            pl.BlockSpec((None, pool_buf, bt, D_POOL), lambda l, i: (l, 0, i, 0)),
        ],
        out_shape=[
            jax.ShapeDtypeStruct((b, dec_seq, d), F32),
            jax.ShapeDtypeStruct((depth, b, RET_HEADS, RET_DK, RET_DK), F32),
            jax.ShapeDtypeStruct((depth, pool_buf, b, D_POOL), F32),
        ],
        scratch_shapes=[
            pltpu.VMEM((len(POOL_WINDOWS), bt * dec_seq, POOL_CG), F32),
            pltpu.VMEM((bt * dec_seq, d), F32),
            pltpu.VMEM((bt * dec_seq, d), F32),
        ],
        compiler_params=pltpu.CompilerParams(
            dimension_semantics=("arbitrary", "arbitrary"), vmem_limit_bytes=VMEM_LIMIT),
        name="sample_layers",
    )(x, mod, g_pre, g_post, w_in, w_pool, pool_scale, w_o, cs, sn, dmask, qdec, kdec, gc, state_ret, state_pool)


def kernel(x_prompt, x_sample, c_prompt, c_sample, state_ret, state_pool, w_ada, b_ada,
           g_pre, g_post, w_in, w_pool, pool_scale, w_o):
    depth = w_in.shape[0]
    bp, seq, d = x_prompt.shape
    bs, dec_seq, _ = x_sample.shape
    assert d == D_MODEL and seq % PROMPT_TILE == 0 and bs % SAMPLE_BT == 0 and dec_seq == 8

    (csp, snp, css, sns, dmp, qdp, kdp, gcp, dms, qds, kds, gcs) = _make_tables(seq, dec_seq, PAST_LEN)
    tabs_p = (csp, snp, dmp, qdp, kdp, gcp)
    tabs_s = (css, sns, dms, qds, kds, gcs)

    mod_s, mod_p, w_in_b, w_o_b, w_pool_b = _prep(c_sample, c_prompt, w_ada, b_ada, w_in, w_o, w_pool)
    g_pre3 = g_pre.reshape(depth, 1, d)
    g_post3 = g_post.reshape(depth, 1, d)
    pscale3 = pool_scale.reshape(depth, 1, D_POOL)
    hs, ret_s, pool_s = _sample_layers(x_sample, mod_s, g_pre3, g_post3, w_in_b, w_pool_b, pscale3, w_o_b,
                                       tabs_s, state_ret, jnp.transpose(state_pool, (0, 2, 1, 3)))
    pool_s = jnp.transpose(pool_s, (0, 2, 1, 3))
    hp, stacked = x_prompt, None
    for l in range(depth):
        hp, *stacked = _prompt_layer(l, hp, mod_p, g_pre3, g_post3, w_in_b, w_pool_b, pscale3, w_o_b, tabs_p,
                                     stacked)
    ret_p, hist_p = stacked
    return (hp, hs, ret_p, hist_p[:, :, HIST - state_pool.shape[2]:], ret_s, pool_s)
```

```python
import functools

import jax
import jax.numpy as jnp
from jax import lax
from jax.experimental import pallas as pl
from jax.experimental.pallas import tpu as pltpu

F32 = jnp.float32
BF16 = jnp.bfloat16

D_MODEL = 1024
D_POOL = 512
D_RET = 512
POOL_WINDOWS = (2, 4, 8, 16)
POOL_CG = 128
HIST = 16
RET_HEADS = 4
RET_DK = 128
RET_CHUNK = 128
ROPE_BASE = 10000.0
PAST_LEN = 16384
ROPE_ROWS = 128
D_IN = D_POOL + 3 * D_RET + D_MODEL
EPS = 1e-6
K_SCALE = RET_DK ** -0.5

PROMPT_TILE = 1024
PROMPT_BLOCK = 256
SAMPLE_BT = 16
PREP_ROWS = 256
VMEM_LIMIT = 56 * 1024 * 1024


def _const_spec(shape):
    return pl.BlockSpec(shape, lambda *_: (0,) * len(shape))


def _decay_tables(lg_lane, lg_row, c, dmask_ref, qdec_ref, kdec_ref, gc_ref):
    idx = lax.broadcasted_iota(jnp.int32, (c, D_RET), 0).astype(F32)
    qdec_ref[...] = jnp.exp((idx + 1.0) * lg_lane)
    kdec_ref[...] = jnp.exp((c - 1.0 - idx) * lg_lane)
    gc_ref[...] = jnp.exp(jnp.full((8, D_RET), float(c), F32) * lg_lane)
    ii = lax.broadcasted_iota(jnp.int32, (c, c), 0)
    jj = lax.broadcasted_iota(jnp.int32, (c, c), 1)
    diff = ii - jj
    for h in range(RET_HEADS):
        dec = jnp.exp(jnp.maximum(diff, 0).astype(F32) * lg_row[h])
        dmask_ref[h] = jnp.where(diff >= 0, dec, 0.0)


def _tables_kernel(inv_ref, csp_ref, snp_ref, css_ref, sns_ref,
                   dmp_ref, qdp_ref, kdp_ref, gcp_ref,
                   dms_ref, qds_ref, kds_ref, gcs_ref, *, seq, dec_seq, past_len):
    inv = inv_ref[...]
    lane = lax.broadcasted_iota(jnp.int32, (1, 2 * 64), 1)
    sign = jnp.where(lane < 64, -1.0, 1.0).astype(F32)

    def angles(n, start, step):
        pos = (lax.broadcasted_iota(jnp.int32, (n, 128), 0) * step + start).astype(F32)
        return pos * inv

    ang = angles(dec_seq, past_len, 1)
    css_ref[...] = jnp.cos(ang)
    sns_ref[...] = jnp.sin(ang) * sign
    lo = angles(ROPE_ROWS, 0, 1)
    hi = angles(seq // ROPE_ROWS, 0, ROPE_ROWS)
    c_lo, s_lo, c_hi, s_hi = jnp.cos(lo), jnp.sin(lo), jnp.cos(hi), jnp.sin(hi)
    for r in range(seq // ROPE_ROWS):
        rows = slice(r * ROPE_ROWS, (r + 1) * ROPE_ROWS)
        csp_ref[rows, :] = c_hi[r:r + 1] * c_lo - s_hi[r:r + 1] * s_lo
        snp_ref[rows, :] = (s_hi[r:r + 1] * c_lo + c_hi[r:r + 1] * s_lo) * sign

    head_lane = (lax.broadcasted_iota(jnp.int32, (1, D_RET), 1) // RET_DK).astype(F32)
    lg_lane = jnp.log(1.0 - jnp.exp2(-5.0 - head_lane))
    for c, refs in ((RET_CHUNK, (dmp_ref, qdp_ref, kdp_ref, gcp_ref)),
                    (dec_seq, (dms_ref, qds_ref, kds_ref, gcs_ref))):
        lg_row = [jnp.log(1.0 - jnp.exp2(jnp.full((1, c), -5.0 - h, F32))) for h in range(RET_HEADS)]
        _decay_tables(lg_lane, lg_row, c, *refs)


def _make_tables(seq, dec_seq, past_len):
    half = RET_DK // 2
    inv = 1.0 / (ROPE_BASE ** (jnp.arange(half, dtype=F32) / half))
    inv2 = jnp.concatenate([inv, inv])[None, :]
    c, cs = RET_CHUNK, dec_seq
    out_shape = (
        jax.ShapeDtypeStruct((seq, 128), F32), jax.ShapeDtypeStruct((seq, 128), F32),
        jax.ShapeDtypeStruct((dec_seq, 128), F32), jax.ShapeDtypeStruct((dec_seq, 128), F32),
        jax.ShapeDtypeStruct((RET_HEADS, c, c), F32), jax.ShapeDtypeStruct((c, D_RET), F32),
        jax.ShapeDtypeStruct((c, D_RET), F32), jax.ShapeDtypeStruct((8, D_RET), F32),
        jax.ShapeDtypeStruct((RET_HEADS, cs, cs), F32), jax.ShapeDtypeStruct((cs, D_RET), F32),
        jax.ShapeDtypeStruct((cs, D_RET), F32), jax.ShapeDtypeStruct((8, D_RET), F32),
    )
    return pl.pallas_call(
        functools.partial(_tables_kernel, seq=seq, dec_seq=dec_seq, past_len=past_len),
        out_shape=out_shape,
        name="tables",
    )(inv2)


def _prep_kernel(cs_ref, cp_ref, wada_ref, bada_ref, win_ref, wo_ref, wpool_ref,
                 mods_ref, modp_ref, winb_ref, wob_ref, wpoolb_ref):
    j = pl.program_id(1)
    w = wada_ref[0].astype(BF16)
    for c_ref, o_ref in ((cs_ref, mods_ref), (cp_ref, modp_ref)):
        c = c_ref[...]
        part = jnp.dot((c * jax.nn.sigmoid(c)).astype(BF16), w, preferred_element_type=F32)

        @pl.when(j == 0)
        def _():
            o_ref[0] = part + bada_ref[pl.ds(pl.program_id(0), 1), :]

        @pl.when(j > 0)
        def _():
            o_ref[0] += part
    winb_ref[0] = win_ref[0].astype(BF16)
    wob_ref[0] = wo_ref[0].astype(BF16)
    wpoolb_ref[0] = wpool_ref[0].astype(BF16)


def _prep(c_sample, c_prompt, w_ada, b_ada, w_in, w_o, w_pool):
    depth, d, d3 = w_ada.shape
    bs, bp = c_sample.shape[0], c_prompt.shape[0]
    rows = lambda l, j: (l, j, 0)
    whole = lambda l, j: (l, 0, 0)
    return pl.pallas_call(
        _prep_kernel,
        grid=(depth, d // PREP_ROWS),
        in_specs=[
            pl.BlockSpec((bs, PREP_ROWS), lambda l, j: (0, j)),
            pl.BlockSpec((bp, PREP_ROWS), lambda l, j: (0, j)),
            pl.BlockSpec((1, PREP_ROWS, d3), rows),
            _const_spec(b_ada.shape),
            pl.BlockSpec((1, PREP_ROWS, w_in.shape[2]), rows),
            pl.BlockSpec((1, PREP_ROWS, w_o.shape[2]), rows),
            pl.BlockSpec((1,) + w_pool.shape[1:], lambda l, j: (l, 0, 0, 0)),
        ],
        out_specs=[
            pl.BlockSpec((1, bs, d3), whole),
            pl.BlockSpec((1, bp, d3), whole),
            pl.BlockSpec((1, PREP_ROWS, w_in.shape[2]), rows),
            pl.BlockSpec((1, PREP_ROWS, w_o.shape[2]), rows),
            pl.BlockSpec((1,) + w_pool.shape[1:], lambda l, j: (l, 0, 0, 0)),
        ],
        out_shape=[
            jax.ShapeDtypeStruct((depth, bs, d3), F32),
            jax.ShapeDtypeStruct((depth, bp, d3), F32),
            jax.ShapeDtypeStruct(w_in.shape, BF16),
            jax.ShapeDtypeStruct(w_o.shape, BF16),
            jax.ShapeDtypeStruct(w_pool.shape, BF16),
        ],
        compiler_params=pltpu.CompilerParams(
            dimension_semantics=("arbitrary", "arbitrary"), vmem_limit_bytes=VMEM_LIMIT),
        name="prep",
    )(c_sample, c_prompt, w_ada, b_ada, w_in, w_o, w_pool)


def _rope(x, cs, sn):
    return x * cs + pltpu.roll(x, RET_DK // 2, x.ndim - 1) * sn


def _rms(x):
    return x * lax.rsqrt(jnp.mean(x * x, axis=-1, keepdims=True) + EPS)


def _prompt_kernel(x_ref, mod_ref, gpre_ref, gpost_ref, win_ref, wpool_ref, pscale_ref, wo_ref,
                   cs_ref, sn_ref, dmask_ref, qdec_ref, kdec_ref, gc_ref,
                   xo_ref, s_ref, nb_ref,
                   hb_ref, ubuf_ref, q_ref, qd_ref, k_ref, kd_ref, v_ref, sc_ref, sprev_ref, mix_ref, sg_ref,
                   *, tile, layer):
    t = pl.program_id(1)
    n_chunks = tile // RET_CHUNK
    blocks = [(r0, r0 + PROMPT_BLOCK) for r0 in range(0, tile, PROMPT_BLOCK)]
    o1, o2, o3, o4 = D_POOL, D_POOL + D_RET, D_POOL + 2 * D_RET, D_POOL + 3 * D_RET
    heads = [slice(hh * RET_DK, (hh + 1) * RET_DK) for hh in range(RET_HEADS)]
    chunks = [slice(c * RET_CHUNK, (c + 1) * RET_CHUNK) for c in range(n_chunks)]

    @pl.when(t == 0)
    def _():
        s_ref[...] = jnp.zeros_like(s_ref)
        ubuf_ref[0:HIST, :] = jnp.zeros((HIST, D_POOL), F32)

    m = mod_ref[pl.ds(pl.program_id(0), 1), :]
    shift, scl, gres = m[:, 0:D_MODEL], m[:, D_MODEL:2 * D_MODEL], m[:, 2 * D_MODEL:]
    pre_gain = gpre_ref[layer:layer + 1, :] * (1.0 + scl)
    post_gain = gres * gpost_ref[layer:layer + 1, :]

    def proj(r0, r1, c0, c1):
        return jnp.dot(hb_ref[r0:r1, :], win_ref[:, c0:c1], preferred_element_type=F32)

    zqk, kvs = {}, {}

    def pre_norm(j):
        r0, r1 = blocks[j]
        hb_ref[r0:r1, :] = (_rms(x_ref[0, r0:r1, :]) * pre_gain + shift).astype(BF16)

    def project(j):
        r0, r1 = blocks[j]
        ubuf_ref[HIST + r0:HIST + r1, :] = proj(r0, r1, 0, o1)
        zqk[j] = (proj(r0, r1, o1, o2), proj(r0, r1, o2, o3))
        v_ref[r0:r1, :] = proj(r0, r1, o3, o4).astype(BF16)

    def mixer_inputs(j):
        r0, r1 = blocks[j]
        row = lax.broadcasted_iota(jnp.int32, (r1 - r0, 1), 0) + (t * tile + r0)
        pooled = []
        for g, w in enumerate(POOL_WINDOWS):
            lo, hi = g * POOL_CG, (g + 1) * POOL_CG
            u_ext = ubuf_ref[r0:r1 + HIST, lo:hi]
            acc = u_ext
            k = 1
            while k < w:
                acc = acc + pltpu.roll(acc, k, 0)
                k *= 2
            cnt = jnp.minimum(row + 1, w).astype(F32)
            pooled.append((acc[HIST:] / cnt - u_ext[HIST:]).astype(BF16))
        zq, zk = zqk.pop(j)
        cs, sn = cs_ref[r0:r1, :], sn_ref[r0:r1, :]
        for sl in heads:
            qr = _rope(zq[:, sl], cs, sn)
            kr = _rope(zk[:, sl], cs, sn) * K_SCALE
            q_ref[r0:r1, sl] = qr.astype(BF16)
            k_ref[r0:r1, sl] = kr.astype(BF16)
            for c in range(r0 // RET_CHUNK, r1 // RET_CHUNK):
                rs, ls = chunks[c], slice(c * RET_CHUNK - r0, (c + 1) * RET_CHUNK - r0)
                qd_ref[rs, sl] = (qr[ls] * qdec_ref[:, sl]).astype(BF16)
                kd_ref[rs, sl] = (kr[ls] * kdec_ref[:, sl]).astype(BF16)
        return pooled

    def small_matmuls(j, pooled):
        r0, r1 = blocks[j]
        for g in range(len(POOL_WINDOWS)):
            lo, hi = g * POOL_CG, (g + 1) * POOL_CG
            y_g = jnp.dot(pooled[g], wpool_ref[g], preferred_element_type=F32)
            mix_ref[r0:r1, lo:hi] = y_g * pscale_ref[layer:layer + 1, lo:hi]
        for c in range(r0 // RET_CHUNK, r1 // RET_CHUNK):
            rs = chunks[c]
            for hh, sl in enumerate(heads):
                scores = lax.dot_general(q_ref[rs, sl], k_ref[rs, sl], (((1,), (1,)), ((), ())),
                                         preferred_element_type=F32)
                sc_ref[rs, sl] = (scores * dmask_ref[hh]).astype(BF16)
                kvs[c, hh] = lax.dot_general(kd_ref[rs, sl], v_ref[rs, sl], (((0,), (0,)), ((), ())),
                                             preferred_element_type=F32)

    def gate(j):
        r0, r1 = blocks[j]
        zg = proj(r0, r1, o4, D_IN)
        sg_ref[r0:r1, :] = zg * jax.nn.sigmoid(zg)

    def state_recurrence():
        for hh, sl in enumerate(heads):
            s = s_ref[0, hh]
            for c in range(n_chunks):
                sprev_ref[c * RET_HEADS + hh] = s.astype(BF16)
                s = gc_ref[0:1, sl] * s + kvs.pop((c, hh))
            s_ref[0, hh] = s

    def retention_out(j):
        r0, r1 = blocks[j]
        for c in range(r0 // RET_CHUNK, r1 // RET_CHUNK):
            rs = chunks[c]
            for hh, sl in enumerate(heads):
                lhs = jnp.concatenate([sc_ref[rs, sl], qd_ref[rs, sl]], axis=1)
                rhs = jnp.concatenate([v_ref[rs, sl], sprev_ref[c * RET_HEADS + hh]], axis=0)
                o_h = jnp.dot(lhs, rhs, preferred_element_type=F32)
                mix_ref[rs, D_POOL + hh * RET_DK:D_POOL + (hh + 1) * RET_DK] = _rms(o_h)

    def out_proj(j):
        r0, r1 = blocks[j]
        mixb = (mix_ref[r0:r1, :] * sg_ref[r0:r1, :]).astype(BF16)
        y = jnp.dot(mixb, wo_ref[...], preferred_element_type=F32)
        xo_ref[0, r0:r1, :] = x_ref[0, r0:r1, :] + _rms(y) * post_gain

    n = len(blocks)
    pre_norm(0)
    project(0)
    for j in range(n):
        if j + 1 < n:
            pre_norm(j + 1)
        pooled = mixer_inputs(j)
        if j + 1 < n:
            project(j + 1)
        else:
            nb_ref[0] = ubuf_ref[tile:tile + HIST, :]
            ubuf_ref[0:HIST, :] = ubuf_ref[tile:tile + HIST, :]
            gate(0)
        small_matmuls(j, pooled)
    gate(1)
    state_recurrence()

    gates_left = list(range(2, n))
    for j in range(n):
        retention_out(j)
        if gates_left:
            gate(gates_left.pop(0))
        if j >= 1:
            out_proj(j - 1)
    out_proj(n - 1)


def _prompt_layer(l, x, mod, g_pre, g_post, w_in, w_pool, pool_scale, w_o, tabs, stacked):
    depth = w_in.shape[0]
    b, seq, d = x.shape
    tile = PROMPT_TILE
    cs, sn, dmask, qdec, kdec, gc = tabs
    layer = lambda *_: (l, 0, 0)
    n_in = 14
    carried = () if stacked is None else tuple(stacked)

    def kern(*refs):
        _prompt_kernel(*refs[:n_in], *refs[n_in + len(carried):], tile=tile, layer=l)

    return pl.pallas_call(
        kern,
        grid=(b, seq // tile),
        in_specs=[
            pl.BlockSpec((1, tile, d), lambda i, t: (i, t, 0)),
            pl.BlockSpec((None, b, 3 * d), layer),
            _const_spec(g_pre.shape),
            _const_spec(g_post.shape),
            pl.BlockSpec((None, d, D_IN), layer, pipeline_mode=pl.Buffered(1)),
            pl.BlockSpec((None, len(POOL_WINDOWS), POOL_CG, POOL_CG), lambda *_: (l, 0, 0, 0)),
            _const_spec(pool_scale.shape),
            pl.BlockSpec((None, d, d), layer, pipeline_mode=pl.Buffered(1)),
            pl.BlockSpec((tile, 128), lambda i, t: (t, 0)),
            pl.BlockSpec((tile, 128), lambda i, t: (t, 0)),
            _const_spec(dmask.shape), _const_spec(qdec.shape), _const_spec(kdec.shape), _const_spec(gc.shape),
        ] + [pl.BlockSpec(memory_space=pl.ANY)] * len(carried),
        out_specs=[
            pl.BlockSpec((1, tile, d), lambda i, t: (i, t, 0)),
            pl.BlockSpec((None, 1, RET_HEADS, RET_DK, RET_DK), lambda i, t: (l, i, 0, 0, 0)),
            pl.BlockSpec((None, 1, HIST, D_POOL), lambda i, t: (l, i, 0, 0)),
        ],
        out_shape=[
            jax.ShapeDtypeStruct((b, seq, d), F32),
            jax.ShapeDtypeStruct((depth, b, RET_HEADS, RET_DK, RET_DK), F32),
            jax.ShapeDtypeStruct((depth, b, HIST, D_POOL), F32),
        ],
        input_output_aliases={n_in + i: 1 + i for i in range(len(carried))},
        scratch_shapes=[
            pltpu.VMEM((tile, d), BF16),
            pltpu.VMEM((HIST + tile, D_POOL), F32),
            pltpu.VMEM((tile, D_RET), BF16),
            pltpu.VMEM((tile, D_RET), BF16),
            pltpu.VMEM((tile, D_RET), BF16),
            pltpu.VMEM((tile, D_RET), BF16),
            pltpu.VMEM((tile, D_RET), BF16),
            pltpu.VMEM((tile, D_RET), BF16),
            pltpu.VMEM((tile // RET_CHUNK * RET_HEADS, RET_DK, RET_DK), BF16),
            pltpu.VMEM((tile, d), F32),
            pltpu.VMEM((tile, d), F32),
        ],
        compiler_params=pltpu.CompilerParams(
            dimension_semantics=("arbitrary", "arbitrary"), vmem_limit_bytes=VMEM_LIMIT),
        name=f"prompt_layer{l}",
    )(x, mod, g_pre, g_post, w_in, w_pool, pool_scale, w_o, cs, sn, dmask, qdec, kdec, gc, *carried)


def _sample_kernel(x_ref, mod_ref, gpre_ref, gpost_ref, win_ref, wpool_ref, pscale_ref, wo_ref,
                   cs_ref, sn_ref, dmask_ref, qdec_ref, kdec_ref, gc_ref, s0_ref, hist_ref,
                   xo_ref, s_ref, nb_ref,
                   usc_ref, mix_ref, sg_ref, *, bt, dec_seq):
    n = bt * dec_seq
    pool_buf = hist_ref.shape[0]
    rows = pl.ds(pl.multiple_of(pl.program_id(1) * bt, bt), bt)

    @pl.when(pl.program_id(0) == 0)
    def _():
        xo_ref[rows] = x_ref[...]

    m = mod_ref[...][:, None, :]
    shift, scl, gres = m[:, :, 0:D_MODEL], m[:, :, D_MODEL:2 * D_MODEL], m[:, :, 2 * D_MODEL:]

    x3 = xo_ref[rows]
    layer = pl.ds(pl.program_id(0), 1)
    h3 = _rms(x3) * (gpre_ref[layer, :][None] * (1.0 + scl)) + shift
    hb = h3.reshape(n, D_MODEL).astype(BF16)

    o1, o2, o3, o4 = D_POOL, D_POOL + D_RET, D_POOL + 2 * D_RET, D_POOL + 3 * D_RET
    heads = [slice(hh * RET_DK, (hh + 1) * RET_DK) for hh in range(RET_HEADS)]
    u = jnp.dot(hb, win_ref[:, 0:o1], preferred_element_type=F32)
    zq = jnp.dot(hb, win_ref[:, o1:o2], preferred_element_type=F32).reshape(bt, dec_seq, D_RET)
    zk = jnp.dot(hb, win_ref[:, o2:o3], preferred_element_type=F32).reshape(bt, dec_seq, D_RET)
    zv = jnp.dot(hb, win_ref[:, o3:o4], preferred_element_type=F32).reshape(bt, dec_seq, D_RET)
    zg = jnp.dot(hb, win_ref[:, o4:], preferred_element_type=F32)

    pooled = []
    for g, w in enumerate(POOL_WINDOWS):
        lo, hi = g * POOL_CG, (g + 1) * POOL_CG
        usc_ref[g] = u[:, lo:hi]
        rows_g = [hist_ref[r, :, lo:hi] for r in range(pool_buf)]
        rows_g += [usc_ref[g, pl.ds(i, bt, stride=dec_seq), :] for i in range(dec_seq)]
        for i in range(dec_seq):
            cur = pool_buf + i
            acc = rows_g[cur]
            for k in range(1, w):
                acc = acc + rows_g[cur - k]
            usc_ref[g, pl.ds(i, bt, stride=dec_seq), :] = acc / float(w) - rows_g[cur]
        pooled.append(usc_ref[g].astype(BF16))
        for r in range(pool_buf):
            nb_ref[r, :, lo:hi] = rows_g[dec_seq + r]

    cs, sn = cs_ref[...][None], sn_ref[...][None]
    qr = [_rope(zq[:, :, sl], cs, sn) for sl in heads]
    kr = [_rope(zk[:, :, sl], cs, sn) * K_SCALE for sl in heads]
    sg_ref[...] = zg * jax.nn.sigmoid(zg)

    for g in range(len(POOL_WINDOWS)):
        lo, hi = g * POOL_CG, (g + 1) * POOL_CG
        mix_ref[:, lo:hi] = jnp.dot(pooled[g], wpool_ref[g], preferred_element_type=F32) * pscale_ref[layer, :][:, lo:hi]
    scores = [jnp.einsum("bid,bjd->bij", qr[hh], kr[hh], preferred_element_type=F32) * dmask_ref[hh][None]
              for hh in range(RET_HEADS)]
    kv, cross = [], []
    for hh, sl in enumerate(heads):
        kv.append(jnp.einsum("bjd,bje->bde", (kr[hh] * kdec_ref[:, sl][None]).astype(BF16),
                             zv[:, :, sl].astype(BF16), preferred_element_type=F32))
        cross.append(jnp.einsum("bid,bde->bie", qr[hh] * qdec_ref[:, sl][None], s0_ref[:, hh],
                                preferred_element_type=F32))
    intra = [jnp.einsum("bij,bje->bie", scores[hh], zv[:, :, sl], preferred_element_type=F32)
             for hh, sl in enumerate(heads)]
    for hh, sl in enumerate(heads):
        s_ref[:, hh] = gc_ref[0:1, sl][None] * s0_ref[:, hh] + kv[hh]
        mix_ref[:, D_POOL + hh * RET_DK:D_POOL + (hh + 1) * RET_DK] = _rms(intra[hh] + cross[hh]).reshape(n, RET_DK)

    mixb = (mix_ref[...] * sg_ref[...]).astype(BF16)
    y = jnp.dot(mixb, wo_ref[...], preferred_element_type=F32)
    xo_ref[rows] = x3 + (gres * gpost_ref[layer, :][None]) * _rms(y).reshape(bt, dec_seq, D_MODEL)


def _sample_layers(x, mod, g_pre, g_post, w_in, w_pool, pool_scale, w_o, tabs, state_ret, state_pool):
    b, dec_seq, d = x.shape
    depth, pool_buf, _, _ = state_pool.shape
    bt = SAMPLE_BT
    cs, sn, dmask, qdec, kdec, gc = tabs
    layer = lambda l, i: (l, 0, 0)
    kern = functools.partial(_sample_kernel, bt=bt, dec_seq=dec_seq)
    return pl.pallas_call(
        kern,
        grid=(depth, b // bt),
        in_specs=[
            pl.BlockSpec((bt, dec_seq, d), lambda l, i: (i, 0, 0)),
            pl.BlockSpec((None, bt, 3 * d), lambda l, i: (l, i, 0)),
            _const_spec(g_pre.shape),
            _const_spec(g_post.shape),
            pl.BlockSpec((None, d, D_IN), layer),
            pl.BlockSpec((None, len(POOL_WINDOWS), POOL_CG, POOL_CG), lambda l, i: (l, 0, 0, 0)),
            _const_spec(pool_scale.shape),
            pl.BlockSpec((None, d, d), layer),
            _const_spec(cs.shape), _const_spec(sn.shape),
            _const_spec(dmask.shape), _const_spec(qdec.shape), _const_spec(kdec.shape), _const_spec(gc.shape),
            pl.BlockSpec((None, bt, RET_HEADS, RET_DK, RET_DK), lambda l, i: (l, i, 0, 0, 0)),
            pl.BlockSpec((None, pool_buf, bt, D_POOL), lambda l, i: (l, 0, i, 0)),
        ],
        out_specs=[
            pl.BlockSpec((b, dec_seq, d), lambda l, i: (0, 0, 0)),
            pl.BlockSpec((None, bt, RET_HEADS, RET_DK, RET_DK), lambda l, i: (l, i, 0, 0, 0)),
            pl.BlockSpec((None, pool_buf, bt, D_POOL), lambda l, i: (l, 0, i, 0)),
        ],
        out_shape=[
            jax.ShapeDtypeStruct((b, dec_seq, d), F32),
            jax.ShapeDtypeStruct((depth, b, RET_HEADS, RET_DK, RET_DK), F32),
            jax.ShapeDtypeStruct((depth, pool_buf, b, D_POOL), F32),
        ],
        scratch_shapes=[
            pltpu.VMEM((len(POOL_WINDOWS), bt * dec_seq, POOL_CG), F32),
            pltpu.VMEM((bt * dec_seq, d), F32),
            pltpu.VMEM((bt * dec_seq, d), F32),
        ],
        compiler_params=pltpu.CompilerParams(
            dimension_semantics=("arbitrary", "arbitrary"), vmem_limit_bytes=VMEM_LIMIT),
        name="sample_layers",
    )(x, mod, g_pre, g_post, w_in, w_pool, pool_scale, w_o, cs, sn, dmask, qdec, kdec, gc, state_ret, state_pool)


def kernel(x_prompt, x_sample, c_prompt, c_sample, state_ret, state_pool, w_ada, b_ada,
           g_pre, g_post, w_in, w_pool, pool_scale, w_o):
    depth = w_in.shape[0]
    bp, seq, d = x_prompt.shape
    bs, dec_seq, _ = x_sample.shape
    assert d == D_MODEL and seq % PROMPT_TILE == 0 and bs % SAMPLE_BT == 0 and dec_seq == 8

    (csp, snp, css, sns, dmp, qdp, kdp, gcp, dms, qds, kds, gcs) = _make_tables(seq, dec_seq, PAST_LEN)
    tabs_p = (csp, snp, dmp, qdp, kdp, gcp)
    tabs_s = (css, sns, dms, qds, kds, gcs)

    mod_s, mod_p, w_in_b, w_o_b, w_pool_b = _prep(c_sample, c_prompt, w_ada, b_ada, w_in, w_o, w_pool)
    hs, ret_s, pool_s = _sample_layers(x_sample, mod_s, g_pre, g_post, w_in_b, w_pool_b, pool_scale, w_o_b,
                                       tabs_s, state_ret, jnp.transpose(state_pool, (0, 2, 1, 3)))
    pool_s = jnp.transpose(pool_s, (0, 2, 1, 3))
    hp, stacked = x_prompt, None
    for l in range(depth):
        hp, *stacked = _prompt_layer(l, hp, mod_p, g_pre, g_post, w_in_b, w_pool_b, pool_scale, w_o_b, tabs_p,
                                     stacked)
    ret_p, hist_p = stacked
    return (hp, hs, ret_p, hist_p[:, :, HIST - state_pool.shape[2]:], ret_s, pool_s)
```

```python
import functools

import jax
import jax.numpy as jnp
from jax import lax
from jax.experimental import pallas as pl
from jax.experimental.pallas import tpu as pltpu

F32 = jnp.float32
BF16 = jnp.bfloat16

D_MODEL = 1024
D_POOL = 512
D_RET = 512
POOL_WINDOWS = (2, 4, 8, 16)
POOL_CG = 128
HIST = 16
RET_HEADS = 4
RET_DK = 128
RET_CHUNK = 128
ROPE_BASE = 10000.0
PAST_LEN = 16384
ROPE_ROWS = 128
D_IN = D_POOL + 3 * D_RET + D_MODEL
EPS = 1e-6
K_SCALE = RET_DK ** -0.5

PROMPT_TILE = 1024
PROMPT_BLOCK = 256
SAMPLE_BT = 16
PREP_ROWS = 512
VMEM_LIMIT = 56 * 1024 * 1024


def _const_spec(shape):
    return pl.BlockSpec(shape, lambda *_: (0,) * len(shape))


def _decay_tables(lg_lane, lg_row, c, dmask_ref, qdec_ref, kdec_ref, gc_ref):
    idx = lax.broadcasted_iota(jnp.int32, (c, D_RET), 0).astype(F32)
    qdec_ref[...] = jnp.exp((idx + 1.0) * lg_lane)
    kdec_ref[...] = jnp.exp((c - 1.0 - idx) * lg_lane)
    gc_ref[...] = jnp.exp(jnp.full((8, D_RET), float(c), F32) * lg_lane)
    ii = lax.broadcasted_iota(jnp.int32, (c, c), 0)
    jj = lax.broadcasted_iota(jnp.int32, (c, c), 1)
    diff = ii - jj
    for h in range(RET_HEADS):
        dec = jnp.exp(jnp.maximum(diff, 0).astype(F32) * lg_row[h])
        dmask_ref[h] = jnp.where(diff >= 0, dec, 0.0)


def _tables_kernel(inv_ref, csp_ref, snp_ref, css_ref, sns_ref,
                   dmp_ref, qdp_ref, kdp_ref, gcp_ref,
                   dms_ref, qds_ref, kds_ref, gcs_ref, *, seq, dec_seq, past_len):
    inv = inv_ref[...]
    lane = lax.broadcasted_iota(jnp.int32, (1, 2 * 64), 1)
    sign = jnp.where(lane < 64, -1.0, 1.0).astype(F32)

    def angles(n, start, step):
        pos = (lax.broadcasted_iota(jnp.int32, (n, 128), 0) * step + start).astype(F32)
        return pos * inv

    ang = angles(dec_seq, past_len, 1)
    css_ref[...] = jnp.cos(ang)
    sns_ref[...] = jnp.sin(ang) * sign
    lo = angles(ROPE_ROWS, 0, 1)
    hi = angles(seq // ROPE_ROWS, 0, ROPE_ROWS)
    c_lo, s_lo, c_hi, s_hi = jnp.cos(lo), jnp.sin(lo), jnp.cos(hi), jnp.sin(hi)
    for r in range(seq // ROPE_ROWS):
        rows = slice(r * ROPE_ROWS, (r + 1) * ROPE_ROWS)
        csp_ref[rows, :] = c_hi[r:r + 1] * c_lo - s_hi[r:r + 1] * s_lo
        snp_ref[rows, :] = (s_hi[r:r + 1] * c_lo + c_hi[r:r + 1] * s_lo) * sign

    head_lane = (lax.broadcasted_iota(jnp.int32, (1, D_RET), 1) // RET_DK).astype(F32)
    lg_lane = jnp.log(1.0 - jnp.exp2(-5.0 - head_lane))
    for c, refs in ((RET_CHUNK, (dmp_ref, qdp_ref, kdp_ref, gcp_ref)),
                    (dec_seq, (dms_ref, qds_ref, kds_ref, gcs_ref))):
        lg_row = [jnp.log(1.0 - jnp.exp2(jnp.full((1, c), -5.0 - h, F32))) for h in range(RET_HEADS)]
        _decay_tables(lg_lane, lg_row, c, *refs)


def _make_tables(seq, dec_seq, past_len):
    half = RET_DK // 2
    inv = 1.0 / (ROPE_BASE ** (jnp.arange(half, dtype=F32) / half))
    inv2 = jnp.concatenate([inv, inv])[None, :]
    c, cs = RET_CHUNK, dec_seq
    out_shape = (
        jax.ShapeDtypeStruct((seq, 128), F32), jax.ShapeDtypeStruct((seq, 128), F32),
        jax.ShapeDtypeStruct((dec_seq, 128), F32), jax.ShapeDtypeStruct((dec_seq, 128), F32),
        jax.ShapeDtypeStruct((RET_HEADS, c, c), F32), jax.ShapeDtypeStruct((c, D_RET), F32),
        jax.ShapeDtypeStruct((c, D_RET), F32), jax.ShapeDtypeStruct((8, D_RET), F32),
        jax.ShapeDtypeStruct((RET_HEADS, cs, cs), F32), jax.ShapeDtypeStruct((cs, D_RET), F32),
        jax.ShapeDtypeStruct((cs, D_RET), F32), jax.ShapeDtypeStruct((8, D_RET), F32),
    )
    return pl.pallas_call(
        functools.partial(_tables_kernel, seq=seq, dec_seq=dec_seq, past_len=past_len),
        out_shape=out_shape,
        name="tables",
    )(inv2)


def _prep_kernel(cs_ref, cp_ref, wada_ref, bada_ref, win_ref, wo_ref, wpool_ref,
                 mods_ref, modp_ref, winb_ref, wob_ref, wpoolb_ref, wfold_ref):
    j = pl.program_id(1)
    w = wada_ref[0].astype(BF16)
    for c_ref, o_ref in ((cs_ref, mods_ref), (cp_ref, modp_ref)):
        c = c_ref[...]
        part = jnp.dot((c * jax.nn.sigmoid(c)).astype(BF16), w, preferred_element_type=F32)

        @pl.when(j == 0)
        def _():
            o_ref[0] = part + bada_ref[pl.ds(pl.program_id(0), 1), :]

        @pl.when(j > 0)
        def _():
            o_ref[0] += part
    winb_ref[0] = win_ref[0].astype(BF16)
    wob_ref[0] = wo_ref[0].astype(BF16)
    wpoolb_ref[0] = wpool_ref[0].astype(BF16)
    for g in range(len(POOL_WINDOWS)):
        lo, hi = g * POOL_CG, (g + 1) * POOL_CG
        wfold_ref[0, :, lo:hi] = jnp.dot(win_ref[0, :, lo:hi], wpool_ref[0, g], precision=lax.Precision.HIGHEST,
                                         preferred_element_type=F32).astype(BF16)


def _prep(c_sample, c_prompt, w_ada, b_ada, w_in, w_o, w_pool):
    depth, d, d3 = w_ada.shape
    bs, bp = c_sample.shape[0], c_prompt.shape[0]
    rows = lambda l, j: (l, j, 0)
    whole = lambda l, j: (l, 0, 0)
    return pl.pallas_call(
        _prep_kernel,
        grid=(depth, d // PREP_ROWS),
        in_specs=[
            pl.BlockSpec((bs, PREP_ROWS), lambda l, j: (0, j)),
            pl.BlockSpec((bp, PREP_ROWS), lambda l, j: (0, j)),
            pl.BlockSpec((1, PREP_ROWS, d3), rows),
            _const_spec(b_ada.shape),
            pl.BlockSpec((1, PREP_ROWS, w_in.shape[2]), rows),
            pl.BlockSpec((1, PREP_ROWS, w_o.shape[2]), rows),
            pl.BlockSpec((1,) + w_pool.shape[1:], lambda l, j: (l, 0, 0, 0)),
        ],
        out_specs=[
            pl.BlockSpec((1, bs, d3), whole),
            pl.BlockSpec((1, bp, d3), whole),
            pl.BlockSpec((1, PREP_ROWS, w_in.shape[2]), rows),
            pl.BlockSpec((1, PREP_ROWS, w_o.shape[2]), rows),
            pl.BlockSpec((1,) + w_pool.shape[1:], lambda l, j: (l, 0, 0, 0)),
            pl.BlockSpec((1, PREP_ROWS, D_POOL), rows),
        ],
        out_shape=[
            jax.ShapeDtypeStruct((depth, bs, d3), F32),
            jax.ShapeDtypeStruct((depth, bp, d3), F32),
            jax.ShapeDtypeStruct(w_in.shape, BF16),
            jax.ShapeDtypeStruct(w_o.shape, BF16),
            jax.ShapeDtypeStruct(w_pool.shape, BF16),
            jax.ShapeDtypeStruct((depth, d, D_POOL), BF16),
        ],
        compiler_params=pltpu.CompilerParams(
            dimension_semantics=("arbitrary", "arbitrary"), vmem_limit_bytes=VMEM_LIMIT),
        name="prep",
    )(c_sample, c_prompt, w_ada, b_ada, w_in, w_o, w_pool)


def _rope(x, cs, sn):
    return x * cs + pltpu.roll(x, RET_DK // 2, x.ndim - 1) * sn


def _rms(x):
    return x * lax.rsqrt(jnp.mean(x * x, axis=-1, keepdims=True) + EPS)


def _prompt_kernel(x_ref, mod_ref, gpre_ref, gpost_ref, win_ref, wfold_ref, pscale_ref, wo_ref,
                   cs_ref, sn_ref, dmask_ref, qdec_ref, kdec_ref, gc_ref,
                   xo_ref, s_ref, nb_ref,
                   hb_ref, ubuf_ref, q_ref, qd_ref, k_ref, kd_ref, v_ref, sc_ref, sprev_ref, mix_ref, sg_ref,
                   *, tile, layer):
    t = pl.program_id(1)
    n_chunks = tile // RET_CHUNK
    blocks = [(r0, r0 + PROMPT_BLOCK) for r0 in range(0, tile, PROMPT_BLOCK)]
    o1, o2, o3, o4 = D_POOL, D_POOL + D_RET, D_POOL + 2 * D_RET, D_POOL + 3 * D_RET
    heads = [slice(hh * RET_DK, (hh + 1) * RET_DK) for hh in range(RET_HEADS)]
    chunks = [slice(c * RET_CHUNK, (c + 1) * RET_CHUNK) for c in range(n_chunks)]

    @pl.when(t == 0)
    def _():
        s_ref[...] = jnp.zeros_like(s_ref)
        ubuf_ref[0:HIST, :] = jnp.zeros((HIST, D_POOL), F32)

    m = mod_ref[pl.ds(pl.program_id(0), 1), :]
    shift, scl, gres = m[:, 0:D_MODEL], m[:, D_MODEL:2 * D_MODEL], m[:, 2 * D_MODEL:]
    pre_gain = gpre_ref[layer:layer + 1, :] * (1.0 + scl)
    post_gain = gres * gpost_ref[layer:layer + 1, :]

    def proj(r0, r1, c0, c1):
        return jnp.dot(hb_ref[r0:r1, :], win_ref[:, c0:c1], preferred_element_type=F32)

    zqk, kvs = {}, {}

    def pre_norm(j):
        r0, r1 = blocks[j]
        hb_ref[r0:r1, :] = (_rms(x_ref[0, r0:r1, :]) * pre_gain + shift).astype(BF16)

    def project(j):
        r0, r1 = blocks[j]
        ubuf_ref[HIST + r0:HIST + r1, :] = jnp.dot(hb_ref[r0:r1, :], wfold_ref[...], preferred_element_type=F32)
        zqk[j] = (proj(r0, r1, o1, o2), proj(r0, r1, o2, o3))
        v_ref[r0:r1, :] = proj(r0, r1, o3, o4).astype(BF16)

    def mixer_inputs(j):
        r0, r1 = blocks[j]
        row = lax.broadcasted_iota(jnp.int32, (r1 - r0, 1), 0) + (t * tile + r0)
        for g, w in enumerate(POOL_WINDOWS):
            lo, hi = g * POOL_CG, (g + 1) * POOL_CG
            u_ext = ubuf_ref[r0:r1 + HIST, lo:hi]
            acc = u_ext
            k = 1
            while k < w:
                acc = acc + pltpu.roll(acc, k, 0)
                k *= 2
            cnt = jnp.minimum(row + 1, w).astype(F32)
            mix_ref[r0:r1, lo:hi] = (acc[HIST:] / cnt - u_ext[HIST:]) * pscale_ref[layer:layer + 1, lo:hi]
        zq, zk = zqk.pop(j)
        cs, sn = cs_ref[r0:r1, :], sn_ref[r0:r1, :]
        for sl in heads:
            qr = _rope(zq[:, sl], cs, sn)
            kr = _rope(zk[:, sl], cs, sn) * K_SCALE
            q_ref[r0:r1, sl] = qr.astype(BF16)
            k_ref[r0:r1, sl] = kr.astype(BF16)
            for c in range(r0 // RET_CHUNK, r1 // RET_CHUNK):
                rs, ls = chunks[c], slice(c * RET_CHUNK - r0, (c + 1) * RET_CHUNK - r0)
                qd_ref[rs, sl] = (qr[ls] * qdec_ref[:, sl]).astype(BF16)
                kd_ref[rs, sl] = (kr[ls] * kdec_ref[:, sl]).astype(BF16)

    def small_matmuls(j):
        r0, r1 = blocks[j]
        for c in range(r0 // RET_CHUNK, r1 // RET_CHUNK):
            rs = chunks[c]
            for hh, sl in enumerate(heads):
                scores = lax.dot_general(q_ref[rs, sl], k_ref[rs, sl], (((1,), (1,)), ((), ())),
                                         preferred_element_type=F32)
                sc_ref[rs, sl] = (scores * dmask_ref[hh]).astype(BF16)
                kvs[c, hh] = lax.dot_general(kd_ref[rs, sl], v_ref[rs, sl], (((0,), (0,)), ((), ())),
                                             preferred_element_type=F32)

    def gate(j):
        r0, r1 = blocks[j]
        zg = proj(r0, r1, o4, D_IN)
        sg_ref[r0:r1, :] = zg * jax.nn.sigmoid(zg)

    def state_recurrence():
        for hh, sl in enumerate(heads):
            s = s_ref[0, hh]
            for c in range(n_chunks):
                sprev_ref[c * RET_HEADS + hh] = s.astype(BF16)
                s = gc_ref[0:1, sl] * s + kvs.pop((c, hh))
            s_ref[0, hh] = s

    def retention_out(j):
        r0, r1 = blocks[j]
        for c in range(r0 // RET_CHUNK, r1 // RET_CHUNK):
            rs = chunks[c]
            for hh, sl in enumerate(heads):
                lhs = jnp.concatenate([sc_ref[rs, sl], qd_ref[rs, sl]], axis=1)
                rhs = jnp.concatenate([v_ref[rs, sl], sprev_ref[c * RET_HEADS + hh]], axis=0)
                o_h = jnp.dot(lhs, rhs, preferred_element_type=F32)
                mix_ref[rs, D_POOL + hh * RET_DK:D_POOL + (hh + 1) * RET_DK] = _rms(o_h)

    def out_proj(j):
        r0, r1 = blocks[j]
        mixb = (mix_ref[r0:r1, :] * sg_ref[r0:r1, :]).astype(BF16)
        y = jnp.dot(mixb, wo_ref[...], preferred_element_type=F32)
        xo_ref[0, r0:r1, :] = x_ref[0, r0:r1, :] + _rms(y) * post_gain

    n = len(blocks)
    pre_norm(0)
    project(0)
    for j in range(n):
        if j + 1 < n:
            pre_norm(j + 1)
        mixer_inputs(j)
        if j + 1 < n:
            project(j + 1)
        else:
            ubuf_ref[0:HIST, :] = ubuf_ref[tile:tile + HIST, :]
            gate(0)
        small_matmuls(j)
    gate(1)
    state_recurrence()

    gates_left = list(range(2, n))
    for j in range(n):
        retention_out(j)
        if gates_left:
            gate(gates_left.pop(0))
        if j >= 1:
            out_proj(j - 1)
    out_proj(n - 1)

    @pl.when(t == pl.num_programs(1) - 1)
    def _():
        nb_ref[0] = jnp.dot(hb_ref[tile - HIST:tile, :], win_ref[:, 0:o1], preferred_element_type=F32)


def _prompt_layer(l, x, mod, g_pre, g_post, w_in, w_fold, pool_scale, w_o, tabs, stacked):
    depth = w_in.shape[0]
    b, seq, d = x.shape
    tile = PROMPT_TILE
    cs, sn, dmask, qdec, kdec, gc = tabs
    layer = lambda *_: (l, 0, 0)
    n_in = 14
    carried = () if stacked is None else tuple(stacked)

    def kern(*refs):
        _prompt_kernel(*refs[:n_in], *refs[n_in + len(carried):], tile=tile, layer=l)

    return pl.pallas_call(
        kern,
        grid=(b, seq // tile),
        in_specs=[
            pl.BlockSpec((1, tile, d), lambda i, t: (i, t, 0)),
            pl.BlockSpec((None, b, 3 * d), layer),
            _const_spec(g_pre.shape),
            _const_spec(g_post.shape),
            pl.BlockSpec((None, d, D_IN), layer, pipeline_mode=pl.Buffered(1)),
            pl.BlockSpec((None, d, D_POOL), layer, pipeline_mode=pl.Buffered(1)),
            _const_spec(pool_scale.shape),
            pl.BlockSpec((None, d, d), layer, pipeline_mode=pl.Buffered(1)),
            pl.BlockSpec((tile, 128), lambda i, t: (t, 0)),
            pl.BlockSpec((tile, 128), lambda i, t: (t, 0)),
            _const_spec(dmask.shape), _const_spec(qdec.shape), _const_spec(kdec.shape), _const_spec(gc.shape),
        ] + [pl.BlockSpec(memory_space=pl.ANY)] * len(carried),
        out_specs=[
            pl.BlockSpec((1, tile, d), lambda i, t: (i, t, 0)),
            pl.BlockSpec((None, 1, RET_HEADS, RET_DK, RET_DK), lambda i, t: (l, i, 0, 0, 0)),
            pl.BlockSpec((None, 1, HIST, D_POOL), lambda i, t: (l, i, 0, 0)),
        ],
        out_shape=[
            jax.ShapeDtypeStruct((b, seq, d), F32),
            jax.ShapeDtypeStruct((depth, b, RET_HEADS, RET_DK, RET_DK), F32),
            jax.ShapeDtypeStruct((depth, b, HIST, D_POOL), F32),
        ],
        input_output_aliases={n_in + i: 1 + i for i in range(len(carried))},
        scratch_shapes=[
            pltpu.VMEM((tile, d), BF16),
            pltpu.VMEM((HIST + tile, D_POOL), F32),
            pltpu.VMEM((tile, D_RET), BF16),
            pltpu.VMEM((tile, D_RET), BF16),
            pltpu.VMEM((tile, D_RET), BF16),
            pltpu.VMEM((tile, D_RET), BF16),
            pltpu.VMEM((tile, D_RET), BF16),
            pltpu.VMEM((tile, D_RET), BF16),
            pltpu.VMEM((tile // RET_CHUNK * RET_HEADS, RET_DK, RET_DK), BF16),
            pltpu.VMEM((tile, d), F32),
            pltpu.VMEM((tile, d), F32),
        ],
        compiler_params=pltpu.CompilerParams(
            dimension_semantics=("arbitrary", "arbitrary"), vmem_limit_bytes=VMEM_LIMIT),
        name=f"prompt_layer{l}",
    )(x, mod, g_pre, g_post, w_in, w_fold, pool_scale, w_o, cs, sn, dmask, qdec, kdec, gc, *carried)


def _sample_kernel(x_ref, mod_ref, gpre_ref, gpost_ref, win_ref, wpool_ref, pscale_ref, wo_ref,
                   cs_ref, sn_ref, dmask_ref, qdec_ref, kdec_ref, gc_ref, s0_ref, hist_ref,
                   xo_ref, s_ref, nb_ref,
                   usc_ref, mix_ref, sg_ref, *, bt, dec_seq):
    n = bt * dec_seq
    pool_buf = hist_ref.shape[0]
    rows = pl.ds(pl.multiple_of(pl.program_id(1) * bt, bt), bt)

    @pl.when(pl.program_id(0) == 0)
    def _():
        xo_ref[rows] = x_ref[...]

    m = mod_ref[...][:, None, :]
    shift, scl, gres = m[:, :, 0:D_MODEL], m[:, :, D_MODEL:2 * D_MODEL], m[:, :, 2 * D_MODEL:]

    x3 = xo_ref[rows]
    layer = pl.ds(pl.program_id(0), 1)
    h3 = _rms(x3) * (gpre_ref[layer, :][None] * (1.0 + scl)) + shift
    hb = h3.reshape(n, D_MODEL).astype(BF16)

    o1, o2, o3, o4 = D_POOL, D_POOL + D_RET, D_POOL + 2 * D_RET, D_POOL + 3 * D_RET
    heads = [slice(hh * RET_DK, (hh + 1) * RET_DK) for hh in range(RET_HEADS)]
    u = jnp.dot(hb, win_ref[:, 0:o1], preferred_element_type=F32)
    zq = jnp.dot(hb, win_ref[:, o1:o2], preferred_element_type=F32).reshape(bt, dec_seq, D_RET)
    zk = jnp.dot(hb, win_ref[:, o2:o3], preferred_element_type=F32).reshape(bt, dec_seq, D_RET)
    zv = jnp.dot(hb, win_ref[:, o3:o4], preferred_element_type=F32).reshape(bt, dec_seq, D_RET)
    zg = jnp.dot(hb, win_ref[:, o4:], preferred_element_type=F32)

    pooled = []
    for g, w in enumerate(POOL_WINDOWS):
        lo, hi = g * POOL_CG, (g + 1) * POOL_CG
        usc_ref[g] = u[:, lo:hi]
        rows_g = [hist_ref[r, :, lo:hi] for r in range(pool_buf)]
        rows_g += [usc_ref[g, pl.ds(i, bt, stride=dec_seq), :] for i in range(dec_seq)]
        for i in range(dec_seq):
            cur = pool_buf + i
            acc = rows_g[cur]
            for k in range(1, w):
                acc = acc + rows_g[cur - k]
            usc_ref[g, pl.ds(i, bt, stride=dec_seq), :] = acc / float(w) - rows_g[cur]
        pooled.append(usc_ref[g].astype(BF16))
        for r in range(pool_buf):
            nb_ref[r, :, lo:hi] = rows_g[dec_seq + r]

    cs, sn = cs_ref[...][None], sn_ref[...][None]
    qr = [_rope(zq[:, :, sl], cs, sn) for sl in heads]
    kr = [_rope(zk[:, :, sl], cs, sn) * K_SCALE for sl in heads]
    sg_ref[...] = zg * jax.nn.sigmoid(zg)

    for g in range(len(POOL_WINDOWS)):
        lo, hi = g * POOL_CG, (g + 1) * POOL_CG
        mix_ref[:, lo:hi] = jnp.dot(pooled[g], wpool_ref[g], preferred_element_type=F32) * pscale_ref[layer, :][:, lo:hi]
    scores = [jnp.einsum("bid,bjd->bij", qr[hh], kr[hh], preferred_element_type=F32) * dmask_ref[hh][None]
              for hh in range(RET_HEADS)]
    kv, cross = [], []
    for hh, sl in enumerate(heads):
        kv.append(jnp.einsum("bjd,bje->bde", (kr[hh] * kdec_ref[:, sl][None]).astype(BF16),
                             zv[:, :, sl].astype(BF16), preferred_element_type=F32))
        cross.append(jnp.einsum("bid,bde->bie", qr[hh] * qdec_ref[:, sl][None], s0_ref[:, hh],
                                preferred_element_type=F32))
    intra = [jnp.einsum("bij,bje->bie", scores[hh], zv[:, :, sl], preferred_element_type=F32)
             for hh, sl in enumerate(heads)]
    for hh, sl in enumerate(heads):
        s_ref[:, hh] = gc_ref[0:1, sl][None] * s0_ref[:, hh] + kv[hh]
        mix_ref[:, D_POOL + hh * RET_DK:D_POOL + (hh + 1) * RET_DK] = _rms(intra[hh] + cross[hh]).reshape(n, RET_DK)

    mixb = (mix_ref[...] * sg_ref[...]).astype(BF16)
    y = jnp.dot(mixb, wo_ref[...], preferred_element_type=F32)
    xo_ref[rows] = x3 + (gres * gpost_ref[layer, :][None]) * _rms(y).reshape(bt, dec_seq, D_MODEL)


def _sample_layers(x, mod, g_pre, g_post, w_in, w_pool, pool_scale, w_o, tabs, state_ret, state_pool):
    b, dec_seq, d = x.shape
    depth, pool_buf, _, _ = state_pool.shape
    bt = SAMPLE_BT
    cs, sn, dmask, qdec, kdec, gc = tabs
    layer = lambda l, i: (l, 0, 0)
    kern = functools.partial(_sample_kernel, bt=bt, dec_seq=dec_seq)
    return pl.pallas_call(
        kern,
        grid=(depth, b // bt),
        in_specs=[
            pl.BlockSpec((bt, dec_seq, d), lambda l, i: (i, 0, 0)),
            pl.BlockSpec((None, bt, 3 * d), lambda l, i: (l, i, 0)),
            _const_spec(g_pre.shape),
            _const_spec(g_post.shape),
            pl.BlockSpec((None, d, D_IN), layer),
            pl.BlockSpec((None, len(POOL_WINDOWS), POOL_CG, POOL_CG), lambda l, i: (l, 0, 0, 0)),
            _const_spec(pool_scale.shape),
            pl.BlockSpec((None, d, d), layer),
            _const_spec(cs.shape), _const_spec(sn.shape),
            _const_spec(dmask.shape), _const_spec(qdec.shape), _const_spec(kdec.shape), _const_spec(gc.shape),
            pl.BlockSpec((None, bt, RET_HEADS, RET_DK, RET_DK), lambda l, i: (l, i, 0, 0, 0)),
            pl.BlockSpec((None, pool_buf, bt, D_POOL), lambda l, i: (l, 0, i, 0)),
        ],
        out_specs=[
            pl.BlockSpec((b, dec_seq, d), lambda l, i: (0, 0, 0)),
            pl.BlockSpec((None, bt, RET_HEADS, RET_DK, RET_DK), lambda l, i: (l, i, 0, 0, 0)),
            pl.BlockSpec((None, pool_buf, bt, D_POOL), lambda l, i: (l, 0, i, 0)),
        ],
        out_shape=[
            jax.ShapeDtypeStruct((b, dec_seq, d), F32),
            jax.ShapeDtypeStruct((depth, b, RET_HEADS, RET_DK, RET_DK), F32),
            jax.ShapeDtypeStruct((depth, pool_buf, b, D_POOL), F32),
        ],
        scratch_shapes=[
            pltpu.VMEM((len(POOL_WINDOWS), bt * dec_seq, POOL_CG), F32),
            pltpu.VMEM((bt * dec_seq, d), F32),
            pltpu.VMEM((bt * dec_seq, d), F32),
        ],
        compiler_params=pltpu.CompilerParams(
            dimension_semantics=("arbitrary", "arbitrary"), vmem_limit_bytes=VMEM_LIMIT),
        name="sample_layers",
    )(x, mod, g_pre, g_post, w_in, w_pool, pool_scale, w_o, cs, sn, dmask, qdec, kdec, gc, state_ret, state_pool)


def kernel(x_prompt, x_sample, c_prompt, c_sample, state_ret, state_pool, w_ada, b_ada,
           g_pre, g_post, w_in, w_pool, pool_scale, w_o):
    depth = w_in.shape[0]
    bp, seq, d = x_prompt.shape
    bs, dec_seq, _ = x_sample.shape
    assert d == D_MODEL and seq % PROMPT_TILE == 0 and bs % SAMPLE_BT == 0 and dec_seq == 8

    (csp, snp, css, sns, dmp, qdp, kdp, gcp, dms, qds, kds, gcs) = _make_tables(seq, dec_seq, PAST_LEN)
    tabs_p = (csp, snp, dmp, qdp, kdp, gcp)
    tabs_s = (css, sns, dms, qds, kds, gcs)

    mod_s, mod_p, w_in_b, w_o_b, w_pool_b, w_fold_b = _prep(c_sample, c_prompt, w_ada, b_ada, w_in, w_o, w_pool)
    hs, ret_s, pool_s = _sample_layers(x_sample, mod_s, g_pre, g_post, w_in_b, w_pool_b, pool_scale, w_o_b,
                                       tabs_s, state_ret, jnp.transpose(state_pool, (0, 2, 1, 3)))
    pool_s = jnp.transpose(pool_s, (0, 2, 1, 3))
    hp, stacked = x_prompt, None
    for l in range(depth):
        hp, *stacked = _prompt_layer(l, hp, mod_p, g_pre, g_post, w_in_b, w_fold_b, pool_scale, w_o_b, tabs_p,
                                     stacked)
    ret_p, hist_p = stacked
    return (hp, hs, ret_p, hist_p[:, :, HIST - state_pool.shape[2]:], ret_s, pool_s)
```

```python
import functools

import jax
import jax.numpy as jnp
from jax import lax
from jax.experimental import pallas as pl
from jax.experimental.pallas import tpu as pltpu

F32 = jnp.float32
BF16 = jnp.bfloat16

D_MODEL = 1024
D_POOL = 512
D_RET = 512
POOL_WINDOWS = (2, 4, 8, 16)
POOL_CG = 128
HIST = 16
RET_HEADS = 4
RET_DK = 128
RET_CHUNK = 128
ROPE_BASE = 10000.0
PAST_LEN = 16384
ROPE_ROWS = 128
D_IN = D_POOL + 3 * D_RET + D_MODEL
EPS = 1e-6
K_SCALE = RET_DK ** -0.5

PROMPT_TILE = 1024
PROMPT_BLOCK = 256
SAMPLE_BT = 32
PREP_ROWS = 512
VMEM_LIMIT = 60 * 1024 * 1024


def _const_spec(shape):
    return pl.BlockSpec(shape, lambda *_: (0,) * len(shape))


def _decay_tables(lg_lane, lg_row, c, dmask_ref, qdec_ref, kdec_ref, gc_ref):
    idx = lax.broadcasted_iota(jnp.int32, (c, D_RET), 0).astype(F32)
    qdec_ref[...] = jnp.exp((idx + 1.0) * lg_lane)
    kdec_ref[...] = jnp.exp((c - 1.0 - idx) * lg_lane)
    gc_ref[...] = jnp.exp(jnp.full((8, D_RET), float(c), F32) * lg_lane)
    ii = lax.broadcasted_iota(jnp.int32, (c, c), 0)
    jj = lax.broadcasted_iota(jnp.int32, (c, c), 1)
    diff = ii - jj
    for h in range(RET_HEADS):
        dec = jnp.exp(jnp.maximum(diff, 0).astype(F32) * lg_row[h])
        dmask_ref[h] = jnp.where(diff >= 0, dec, 0.0)


def _tables_kernel(inv_ref, csp_ref, snp_ref, css_ref, sns_ref,
                   dmp_ref, qdp_ref, kdp_ref, gcp_ref,
                   dms_ref, qds_ref, kds_ref, gcs_ref, *, seq, dec_seq, past_len):
    inv = inv_ref[...]
    lane = lax.broadcasted_iota(jnp.int32, (1, 2 * 64), 1)
    sign = jnp.where(lane < 64, -1.0, 1.0).astype(F32)

    def angles(n, start, step):
        pos = (lax.broadcasted_iota(jnp.int32, (n, 128), 0) * step + start).astype(F32)
        return pos * inv

    ang = angles(dec_seq, past_len, 1)
    css_ref[...] = jnp.cos(ang)
    sns_ref[...] = jnp.sin(ang) * sign
    lo = angles(ROPE_ROWS, 0, 1)
    hi = angles(seq // ROPE_ROWS, 0, ROPE_ROWS)
    c_lo, s_lo, c_hi, s_hi = jnp.cos(lo), jnp.sin(lo), jnp.cos(hi), jnp.sin(hi)
    for r in range(seq // ROPE_ROWS):
        rows = slice(r * ROPE_ROWS, (r + 1) * ROPE_ROWS)
        csp_ref[rows, :] = c_hi[r:r + 1] * c_lo - s_hi[r:r + 1] * s_lo
        snp_ref[rows, :] = (s_hi[r:r + 1] * c_lo + c_hi[r:r + 1] * s_lo) * sign

    head_lane = (lax.broadcasted_iota(jnp.int32, (1, D_RET), 1) // RET_DK).astype(F32)
    lg_lane = jnp.log(1.0 - jnp.exp2(-5.0 - head_lane))
    for c, refs in ((RET_CHUNK, (dmp_ref, qdp_ref, kdp_ref, gcp_ref)),
                    (dec_seq, (dms_ref, qds_ref, kds_ref, gcs_ref))):
        lg_row = [jnp.log(1.0 - jnp.exp2(jnp.full((1, c), -5.0 - h, F32))) for h in range(RET_HEADS)]
        _decay_tables(lg_lane, lg_row, c, *refs)


def _make_tables(seq, dec_seq, past_len):
    half = RET_DK // 2
    inv = 1.0 / (ROPE_BASE ** (jnp.arange(half, dtype=F32) / half))
    inv2 = jnp.concatenate([inv, inv])[None, :]
    c, cs = RET_CHUNK, dec_seq
    out_shape = (
        jax.ShapeDtypeStruct((seq, 128), F32), jax.ShapeDtypeStruct((seq, 128), F32),
        jax.ShapeDtypeStruct((dec_seq, 128), F32), jax.ShapeDtypeStruct((dec_seq, 128), F32),
        jax.ShapeDtypeStruct((RET_HEADS, c, c), F32), jax.ShapeDtypeStruct((c, D_RET), F32),
        jax.ShapeDtypeStruct((c, D_RET), F32), jax.ShapeDtypeStruct((8, D_RET), F32),
        jax.ShapeDtypeStruct((RET_HEADS, cs, cs), F32), jax.ShapeDtypeStruct((cs, D_RET), F32),
        jax.ShapeDtypeStruct((cs, D_RET), F32), jax.ShapeDtypeStruct((8, D_RET), F32),
    )
    return pl.pallas_call(
        functools.partial(_tables_kernel, seq=seq, dec_seq=dec_seq, past_len=past_len),
        out_shape=out_shape,
        name="tables",
    )(inv2)


def _prep_kernel(cs_ref, cp_ref, wada_ref, bada_ref, win_ref, wo_ref, wpool_ref,
                 mods_ref, modp_ref, winb_ref, wob_ref, wpoolb_ref, wfold_ref):
    j = pl.program_id(1)
    w = wada_ref[0].astype(BF16)
    for c_ref, o_ref in ((cs_ref, mods_ref), (cp_ref, modp_ref)):
        c = c_ref[...]
        part = jnp.dot((c * jax.nn.sigmoid(c)).astype(BF16), w, preferred_element_type=F32)

        @pl.when(j == 0)
        def _():
            o_ref[0] = part + bada_ref[pl.ds(pl.program_id(0), 1), :]

        @pl.when(j > 0)
        def _():
            o_ref[0] += part
    winb_ref[0] = win_ref[0].astype(BF16)
    wob_ref[0] = wo_ref[0].astype(BF16)
    wpoolb_ref[0] = wpool_ref[0].astype(BF16)
    for g in range(len(POOL_WINDOWS)):
        lo, hi = g * POOL_CG, (g + 1) * POOL_CG
        wfold_ref[0, :, lo:hi] = jnp.dot(win_ref[0, :, lo:hi], wpool_ref[0, g], precision=lax.Precision.HIGHEST,
                                         preferred_element_type=F32).astype(BF16)


def _prep(c_sample, c_prompt, w_ada, b_ada, w_in, w_o, w_pool):
    depth, d, d3 = w_ada.shape
    bs, bp = c_sample.shape[0], c_prompt.shape[0]
    rows = lambda l, j: (l, j, 0)
    whole = lambda l, j: (l, 0, 0)
    return pl.pallas_call(
        _prep_kernel,
        grid=(depth, d // PREP_ROWS),
        in_specs=[
            pl.BlockSpec((bs, PREP_ROWS), lambda l, j: (0, j)),
            pl.BlockSpec((bp, PREP_ROWS), lambda l, j: (0, j)),
            pl.BlockSpec((1, PREP_ROWS, d3), rows),
            _const_spec(b_ada.shape),
            pl.BlockSpec((1, PREP_ROWS, w_in.shape[2]), rows),
            pl.BlockSpec((1, PREP_ROWS, w_o.shape[2]), rows),
            pl.BlockSpec((1,) + w_pool.shape[1:], lambda l, j: (l, 0, 0, 0)),
        ],
        out_specs=[
            pl.BlockSpec((1, bs, d3), whole),
            pl.BlockSpec((1, bp, d3), whole),
            pl.BlockSpec((1, PREP_ROWS, w_in.shape[2]), rows),
            pl.BlockSpec((1, PREP_ROWS, w_o.shape[2]), rows),
            pl.BlockSpec((1,) + w_pool.shape[1:], lambda l, j: (l, 0, 0, 0)),
            pl.BlockSpec((1, PREP_ROWS, D_POOL), rows),
        ],
        out_shape=[
            jax.ShapeDtypeStruct((depth, bs, d3), F32),
            jax.ShapeDtypeStruct((depth, bp, d3), F32),
            jax.ShapeDtypeStruct(w_in.shape, BF16),
            jax.ShapeDtypeStruct(w_o.shape, BF16),
            jax.ShapeDtypeStruct(w_pool.shape, BF16),
            jax.ShapeDtypeStruct((depth, d, D_POOL), BF16),
        ],
        compiler_params=pltpu.CompilerParams(
            dimension_semantics=("arbitrary", "arbitrary"), vmem_limit_bytes=VMEM_LIMIT),
        name="prep",
    )(c_sample, c_prompt, w_ada, b_ada, w_in, w_o, w_pool)


def _rope(x, cs, sn):
    return x * cs + pltpu.roll(x, RET_DK // 2, x.ndim - 1) * sn


def _rms(x):
    return x * lax.rsqrt(jnp.mean(x * x, axis=-1, keepdims=True) + EPS)


def _prompt_kernel(x_ref, mod_ref, gpre_ref, gpost_ref, win_ref, wfold_ref, pscale_ref, wo_ref,
                   cs_ref, sn_ref, dmask_ref, qdec_ref, kdec_ref, gc_ref,
                   xo_ref, s_ref, nb_ref,
                   hb_ref, ubuf_ref, q_ref, qd_ref, k_ref, kd_ref, v_ref, sc_ref, sprev_ref, mix_ref, sg_ref,
                   *, tile, layer):
    t = pl.program_id(1)
    n_chunks = tile // RET_CHUNK
    blocks = [(r0, r0 + PROMPT_BLOCK) for r0 in range(0, tile, PROMPT_BLOCK)]
    o1, o2, o3, o4 = D_POOL, D_POOL + D_RET, D_POOL + 2 * D_RET, D_POOL + 3 * D_RET
    heads = [slice(hh * RET_DK, (hh + 1) * RET_DK) for hh in range(RET_HEADS)]
    chunks = [slice(c * RET_CHUNK, (c + 1) * RET_CHUNK) for c in range(n_chunks)]

    @pl.when(t == 0)
    def _():
        s_ref[...] = jnp.zeros_like(s_ref)
        ubuf_ref[0:HIST, :] = jnp.zeros((HIST, D_POOL), F32)

    m = mod_ref[pl.ds(pl.program_id(0), 1), :]
    shift, scl, gres = m[:, 0:D_MODEL], m[:, D_MODEL:2 * D_MODEL], m[:, 2 * D_MODEL:]
    pre_gain = gpre_ref[layer:layer + 1, :] * (1.0 + scl)
    post_gain = gres * gpost_ref[layer:layer + 1, :]

    def proj(r0, r1, c0, c1):
        return jnp.dot(hb_ref[r0:r1, :], win_ref[:, c0:c1], preferred_element_type=F32)

    zqk, kvs = {}, {}

    def pre_norm(j):
        r0, r1 = blocks[j]
        hb_ref[r0:r1, :] = (_rms(x_ref[0, r0:r1, :]) * pre_gain + shift).astype(BF16)

    def project(j):
        r0, r1 = blocks[j]
        ubuf_ref[HIST + r0:HIST + r1, :] = jnp.dot(hb_ref[r0:r1, :], wfold_ref[...], preferred_element_type=F32)
        zqk[j] = (proj(r0, r1, o1, o2), proj(r0, r1, o2, o3))
        v_ref[r0:r1, :] = proj(r0, r1, o3, o4).astype(BF16)

    def mixer_inputs(j):
        r0, r1 = blocks[j]
        row = lax.broadcasted_iota(jnp.int32, (r1 - r0, 1), 0) + (t * tile + r0)
        for g, w in enumerate(POOL_WINDOWS):
            lo, hi = g * POOL_CG, (g + 1) * POOL_CG
            u_ext = ubuf_ref[r0:r1 + HIST, lo:hi]
            acc = u_ext
            k = 1
            while k < w:
                acc = acc + pltpu.roll(acc, k, 0)
                k *= 2
            cnt = jnp.minimum(row + 1, w).astype(F32)
            mix_ref[r0:r1, lo:hi] = (acc[HIST:] / cnt - u_ext[HIST:]) * pscale_ref[layer:layer + 1, lo:hi]
        zq, zk = zqk.pop(j)
        cs, sn = cs_ref[r0:r1, :], sn_ref[r0:r1, :]
        for sl in heads:
            qr = _rope(zq[:, sl], cs, sn)
            kr = _rope(zk[:, sl], cs, sn) * K_SCALE
            q_ref[r0:r1, sl] = qr.astype(BF16)
            k_ref[r0:r1, sl] = kr.astype(BF16)
            for c in range(r0 // RET_CHUNK, r1 // RET_CHUNK):
                rs, ls = chunks[c], slice(c * RET_CHUNK - r0, (c + 1) * RET_CHUNK - r0)
                qd_ref[rs, sl] = (qr[ls] * qdec_ref[:, sl]).astype(BF16)
                kd_ref[rs, sl] = (kr[ls] * kdec_ref[:, sl]).astype(BF16)

    def small_matmuls(j):
        r0, r1 = blocks[j]
        for c in range(r0 // RET_CHUNK, r1 // RET_CHUNK):
            rs = chunks[c]
            for hh, sl in enumerate(heads):
                scores = lax.dot_general(q_ref[rs, sl], k_ref[rs, sl], (((1,), (1,)), ((), ())),
                                         preferred_element_type=F32)
                sc_ref[rs, sl] = (scores * dmask_ref[hh]).astype(BF16)
                kvs[c, hh] = lax.dot_general(kd_ref[rs, sl], v_ref[rs, sl], (((0,), (0,)), ((), ())),
                                             preferred_element_type=F32)

    def gate(j):
        r0, r1 = blocks[j]
        zg = proj(r0, r1, o4, D_IN)
        sg_ref[r0:r1, :] = zg * jax.nn.sigmoid(zg)

    def state_recurrence():
        for hh, sl in enumerate(heads):
            s = s_ref[0, hh]
            for c in range(n_chunks):
                sprev_ref[c * RET_HEADS + hh] = s.astype(BF16)
                s = gc_ref[0:1, sl] * s + kvs.pop((c, hh))
            s_ref[0, hh] = s

    def retention_out(j):
        r0, r1 = blocks[j]
        for c in range(r0 // RET_CHUNK, r1 // RET_CHUNK):
            rs = chunks[c]
            for hh, sl in enumerate(heads):
                lhs = jnp.concatenate([sc_ref[rs, sl], qd_ref[rs, sl]], axis=1)
                rhs = jnp.concatenate([v_ref[rs, sl], sprev_ref[c * RET_HEADS + hh]], axis=0)
                o_h = jnp.dot(lhs, rhs, preferred_element_type=F32)
                mix_ref[rs, D_POOL + hh * RET_DK:D_POOL + (hh + 1) * RET_DK] = _rms(o_h)

    def out_proj(j):
        r0, r1 = blocks[j]
        mixb = (mix_ref[r0:r1, :] * sg_ref[r0:r1, :]).astype(BF16)
        y = jnp.dot(mixb, wo_ref[...], preferred_element_type=F32)
        xo_ref[0, r0:r1, :] = x_ref[0, r0:r1, :] + _rms(y) * post_gain

    n = len(blocks)
    pre_norm(0)
    project(0)
    for j in range(n):
        if j + 1 < n:
            pre_norm(j + 1)
        mixer_inputs(j)
        if j + 1 < n:
            project(j + 1)
        else:
            ubuf_ref[0:HIST, :] = ubuf_ref[tile:tile + HIST, :]
            gate(0)
        small_matmuls(j)
    gate(1)
    state_recurrence()

    gates_left = list(range(2, n))
    for j in range(n):
        retention_out(j)
        if gates_left:
            gate(gates_left.pop(0))
        if j >= 1:
            out_proj(j - 1)
    out_proj(n - 1)

    @pl.when(t == pl.num_programs(1) - 1)
    def _():
        nb_ref[0] = jnp.dot(hb_ref[tile - HIST:tile, :], win_ref[:, 0:o1], preferred_element_type=F32)


def _prompt_layer(l, x, mod, g_pre, g_post, w_in, w_fold, pool_scale, w_o, tabs, stacked):
    depth = w_in.shape[0]
    b, seq, d = x.shape
    tile = PROMPT_TILE
    cs, sn, dmask, qdec, kdec, gc = tabs
    layer = lambda *_: (l, 0, 0)
    n_in = 14
    carried = () if stacked is None else tuple(stacked)

    def kern(*refs):
        _prompt_kernel(*refs[:n_in], *refs[n_in + len(carried):], tile=tile, layer=l)

    return pl.pallas_call(
        kern,
        grid=(b, seq // tile),
        in_specs=[
            pl.BlockSpec((1, tile, d), lambda i, t: (i, t, 0)),
            pl.BlockSpec((None, b, 3 * d), layer),
            _const_spec(g_pre.shape),
            _const_spec(g_post.shape),
            pl.BlockSpec((None, d, D_IN), layer, pipeline_mode=pl.Buffered(1)),
            pl.BlockSpec((None, d, D_POOL), layer, pipeline_mode=pl.Buffered(1)),
            _const_spec(pool_scale.shape),
            pl.BlockSpec((None, d, d), layer, pipeline_mode=pl.Buffered(1)),
            pl.BlockSpec((tile, 128), lambda i, t: (t, 0)),
            pl.BlockSpec((tile, 128), lambda i, t: (t, 0)),
            _const_spec(dmask.shape), _const_spec(qdec.shape), _const_spec(kdec.shape), _const_spec(gc.shape),
        ] + [pl.BlockSpec(memory_space=pl.ANY)] * len(carried),
        out_specs=[
            pl.BlockSpec((1, tile, d), lambda i, t: (i, t, 0)),
            pl.BlockSpec((None, 1, RET_HEADS, RET_DK, RET_DK), lambda i, t: (l, i, 0, 0, 0)),
            pl.BlockSpec((None, 1, HIST, D_POOL), lambda i, t: (l, i, 0, 0)),
        ],
        out_shape=[
            jax.ShapeDtypeStruct((b, seq, d), F32),
            jax.ShapeDtypeStruct((depth, b, RET_HEADS, RET_DK, RET_DK), F32),
            jax.ShapeDtypeStruct((depth, b, HIST, D_POOL), F32),
        ],
        input_output_aliases={n_in + i: 1 + i for i in range(len(carried))},
        scratch_shapes=[
            pltpu.VMEM((tile, d), BF16),
            pltpu.VMEM((HIST + tile, D_POOL), F32),
            pltpu.VMEM((tile, D_RET), BF16),
            pltpu.VMEM((tile, D_RET), BF16),
            pltpu.VMEM((tile, D_RET), BF16),
            pltpu.VMEM((tile, D_RET), BF16),
            pltpu.VMEM((tile, D_RET), BF16),
            pltpu.VMEM((tile, D_RET), BF16),
            pltpu.VMEM((tile // RET_CHUNK * RET_HEADS, RET_DK, RET_DK), BF16),
            pltpu.VMEM((tile, d), F32),
            pltpu.VMEM((tile, d), F32),
        ],
        compiler_params=pltpu.CompilerParams(
            dimension_semantics=("arbitrary", "arbitrary"), vmem_limit_bytes=VMEM_LIMIT),
        name=f"prompt_layer{l}",
    )(x, mod, g_pre, g_post, w_in, w_fold, pool_scale, w_o, cs, sn, dmask, qdec, kdec, gc, *carried)


def _sample_kernel(x_ref, mod_ref, gpre_ref, gpost_ref, win_ref, wpool_ref, pscale_ref, wo_ref,
                   cs_ref, sn_ref, dmask_ref, qdec_ref, kdec_ref, gc_ref, s0_ref, hist_ref,
                   xo_ref, s_ref, nb_ref,
                   usc_ref, mix_ref, sg_ref, *, bt, dec_seq):
    n = bt * dec_seq
    pool_buf = hist_ref.shape[0]
    rows = pl.ds(pl.multiple_of(pl.program_id(1) * bt, bt), bt)

    @pl.when(pl.program_id(0) == 0)
    def _():
        xo_ref[rows] = x_ref[...]

    m = mod_ref[...][:, None, :]
    shift, scl, gres = m[:, :, 0:D_MODEL], m[:, :, D_MODEL:2 * D_MODEL], m[:, :, 2 * D_MODEL:]

    x3 = xo_ref[rows]
    layer = pl.ds(pl.program_id(0), 1)
    h3 = _rms(x3) * (gpre_ref[layer, :][None] * (1.0 + scl)) + shift
    hb = h3.reshape(n, D_MODEL).astype(BF16)

    o1, o2, o3, o4 = D_POOL, D_POOL + D_RET, D_POOL + 2 * D_RET, D_POOL + 3 * D_RET
    heads = [slice(hh * RET_DK, (hh + 1) * RET_DK) for hh in range(RET_HEADS)]
    u = jnp.dot(hb, win_ref[:, 0:o1], preferred_element_type=F32)
    zq = jnp.dot(hb, win_ref[:, o1:o2], preferred_element_type=F32).reshape(bt, dec_seq, D_RET)
    zk = jnp.dot(hb, win_ref[:, o2:o3], preferred_element_type=F32).reshape(bt, dec_seq, D_RET)
    zv = jnp.dot(hb, win_ref[:, o3:o4], preferred_element_type=F32).reshape(bt, dec_seq, D_RET)
    zg = jnp.dot(hb, win_ref[:, o4:], preferred_element_type=F32)

    pooled = []
    for g, w in enumerate(POOL_WINDOWS):
        lo, hi = g * POOL_CG, (g + 1) * POOL_CG
        usc_ref[g] = u[:, lo:hi]
        rows_g = [hist_ref[r, :, lo:hi] for r in range(pool_buf)]
        rows_g += [usc_ref[g, pl.ds(i, bt, stride=dec_seq), :] for i in range(dec_seq)]
        for i in range(dec_seq):
            cur = pool_buf + i
            acc = rows_g[cur]
            for k in range(1, w):
                acc = acc + rows_g[cur - k]
            usc_ref[g, pl.ds(i, bt, stride=dec_seq), :] = acc / float(w) - rows_g[cur]
        pooled.append(usc_ref[g].astype(BF16))
        for r in range(pool_buf):
            nb_ref[r, :, lo:hi] = rows_g[dec_seq + r]

    cs, sn = cs_ref[...][None], sn_ref[...][None]
    qr = [_rope(zq[:, :, sl], cs, sn) for sl in heads]
    kr = [_rope(zk[:, :, sl], cs, sn) * K_SCALE for sl in heads]
    sg_ref[...] = zg * jax.nn.sigmoid(zg)

    for g in range(len(POOL_WINDOWS)):
        lo, hi = g * POOL_CG, (g + 1) * POOL_CG
        mix_ref[:, lo:hi] = jnp.dot(pooled[g], wpool_ref[g], preferred_element_type=F32) * pscale_ref[layer, :][:, lo:hi]
    scores = [jnp.einsum("bid,bjd->bij", qr[hh], kr[hh], preferred_element_type=F32) * dmask_ref[hh][None]
              for hh in range(RET_HEADS)]
    kv, cross = [], []
    for hh, sl in enumerate(heads):
        kv.append(jnp.einsum("bjd,bje->bde", (kr[hh] * kdec_ref[:, sl][None]).astype(BF16),
                             zv[:, :, sl].astype(BF16), preferred_element_type=F32))
        cross.append(jnp.einsum("bid,bde->bie", qr[hh] * qdec_ref[:, sl][None], s0_ref[:, hh],
                                preferred_element_type=F32))
    intra = [jnp.einsum("bij,bje->bie", scores[hh], zv[:, :, sl], preferred_element_type=F32)
             for hh, sl in enumerate(heads)]
    for hh, sl in enumerate(heads):
        s_ref[:, hh] = gc_ref[0:1, sl][None] * s0_ref[:, hh] + kv[hh]
        mix_ref[:, D_POOL + hh * RET_DK:D_POOL + (hh + 1) * RET_DK] = _rms(intra[hh] + cross[hh]).reshape(n, RET_DK)

    mixb = (mix_ref[...] * sg_ref[...]).astype(BF16)
    y = jnp.dot(mixb, wo_ref[...], preferred_element_type=F32)
    xo_ref[rows] = x3 + (gres * gpost_ref[layer, :][None]) * _rms(y).reshape(bt, dec_seq, D_MODEL)


def _sample_layers(x, mod, g_pre, g_post, w_in, w_pool, pool_scale, w_o, tabs, state_ret, state_pool):
    b, dec_seq, d = x.shape
    depth, pool_buf, _, _ = state_pool.shape
    bt = SAMPLE_BT
    cs, sn, dmask, qdec, kdec, gc = tabs
    layer = lambda l, i: (l, 0, 0)
    kern = functools.partial(_sample_kernel, bt=bt, dec_seq=dec_seq)
    return pl.pallas_call(
        kern,
        grid=(depth, b // bt),
        in_specs=[
            pl.BlockSpec((bt, dec_seq, d), lambda l, i: (i, 0, 0)),
            pl.BlockSpec((None, bt, 3 * d), lambda l, i: (l, i, 0)),
            _const_spec(g_pre.shape),
            _const_spec(g_post.shape),
            pl.BlockSpec((None, d, D_IN), layer, pipeline_mode=pl.Buffered(1)),
            pl.BlockSpec((None, len(POOL_WINDOWS), POOL_CG, POOL_CG), lambda l, i: (l, 0, 0, 0)),
            _const_spec(pool_scale.shape),
            pl.BlockSpec((None, d, d), layer, pipeline_mode=pl.Buffered(1)),
            _const_spec(cs.shape), _const_spec(sn.shape),
            _const_spec(dmask.shape), _const_spec(qdec.shape), _const_spec(kdec.shape), _const_spec(gc.shape),
            pl.BlockSpec((None, bt, RET_HEADS, RET_DK, RET_DK), lambda l, i: (l, i, 0, 0, 0)),
            pl.BlockSpec((None, pool_buf, bt, D_POOL), lambda l, i: (l, 0, i, 0)),
        ],
        out_specs=[
            pl.BlockSpec((b, dec_seq, d), lambda l, i: (0, 0, 0)),
            pl.BlockSpec((None, bt, RET_HEADS, RET_DK, RET_DK), lambda l, i: (l, i, 0, 0, 0)),
            pl.BlockSpec((None, pool_buf, bt, D_POOL), lambda l, i: (l, 0, i, 0)),
        ],
        out_shape=[
            jax.ShapeDtypeStruct((b, dec_seq, d), F32),
            jax.ShapeDtypeStruct((depth, b, RET_HEADS, RET_DK, RET_DK), F32),
            jax.ShapeDtypeStruct((depth, pool_buf, b, D_POOL), F32),
        ],
        scratch_shapes=[
            pltpu.VMEM((len(POOL_WINDOWS), bt * dec_seq, POOL_CG), F32),
            pltpu.VMEM((bt * dec_seq, d), F32),
            pltpu.VMEM((bt * dec_seq, d), F32),
        ],
        compiler_params=pltpu.CompilerParams(
            dimension_semantics=("arbitrary", "arbitrary"), vmem_limit_bytes=VMEM_LIMIT),
        name="sample_layers",
    )(x, mod, g_pre, g_post, w_in, w_pool, pool_scale, w_o, cs, sn, dmask, qdec, kdec, gc, state_ret, state_pool)


def kernel(x_prompt, x_sample, c_prompt, c_sample, state_ret, state_pool, w_ada, b_ada,
           g_pre, g_post, w_in, w_pool, pool_scale, w_o):
    depth = w_in.shape[0]
    bp, seq, d = x_prompt.shape
    bs, dec_seq, _ = x_sample.shape
    assert d == D_MODEL and seq % PROMPT_TILE == 0 and bs % SAMPLE_BT == 0 and dec_seq == 8

    (csp, snp, css, sns, dmp, qdp, kdp, gcp, dms, qds, kds, gcs) = _make_tables(seq, dec_seq, PAST_LEN)
    tabs_p = (csp, snp, dmp, qdp, kdp, gcp)
    tabs_s = (css, sns, dms, qds, kds, gcs)

    mod_s, mod_p, w_in_b, w_o_b, w_pool_b, w_fold_b = _prep(c_sample, c_prompt, w_ada, b_ada, w_in, w_o, w_pool)
    hs, ret_s, pool_s = _sample_layers(x_sample, mod_s, g_pre, g_post, w_in_b, w_pool_b, pool_scale, w_o_b,
                                       tabs_s, state_ret, jnp.transpose(state_pool, (0, 2, 1, 3)))
    pool_s = jnp.transpose(pool_s, (0, 2, 1, 3))
    hp, stacked = x_prompt, None
    for l in range(depth):
        hp, *stacked = _prompt_layer(l, hp, mod_p, g_pre, g_post, w_in_b, w_fold_b, pool_scale, w_o_b, tabs_p,
                                     stacked)
    ret_p, hist_p = stacked
    return (hp, hs, ret_p, hist_p[:, :, HIST - state_pool.shape[2]:], ret_s, pool_s)
```

```python
import functools

import jax
import jax.numpy as jnp
from jax import lax
from jax.experimental import pallas as pl
from jax.experimental.pallas import tpu as pltpu

F32 = jnp.float32
BF16 = jnp.bfloat16

D_MODEL = 1024
D_POOL = 512
D_RET = 512
POOL_WINDOWS = (2, 4, 8, 16)
POOL_CG = 128
HIST = 16
RET_HEADS = 4
RET_DK = 128
RET_CHUNK = 128
ROPE_BASE = 10000.0
PAST_LEN = 16384
ROPE_ROWS = 128
D_IN = D_POOL + 3 * D_RET + D_MODEL
EPS = 1e-6
K_SCALE = RET_DK ** -0.5

PROMPT_TILE = 1024
PROMPT_BLOCK = 256
SAMPLE_BT = 32
PREP_ROWS = 512
VMEM_LIMIT = 60 * 1024 * 1024


def _const_spec(shape):
    return pl.BlockSpec(shape, lambda *_: (0,) * len(shape))


def _decay_tables(lg_lane, lg_row, c, dmask_ref, qdec_ref, kdec_ref, gc_ref):
    idx = lax.broadcasted_iota(jnp.int32, (c, D_RET), 0).astype(F32)
    qdec_ref[...] = jnp.exp((idx + 1.0) * lg_lane)
    kdec_ref[...] = jnp.exp((c - 1.0 - idx) * lg_lane)
    gc_ref[...] = jnp.exp(jnp.full((8, D_RET), float(c), F32) * lg_lane)
    ii = lax.broadcasted_iota(jnp.int32, (c, c), 0)
    jj = lax.broadcasted_iota(jnp.int32, (c, c), 1)
    diff = ii - jj
    for h in range(RET_HEADS):
        dec = jnp.exp(jnp.maximum(diff, 0).astype(F32) * lg_row[h])
        dmask_ref[h] = jnp.where(diff >= 0, dec, 0.0)


def _tables_kernel(inv_ref, csp_ref, snp_ref, css_ref, sns_ref,
                   dmp_ref, qdp_ref, kdp_ref, gcp_ref,
                   dms_ref, qds_ref, kds_ref, gcs_ref, *, seq, dec_seq, past_len):
    inv = inv_ref[...]
    lane = lax.broadcasted_iota(jnp.int32, (1, RET_DK), 1)
    sign = jnp.where(lane < RET_DK // 2, -1.0, 1.0).astype(F32)

    def angles(n, start, step):
        pos = (lax.broadcasted_iota(jnp.int32, (n, RET_DK), 0) * step + start).astype(F32)
        return pos * inv

    ang = angles(dec_seq, past_len, 1)
    css_ref[...] = jnp.cos(ang)
    sns_ref[...] = jnp.sin(ang) * sign
    lo = angles(ROPE_ROWS, 0, 1)
    hi = angles(seq // ROPE_ROWS, 0, ROPE_ROWS)
    c_lo, s_lo, c_hi, s_hi = jnp.cos(lo), jnp.sin(lo), jnp.cos(hi), jnp.sin(hi)
    for r in range(seq // ROPE_ROWS):
        rows = slice(r * ROPE_ROWS, (r + 1) * ROPE_ROWS)
        csp_ref[rows, :] = c_hi[r:r + 1] * c_lo - s_hi[r:r + 1] * s_lo
        snp_ref[rows, :] = (s_hi[r:r + 1] * c_lo + c_hi[r:r + 1] * s_lo) * sign

    head_lane = (lax.broadcasted_iota(jnp.int32, (1, D_RET), 1) // RET_DK).astype(F32)
    lg_lane = jnp.log(1.0 - jnp.exp2(-5.0 - head_lane))
    for c, refs in ((RET_CHUNK, (dmp_ref, qdp_ref, kdp_ref, gcp_ref)),
                    (dec_seq, (dms_ref, qds_ref, kds_ref, gcs_ref))):
        lg_row = [jnp.log(1.0 - jnp.exp2(jnp.full((1, c), -5.0 - h, F32))) for h in range(RET_HEADS)]
        _decay_tables(lg_lane, lg_row, c, *refs)


def _make_tables(seq, dec_seq, past_len):
    half = RET_DK // 2
    inv = 1.0 / (ROPE_BASE ** (jnp.arange(half, dtype=F32) / half))
    inv2 = jnp.concatenate([inv, inv])[None, :]
    c, cs = RET_CHUNK, dec_seq
    out_shape = (
        jax.ShapeDtypeStruct((seq, RET_DK), F32), jax.ShapeDtypeStruct((seq, RET_DK), F32),
        jax.ShapeDtypeStruct((dec_seq, RET_DK), F32), jax.ShapeDtypeStruct((dec_seq, RET_DK), F32),
        jax.ShapeDtypeStruct((RET_HEADS, c, c), F32), jax.ShapeDtypeStruct((c, D_RET), F32),
        jax.ShapeDtypeStruct((c, D_RET), F32), jax.ShapeDtypeStruct((8, D_RET), F32),
        jax.ShapeDtypeStruct((RET_HEADS, cs, cs), F32), jax.ShapeDtypeStruct((cs, D_RET), F32),
        jax.ShapeDtypeStruct((cs, D_RET), F32), jax.ShapeDtypeStruct((8, D_RET), F32),
    )
    return pl.pallas_call(
        functools.partial(_tables_kernel, seq=seq, dec_seq=dec_seq, past_len=past_len),
        out_shape=out_shape,
        name="tables",
    )(inv2)


def _prep_kernel(cs_ref, cp_ref, wada_ref, bada_ref, win_ref, wo_ref, wpool_ref,
                 mods_ref, modp_ref, winb_ref, wob_ref, wpoolb_ref, wfold_ref, ret0_ref, hist0_ref):
    j = pl.program_id(1)
    w = wada_ref[0].astype(BF16)
    for c_ref, o_ref in ((cs_ref, mods_ref), (cp_ref, modp_ref)):
        c = c_ref[...]
        part = jnp.dot((c * jax.nn.sigmoid(c)).astype(BF16), w, preferred_element_type=F32)

        @pl.when(j == 0)
        def _():
            o_ref[0] = part + bada_ref[pl.ds(pl.program_id(0), 1), :]

        @pl.when(j > 0)
        def _():
            o_ref[0] += part
    winb_ref[0] = win_ref[0].astype(BF16)
    wob_ref[0] = wo_ref[0].astype(BF16)
    wpoolb_ref[0] = wpool_ref[0].astype(BF16)
    ret0_ref[...] = jnp.zeros_like(ret0_ref)
    hist0_ref[...] = jnp.zeros_like(hist0_ref)
    for g in range(len(POOL_WINDOWS)):
        lo, hi = g * POOL_CG, (g + 1) * POOL_CG
        wfold_ref[0, :, lo:hi] = jnp.dot(win_ref[0, :, lo:hi], wpool_ref[0, g], precision=lax.Precision.HIGHEST,
                                         preferred_element_type=F32).astype(BF16)


def _prep(c_sample, c_prompt, w_ada, b_ada, w_in, w_o, w_pool):
    depth, d, d3 = w_ada.shape
    bs, bp = c_sample.shape[0], c_prompt.shape[0]
    rows = lambda l, j: (l, j, 0)
    whole = lambda l, j: (l, 0, 0)
    return pl.pallas_call(
        _prep_kernel,
        grid=(depth, d // PREP_ROWS),
        in_specs=[
            pl.BlockSpec((bs, PREP_ROWS), lambda l, j: (0, j)),
            pl.BlockSpec((bp, PREP_ROWS), lambda l, j: (0, j)),
            pl.BlockSpec((1, PREP_ROWS, d3), rows),
            _const_spec(b_ada.shape),
            pl.BlockSpec((1, PREP_ROWS, w_in.shape[2]), rows),
            pl.BlockSpec((1, PREP_ROWS, w_o.shape[2]), rows),
            pl.BlockSpec((1,) + w_pool.shape[1:], lambda l, j: (l, 0, 0, 0)),
        ],
        out_specs=[
            pl.BlockSpec((1, bs, d3), whole),
            pl.BlockSpec((1, bp, d3), whole),
            pl.BlockSpec((1, PREP_ROWS, w_in.shape[2]), rows),
            pl.BlockSpec((1, PREP_ROWS, w_o.shape[2]), rows),
            pl.BlockSpec((1,) + w_pool.shape[1:], lambda l, j: (l, 0, 0, 0)),
            pl.BlockSpec((1, PREP_ROWS, D_POOL), rows),
            pl.BlockSpec((1, bp, RET_HEADS, RET_DK, RET_DK), lambda l, j: (l, 0, 0, 0, 0)),
            pl.BlockSpec((1, bp, HIST, D_POOL), lambda l, j: (l, 0, 0, 0)),
        ],
        out_shape=[
            jax.ShapeDtypeStruct((depth, bs, d3), F32),
            jax.ShapeDtypeStruct((depth, bp, d3), F32),
            jax.ShapeDtypeStruct(w_in.shape, BF16),
            jax.ShapeDtypeStruct(w_o.shape, BF16),
            jax.ShapeDtypeStruct(w_pool.shape, BF16),
            jax.ShapeDtypeStruct((depth, d, D_POOL), BF16),
            jax.ShapeDtypeStruct((depth, bp, RET_HEADS, RET_DK, RET_DK), F32),
            jax.ShapeDtypeStruct((depth, bp, HIST, D_POOL), F32),
        ],
        compiler_params=pltpu.CompilerParams(
            dimension_semantics=("arbitrary", "arbitrary"), vmem_limit_bytes=VMEM_LIMIT),
        name="prep",
    )(c_sample, c_prompt, w_ada, b_ada, w_in, w_o, w_pool)


def _rope(x, cs, sn):
    return x * cs + pltpu.roll(x, RET_DK // 2, x.ndim - 1) * sn


def _rms(x):
    return x * lax.rsqrt(jnp.mean(x * x, axis=-1, keepdims=True) + EPS)


def _prompt_kernel(x_ref, mod_ref, gpre_ref, gpost_ref, win_ref, wfold_ref, pscale_ref, wo_ref,
                   cs_ref, sn_ref, dmask_ref, qdec_ref, kdec_ref, gc_ref,
                   xo_ref, s_ref, nb_ref,
                   hb_ref, ubuf_ref, q_ref, qd_ref, k_ref, kd_ref, v_ref, sc_ref, sprev_ref, mix_ref, sg_ref,
                   *, tile, layer):
    t = pl.program_id(1)
    n_chunks = tile // RET_CHUNK
    blocks = [(r0, r0 + PROMPT_BLOCK) for r0 in range(0, tile, PROMPT_BLOCK)]
    o1, o2, o3, o4 = D_POOL, D_POOL + D_RET, D_POOL + 2 * D_RET, D_POOL + 3 * D_RET
    heads = [slice(hh * RET_DK, (hh + 1) * RET_DK) for hh in range(RET_HEADS)]
    chunks = [slice(c * RET_CHUNK, (c + 1) * RET_CHUNK) for c in range(n_chunks)]

    @pl.when(t == 0)
    def _():
        s_ref[...] = jnp.zeros_like(s_ref)
        ubuf_ref[0:HIST, :] = jnp.zeros((HIST, D_POOL), F32)

    m = mod_ref[pl.ds(pl.program_id(0), 1), :]
    shift, scl, gres = m[:, 0:D_MODEL], m[:, D_MODEL:2 * D_MODEL], m[:, 2 * D_MODEL:]
    pre_gain = gpre_ref[layer:layer + 1, :] * (1.0 + scl)
    post_gain = gres * gpost_ref[layer:layer + 1, :]

    def proj(r0, r1, c0, c1):
        return jnp.dot(hb_ref[r0:r1, :], win_ref[:, c0:c1], preferred_element_type=F32)

    zqk, kvs = {}, {}

    def pre_norm(j):
        r0, r1 = blocks[j]
        hb_ref[r0:r1, :] = (_rms(x_ref[0, r0:r1, :]) * pre_gain + shift).astype(BF16)

    def project(j):
        r0, r1 = blocks[j]
        ubuf_ref[HIST + r0:HIST + r1, :] = jnp.dot(hb_ref[r0:r1, :], wfold_ref[...], preferred_element_type=F32)
        zqk[j] = (proj(r0, r1, o1, o2), proj(r0, r1, o2, o3))
        v_ref[r0:r1, :] = proj(r0, r1, o3, o4).astype(BF16)

    def mixer_inputs(j):
        r0, r1 = blocks[j]
        row = lax.broadcasted_iota(jnp.int32, (r1 - r0, 1), 0) + (t * tile + r0)
        for g, w in enumerate(POOL_WINDOWS):
            lo, hi = g * POOL_CG, (g + 1) * POOL_CG
            u_ext = ubuf_ref[r0:r1 + HIST, lo:hi]
            acc = u_ext
            k = 1
            while k < w:
                acc = acc + pltpu.roll(acc, k, 0)
                k *= 2
            cnt = jnp.minimum(row + 1, w).astype(F32)
            mix_ref[r0:r1, lo:hi] = (acc[HIST:] / cnt - u_ext[HIST:]) * pscale_ref[layer:layer + 1, lo:hi]
        zq, zk = zqk.pop(j)
        cs, sn = cs_ref[r0:r1, :], sn_ref[r0:r1, :]
        for sl in heads:
            qr = _rope(zq[:, sl], cs, sn)
            kr = _rope(zk[:, sl], cs, sn) * K_SCALE
            q_ref[r0:r1, sl] = qr.astype(BF16)
            k_ref[r0:r1, sl] = kr.astype(BF16)
            for c in range(r0 // RET_CHUNK, r1 // RET_CHUNK):
                rs, ls = chunks[c], slice(c * RET_CHUNK - r0, (c + 1) * RET_CHUNK - r0)
                qd_ref[rs, sl] = (qr[ls] * qdec_ref[:, sl]).astype(BF16)
                kd_ref[rs, sl] = (kr[ls] * kdec_ref[:, sl]).astype(BF16)

    def small_matmuls(j):
        r0, r1 = blocks[j]
        for c in range(r0 // RET_CHUNK, r1 // RET_CHUNK):
            rs = chunks[c]
            for hh, sl in enumerate(heads):
                scores = lax.dot_general(q_ref[rs, sl], k_ref[rs, sl], (((1,), (1,)), ((), ())),
                                         preferred_element_type=F32)
                sc_ref[rs, sl] = (scores * dmask_ref[hh]).astype(BF16)
                kvs[c, hh] = lax.dot_general(kd_ref[rs, sl], v_ref[rs, sl], (((0,), (0,)), ((), ())),
                                             preferred_element_type=F32)

    def gate(j):
        r0, r1 = blocks[j]
        zg = proj(r0, r1, o4, D_IN)
        sg_ref[r0:r1, :] = zg * jax.nn.sigmoid(zg)

    def state_recurrence():
        for hh, sl in enumerate(heads):
            s = s_ref[0, hh]
            for c in range(n_chunks):
                sprev_ref[c * RET_HEADS + hh] = s.astype(BF16)
                s = gc_ref[0:1, sl] * s + kvs.pop((c, hh))
            s_ref[0, hh] = s

    def retention_out(j):
        r0, r1 = blocks[j]
        for c in range(r0 // RET_CHUNK, r1 // RET_CHUNK):
            rs = chunks[c]
            for hh, sl in enumerate(heads):
                lhs = jnp.concatenate([sc_ref[rs, sl], qd_ref[rs, sl]], axis=1)
                rhs = jnp.concatenate([v_ref[rs, sl], sprev_ref[c * RET_HEADS + hh]], axis=0)
                o_h = jnp.dot(lhs, rhs, preferred_element_type=F32)
                mix_ref[rs, D_POOL + hh * RET_DK:D_POOL + (hh + 1) * RET_DK] = _rms(o_h)

    def out_proj(j):
        r0, r1 = blocks[j]
        mixb = (mix_ref[r0:r1, :] * sg_ref[r0:r1, :]).astype(BF16)
        y = jnp.dot(mixb, wo_ref[...], preferred_element_type=F32)
        xo_ref[0, r0:r1, :] = x_ref[0, r0:r1, :] + _rms(y) * post_gain

    n = len(blocks)
    pre_norm(0)
    project(0)
    for j in range(n):
        if j + 1 < n:
            pre_norm(j + 1)
        mixer_inputs(j)
        if j + 1 < n:
            project(j + 1)
        else:
            ubuf_ref[0:HIST, :] = ubuf_ref[tile:tile + HIST, :]
            gate(0)
        small_matmuls(j)
    gate(1)
    state_recurrence()

    gates_left = list(range(2, n))
    for j in range(n):
        retention_out(j)
        if gates_left:
            gate(gates_left.pop(0))
        if j >= 1:
            out_proj(j - 1)
    out_proj(n - 1)

    @pl.when(t == pl.num_programs(1) - 1)
    def _():
        nb_ref[0] = jnp.dot(hb_ref[tile - HIST:tile, :], win_ref[:, 0:o1], preferred_element_type=F32)


def _prompt_layer(l, x, mod, g_pre, g_post, w_in, w_fold, pool_scale, w_o, tabs, stacked):
    depth = w_in.shape[0]
    b, seq, d = x.shape
    tile = PROMPT_TILE
    cs, sn, dmask, qdec, kdec, gc = tabs
    layer = lambda *_: (l, 0, 0)
    n_in = 14
    carried = tuple(stacked)

    def kern(*refs):
        _prompt_kernel(*refs[:n_in], *refs[n_in + len(carried):], tile=tile, layer=l)

    return pl.pallas_call(
        kern,
        grid=(b, seq // tile),
        in_specs=[
            pl.BlockSpec((1, tile, d), lambda i, t: (i, t, 0)),
            pl.BlockSpec((None, b, 3 * d), layer),
            _const_spec(g_pre.shape),
            _const_spec(g_post.shape),
            pl.BlockSpec((None, d, D_IN), layer, pipeline_mode=pl.Buffered(1)),
            pl.BlockSpec((None, d, D_POOL), layer, pipeline_mode=pl.Buffered(1)),
            _const_spec(pool_scale.shape),
            pl.BlockSpec((None, d, d), layer, pipeline_mode=pl.Buffered(1)),
            pl.BlockSpec((tile, RET_DK), lambda i, t: (t, 0)),
            pl.BlockSpec((tile, RET_DK), lambda i, t: (t, 0)),
            _const_spec(dmask.shape), _const_spec(qdec.shape), _const_spec(kdec.shape), _const_spec(gc.shape),
        ] + [pl.BlockSpec(memory_space=pl.ANY)] * len(carried),
        out_specs=[
            pl.BlockSpec((1, tile, d), lambda i, t: (i, t, 0)),
            pl.BlockSpec((None, 1, RET_HEADS, RET_DK, RET_DK), lambda i, t: (l, i, 0, 0, 0)),
            pl.BlockSpec((None, 1, HIST, D_POOL), lambda i, t: (l, i, 0, 0)),
        ],
        out_shape=[
            jax.ShapeDtypeStruct((b, seq, d), F32),
            jax.ShapeDtypeStruct((depth, b, RET_HEADS, RET_DK, RET_DK), F32),
            jax.ShapeDtypeStruct((depth, b, HIST, D_POOL), F32),
        ],
        input_output_aliases={n_in + i: 1 + i for i in range(len(carried))},
        scratch_shapes=[
            pltpu.VMEM((tile, d), BF16),
            pltpu.VMEM((HIST + tile, D_POOL), F32),
            pltpu.VMEM((tile, D_RET), BF16),
            pltpu.VMEM((tile, D_RET), BF16),
            pltpu.VMEM((tile, D_RET), BF16),
            pltpu.VMEM((tile, D_RET), BF16),
            pltpu.VMEM((tile, D_RET), BF16),
            pltpu.VMEM((tile, D_RET), BF16),
            pltpu.VMEM((tile // RET_CHUNK * RET_HEADS, RET_DK, RET_DK), BF16),
            pltpu.VMEM((tile, d), F32),
            pltpu.VMEM((tile, d), F32),
        ],
        compiler_params=pltpu.CompilerParams(
            dimension_semantics=("arbitrary", "arbitrary"), vmem_limit_bytes=VMEM_LIMIT),
        name=f"prompt_layer{l}",
    )(x, mod, g_pre, g_post, w_in, w_fold, pool_scale, w_o, cs, sn, dmask, qdec, kdec, gc, *carried)


def _sample_kernel(x_ref, mod_ref, gpre_ref, gpost_ref, win_ref, wpool_ref, pscale_ref, wo_ref,
                   cs_ref, sn_ref, dmask_ref, qdec_ref, kdec_ref, gc_ref, s0_ref, hist_ref,
                   xo_ref, s_ref, nb_ref,
                   usc_ref, mix_ref, sg_ref, *, bt, dec_seq):
    n = bt * dec_seq
    pool_buf = hist_ref.shape[0]
    rows = pl.ds(pl.multiple_of(pl.program_id(1) * bt, bt), bt)

    @pl.when(pl.program_id(0) == 0)
    def _():
        xo_ref[rows] = x_ref[...]

    m = mod_ref[...][:, None, :]
    shift, scl, gres = m[:, :, 0:D_MODEL], m[:, :, D_MODEL:2 * D_MODEL], m[:, :, 2 * D_MODEL:]

    x3 = xo_ref[rows]
    layer = pl.ds(pl.program_id(0), 1)
    h3 = _rms(x3) * (gpre_ref[layer, :][None] * (1.0 + scl)) + shift
    hb = h3.reshape(n, D_MODEL).astype(BF16)

    o1, o2, o3, o4 = D_POOL, D_POOL + D_RET, D_POOL + 2 * D_RET, D_POOL + 3 * D_RET
    heads = [slice(hh * RET_DK, (hh + 1) * RET_DK) for hh in range(RET_HEADS)]
    u = jnp.dot(hb, win_ref[:, 0:o1], preferred_element_type=F32)
    zq = jnp.dot(hb, win_ref[:, o1:o2], preferred_element_type=F32).reshape(bt, dec_seq, D_RET)
    zk = jnp.dot(hb, win_ref[:, o2:o3], preferred_element_type=F32).reshape(bt, dec_seq, D_RET)
    zv = jnp.dot(hb, win_ref[:, o3:o4], preferred_element_type=F32).reshape(bt, dec_seq, D_RET)
    zg = jnp.dot(hb, win_ref[:, o4:], preferred_element_type=F32)

    pooled = []
    for g, w in enumerate(POOL_WINDOWS):
        lo, hi = g * POOL_CG, (g + 1) * POOL_CG
        usc_ref[g] = u[:, lo:hi]
        rows_g = [hist_ref[r, :, lo:hi] for r in range(pool_buf)]
        rows_g += [usc_ref[g, pl.ds(i, bt, stride=dec_seq), :] for i in range(dec_seq)]
        for i in range(dec_seq):
            cur = pool_buf + i
            acc = rows_g[cur]
            for k in range(1, w):
                acc = acc + rows_g[cur - k]
            usc_ref[g, pl.ds(i, bt, stride=dec_seq), :] = acc / float(w) - rows_g[cur]
        pooled.append(usc_ref[g].astype(BF16))
        for r in range(pool_buf):
            nb_ref[r, :, lo:hi] = rows_g[dec_seq + r]

    cs, sn = cs_ref[...][None], sn_ref[...][None]
    qr = [_rope(zq[:, :, sl], cs, sn) for sl in heads]
    kr = [_rope(zk[:, :, sl], cs, sn) * K_SCALE for sl in heads]
    sg_ref[...] = zg * jax.nn.sigmoid(zg)

    for g in range(len(POOL_WINDOWS)):
        lo, hi = g * POOL_CG, (g + 1) * POOL_CG
        mix_ref[:, lo:hi] = jnp.dot(pooled[g], wpool_ref[g], preferred_element_type=F32) * pscale_ref[layer, :][:, lo:hi]
    scores = [jnp.einsum("bid,bjd->bij", qr[hh], kr[hh], preferred_element_type=F32) * dmask_ref[hh][None]
              for hh in range(RET_HEADS)]
    kv, cross = [], []
    for hh, sl in enumerate(heads):
        kv.append(jnp.einsum("bjd,bje->bde", (kr[hh] * kdec_ref[:, sl][None]).astype(BF16),
                             zv[:, :, sl].astype(BF16), preferred_element_type=F32))
        cross.append(jnp.einsum("bid,bde->bie", qr[hh] * qdec_ref[:, sl][None], s0_ref[:, hh],
                                preferred_element_type=F32))
    intra = [jnp.einsum("bij,bje->bie", scores[hh], zv[:, :, sl], preferred_element_type=F32)
             for hh, sl in enumerate(heads)]
    for hh, sl in enumerate(heads):
        s_ref[:, hh] = gc_ref[0:1, sl][None] * s0_ref[:, hh] + kv[hh]
        mix_ref[:, D_POOL + hh * RET_DK:D_POOL + (hh + 1) * RET_DK] = _rms(intra[hh] + cross[hh]).reshape(n, RET_DK)

    mixb = (mix_ref[...] * sg_ref[...]).astype(BF16)
    y = jnp.dot(mixb, wo_ref[...], preferred_element_type=F32)
    xo_ref[rows] = x3 + (gres * gpost_ref[layer, :][None]) * _rms(y).reshape(bt, dec_seq, D_MODEL)


def _sample_layers(x, mod, g_pre, g_post, w_in, w_pool, pool_scale, w_o, tabs, state_ret, state_pool):
    b, dec_seq, d = x.shape
    depth, pool_buf, _, _ = state_pool.shape
    bt = SAMPLE_BT
    cs, sn, dmask, qdec, kdec, gc = tabs
    layer = lambda l, i: (l, 0, 0)
    kern = functools.partial(_sample_kernel, bt=bt, dec_seq=dec_seq)
    return pl.pallas_call(
        kern,
        grid=(depth, b // bt),
        in_specs=[
            pl.BlockSpec((bt, dec_seq, d), lambda l, i: (i, 0, 0)),
            pl.BlockSpec((None, bt, 3 * d), lambda l, i: (l, i, 0)),
            _const_spec(g_pre.shape),
            _const_spec(g_post.shape),
            pl.BlockSpec((None, d, D_IN), layer, pipeline_mode=pl.Buffered(1)),
            pl.BlockSpec((None, len(POOL_WINDOWS), POOL_CG, POOL_CG), lambda l, i: (l, 0, 0, 0)),
            _const_spec(pool_scale.shape),
            pl.BlockSpec((None, d, d), layer, pipeline_mode=pl.Buffered(1)),
            _const_spec(cs.shape), _const_spec(sn.shape),
            _const_spec(dmask.shape), _const_spec(qdec.shape), _const_spec(kdec.shape), _const_spec(gc.shape),
            pl.BlockSpec((None, bt, RET_HEADS, RET_DK, RET_DK), lambda l, i: (l, i, 0, 0, 0)),
            pl.BlockSpec((None, pool_buf, bt, D_POOL), lambda l, i: (l, 0, i, 0)),
        ],
        out_specs=[
            pl.BlockSpec((b, dec_seq, d), lambda l, i: (0, 0, 0)),
            pl.BlockSpec((None, bt, RET_HEADS, RET_DK, RET_DK), lambda l, i: (l, i, 0, 0, 0)),
            pl.BlockSpec((None, pool_buf, bt, D_POOL), lambda l, i: (l, 0, i, 0)),
        ],
        out_shape=[
            jax.ShapeDtypeStruct((b, dec_seq, d), F32),
            jax.ShapeDtypeStruct((depth, b, RET_HEADS, RET_DK, RET_DK), F32),
            jax.ShapeDtypeStruct((depth, pool_buf, b, D_POOL), F32),
        ],
        scratch_shapes=[
            pltpu.VMEM((len(POOL_WINDOWS), bt * dec_seq, POOL_CG), F32),
            pltpu.VMEM((bt * dec_seq, d), F32),
            pltpu.VMEM((bt * dec_seq, d), F32),
        ],
        compiler_params=pltpu.CompilerParams(
            dimension_semantics=("arbitrary", "arbitrary"), vmem_limit_bytes=VMEM_LIMIT),
        name="sample_layers",
    )(x, mod, g_pre, g_post, w_in, w_pool, pool_scale, w_o, cs, sn, dmask, qdec, kdec, gc, state_ret, state_pool)


def kernel(x_prompt, x_sample, c_prompt, c_sample, state_ret, state_pool, w_ada, b_ada,
           g_pre, g_post, w_in, w_pool, pool_scale, w_o):
    depth = w_in.shape[0]
    bp, seq, d = x_prompt.shape
    bs, dec_seq, _ = x_sample.shape
    assert d == D_MODEL and w_in.shape[2] == D_IN and state_pool.shape[2] == HIST - 1
    assert seq % PROMPT_TILE == 0 and bs % SAMPLE_BT == 0
    assert dec_seq == 8 and dec_seq % RET_CHUNK != 0

    (csp, snp, css, sns, dmp, qdp, kdp, gcp, dms, qds, kds, gcs) = _make_tables(seq, dec_seq, PAST_LEN)
    tabs_p = (csp, snp, dmp, qdp, kdp, gcp)
    tabs_s = (css, sns, dms, qds, kds, gcs)

    mod_s, mod_p, w_in_b, w_o_b, w_pool_b, w_fold_b, *stacked = _prep(c_sample, c_prompt, w_ada, b_ada,
                                                                      w_in, w_o, w_pool)
    hs, ret_s, pool_s = _sample_layers(x_sample, mod_s, g_pre, g_post, w_in_b, w_pool_b, pool_scale, w_o_b,
                                       tabs_s, state_ret, jnp.transpose(state_pool, (0, 2, 1, 3)))
    pool_s = jnp.transpose(pool_s, (0, 2, 1, 3))
    hp = x_prompt
    for l in range(depth):
        hp, *stacked = _prompt_layer(l, hp, mod_p, g_pre, g_post, w_in_b, w_fold_b, pool_scale, w_o_b, tabs_p,
                                     stacked)
    ret_p, hist_p = stacked
    return (hp, hs, ret_p, hist_p[:, :, HIST - state_pool.shape[2]:], ret_s, pool_s)
```

```python
import functools

import jax
import jax.numpy as jnp
from jax import lax
from jax.experimental import pallas as pl
from jax.experimental.pallas import tpu as pltpu

F32 = jnp.float32
BF16 = jnp.bfloat16

D_MODEL = 1024
D_POOL = 512
D_RET = 512
POOL_WINDOWS = (2, 4, 8, 16)
POOL_CG = 128
HIST = 16
RET_HEADS = 4
RET_DK = 128
RET_CHUNK = 128
ROPE_BASE = 10000.0
PAST_LEN = 16384
ROPE_ROWS = 128
D_IN = D_POOL + 3 * D_RET + D_MODEL
EPS = 1e-6
K_SCALE = RET_DK ** -0.5

PROMPT_TILE = 1024
PROMPT_BLOCK = 256
SAMPLE_BT = 32
PREP_ROWS = 256
VMEM_LIMIT = 60 * 1024 * 1024


def _const_spec(shape):
    return pl.BlockSpec(shape, lambda *_: (0,) * len(shape))


def _decay_tables(lg_lane, lg_row, c, dmask_ref, qdec_ref, kdec_ref, gc_ref):
    idx = lax.broadcasted_iota(jnp.int32, (c, D_RET), 0).astype(F32)
    qdec_ref[...] = jnp.exp((idx + 1.0) * lg_lane)
    kdec_ref[...] = jnp.exp((c - 1.0 - idx) * lg_lane)
    gc_ref[...] = jnp.exp(jnp.full((8, D_RET), float(c), F32) * lg_lane)
    ii = lax.broadcasted_iota(jnp.int32, (c, c), 0)
    jj = lax.broadcasted_iota(jnp.int32, (c, c), 1)
    diff = ii - jj
    for h in range(RET_HEADS):
        dec = jnp.exp(jnp.maximum(diff, 0).astype(F32) * lg_row[h])
        dmask_ref[h] = jnp.where(diff >= 0, dec, 0.0)


def _tables_kernel(inv_ref, csp_ref, snp_ref, css_ref, sns_ref,
                   dmp_ref, qdp_ref, kdp_ref, gcp_ref,
                   dms_ref, qds_ref, kds_ref, gcs_ref, *, seq, dec_seq, past_len):
    inv = inv_ref[...]
    lane = lax.broadcasted_iota(jnp.int32, (1, RET_DK), 1)
    sign = jnp.where(lane < RET_DK // 2, -1.0, 1.0).astype(F32)

    def angles(n, start, step):
        pos = (lax.broadcasted_iota(jnp.int32, (n, RET_DK), 0) * step + start).astype(F32)
        return pos * inv

    ang = angles(dec_seq, past_len, 1)
    css_ref[...] = jnp.cos(ang)
    sns_ref[...] = jnp.sin(ang) * sign
    lo = angles(ROPE_ROWS, 0, 1)
    hi = angles(seq // ROPE_ROWS, 0, ROPE_ROWS)
    c_lo, s_lo, c_hi, s_hi = jnp.cos(lo), jnp.sin(lo), jnp.cos(hi), jnp.sin(hi)
    for r in range(seq // ROPE_ROWS):
        rows = slice(r * ROPE_ROWS, (r + 1) * ROPE_ROWS)
        csp_ref[rows, :] = c_hi[r:r + 1] * c_lo - s_hi[r:r + 1] * s_lo
        snp_ref[rows, :] = (s_hi[r:r + 1] * c_lo + c_hi[r:r + 1] * s_lo) * sign

    head_lane = (lax.broadcasted_iota(jnp.int32, (1, D_RET), 1) // RET_DK).astype(F32)
    lg_lane = jnp.log(1.0 - jnp.exp2(-5.0 - head_lane))
    for c, refs in ((RET_CHUNK, (dmp_ref, qdp_ref, kdp_ref, gcp_ref)),
                    (dec_seq, (dms_ref, qds_ref, kds_ref, gcs_ref))):
        lg_row = [jnp.log(1.0 - jnp.exp2(jnp.full((1, c), -5.0 - h, F32))) for h in range(RET_HEADS)]
        _decay_tables(lg_lane, lg_row, c, *refs)


def _make_tables(seq, dec_seq, past_len):
    half = RET_DK // 2
    inv = 1.0 / (ROPE_BASE ** (jnp.arange(half, dtype=F32) / half))
    inv2 = jnp.concatenate([inv, inv])[None, :]
    c, cs = RET_CHUNK, dec_seq
    out_shape = (
        jax.ShapeDtypeStruct((seq, RET_DK), F32), jax.ShapeDtypeStruct((seq, RET_DK), F32),
        jax.ShapeDtypeStruct((dec_seq, RET_DK), F32), jax.ShapeDtypeStruct((dec_seq, RET_DK), F32),
        jax.ShapeDtypeStruct((RET_HEADS, c, c), F32), jax.ShapeDtypeStruct((c, D_RET), F32),
        jax.ShapeDtypeStruct((c, D_RET), F32), jax.ShapeDtypeStruct((8, D_RET), F32),
        jax.ShapeDtypeStruct((RET_HEADS, cs, cs), F32), jax.ShapeDtypeStruct((cs, D_RET), F32),
        jax.ShapeDtypeStruct((cs, D_RET), F32), jax.ShapeDtypeStruct((8, D_RET), F32),
    )
    return pl.pallas_call(
        functools.partial(_tables_kernel, seq=seq, dec_seq=dec_seq, past_len=past_len),
        out_shape=out_shape,
        name="tables",
    )(inv2)


def _prep_kernel(cs_ref, cp_ref, wada_ref, bada_ref, win_ref, wo_ref, wpool_ref,
                 mods_ref, modp_ref, winb_ref, wob_ref, wpoolb_ref, wfold_ref, ret0_ref, hist0_ref):
    j = pl.program_id(1)
    w = wada_ref[0].astype(BF16)
    for c_ref, o_ref in ((cs_ref, mods_ref), (cp_ref, modp_ref)):
        c = c_ref[...]
        part = jnp.dot((c * jax.nn.sigmoid(c)).astype(BF16), w, preferred_element_type=F32)

        @pl.when(j == 0)
        def _():
            o_ref[0] = part + bada_ref[pl.ds(pl.program_id(0), 1), :]

        @pl.when(j > 0)
        def _():
            o_ref[0] += part
    winb_ref[0] = win_ref[0].astype(BF16)
    wob_ref[0] = wo_ref[0].astype(BF16)
    wpoolb_ref[0] = wpool_ref[0].astype(BF16)

    @pl.when(j == 0)
    def _():
        ret0_ref[...] = jnp.zeros_like(ret0_ref)
        hist0_ref[...] = jnp.zeros_like(hist0_ref)

    for g in range(len(POOL_WINDOWS)):
        lo, hi = g * POOL_CG, (g + 1) * POOL_CG
        wfold_ref[0, :, lo:hi] = jnp.dot(win_ref[0, :, lo:hi], wpool_ref[0, g], precision=lax.Precision.HIGHEST,
                                         preferred_element_type=F32).astype(BF16)


def _prep(c_sample, c_prompt, w_ada, b_ada, w_in, w_o, w_pool):
    depth, d, d3 = w_ada.shape
    bs, bp = c_sample.shape[0], c_prompt.shape[0]
    rows = lambda l, j: (l, j, 0)
    whole = lambda l, j: (l, 0, 0)
    return pl.pallas_call(
        _prep_kernel,
        grid=(depth, d // PREP_ROWS),
        in_specs=[
            pl.BlockSpec((bs, PREP_ROWS), lambda l, j: (0, j)),
            pl.BlockSpec((bp, PREP_ROWS), lambda l, j: (0, j)),
            pl.BlockSpec((1, PREP_ROWS, d3), rows),
            _const_spec(b_ada.shape),
            pl.BlockSpec((1, PREP_ROWS, w_in.shape[2]), rows),
            pl.BlockSpec((1, PREP_ROWS, w_o.shape[2]), rows),
            pl.BlockSpec((1,) + w_pool.shape[1:], lambda l, j: (l, 0, 0, 0)),
        ],
        out_specs=[
            pl.BlockSpec((1, bs, d3), whole),
            pl.BlockSpec((1, bp, d3), whole),
            pl.BlockSpec((1, PREP_ROWS, w_in.shape[2]), rows),
            pl.BlockSpec((1, PREP_ROWS, w_o.shape[2]), rows),
            pl.BlockSpec((1,) + w_pool.shape[1:], lambda l, j: (l, 0, 0, 0)),
            pl.BlockSpec((1, PREP_ROWS, D_POOL), rows),
            pl.BlockSpec((1, bp, RET_HEADS, RET_DK, RET_DK), lambda l, j: (l, 0, 0, 0, 0)),
            pl.BlockSpec((1, bp, HIST, D_POOL), lambda l, j: (l, 0, 0, 0)),
        ],
        out_shape=[
            jax.ShapeDtypeStruct((depth, bs, d3), F32),
            jax.ShapeDtypeStruct((depth, bp, d3), F32),
            jax.ShapeDtypeStruct(w_in.shape, BF16),
            jax.ShapeDtypeStruct(w_o.shape, BF16),
            jax.ShapeDtypeStruct(w_pool.shape, BF16),
            jax.ShapeDtypeStruct((depth, d, D_POOL), BF16),
            jax.ShapeDtypeStruct((depth, bp, RET_HEADS, RET_DK, RET_DK), F32),
            jax.ShapeDtypeStruct((depth, bp, HIST, D_POOL), F32),
        ],
        compiler_params=pltpu.CompilerParams(
            dimension_semantics=("arbitrary", "arbitrary"), vmem_limit_bytes=VMEM_LIMIT),
        name="prep",
    )(c_sample, c_prompt, w_ada, b_ada, w_in, w_o, w_pool)


def _rope(x, cs, sn):
    return x * cs + pltpu.roll(x, RET_DK // 2, x.ndim - 1) * sn


def _rms(x):
    return x * lax.rsqrt(jnp.mean(x * x, axis=-1, keepdims=True) + EPS)


def _prompt_kernel(x_ref, mod_ref, gpre_ref, gpost_ref, win_ref, wfold_ref, pscale_ref, wo_ref,
                   cs_ref, sn_ref, dmask_ref, qdec_ref, kdec_ref, gc_ref,
                   xo_ref, s_ref, nb_ref,
                   hb_ref, ubuf_ref, q_ref, qd_ref, k_ref, kd_ref, v_ref, sc_ref, sprev_ref, mix_ref, sg_ref,
                   *, tile, layer):
    t = pl.program_id(1)
    n_chunks = tile // RET_CHUNK
    blocks = [(r0, r0 + PROMPT_BLOCK) for r0 in range(0, tile, PROMPT_BLOCK)]
    o1, o2, o3, o4 = D_POOL, D_POOL + D_RET, D_POOL + 2 * D_RET, D_POOL + 3 * D_RET
    heads = [slice(hh * RET_DK, (hh + 1) * RET_DK) for hh in range(RET_HEADS)]
    chunks = [slice(c * RET_CHUNK, (c + 1) * RET_CHUNK) for c in range(n_chunks)]

    @pl.when(t == 0)
    def _():
        s_ref[...] = jnp.zeros_like(s_ref)
        ubuf_ref[0:HIST, :] = jnp.zeros((HIST, D_POOL), F32)

    m = mod_ref[pl.ds(pl.program_id(0), 1), :]
    shift, scl, gres = m[:, 0:D_MODEL], m[:, D_MODEL:2 * D_MODEL], m[:, 2 * D_MODEL:]
    pre_gain = gpre_ref[layer:layer + 1, :] * (1.0 + scl)
    post_gain = gres * gpost_ref[layer:layer + 1, :]

    def proj(r0, r1, c0, c1):
        return jnp.dot(hb_ref[r0:r1, :], win_ref[:, c0:c1], preferred_element_type=F32)

    zqk, kvs = {}, {}

    def pre_norm(j):
        r0, r1 = blocks[j]
        hb_ref[r0:r1, :] = (_rms(x_ref[0, r0:r1, :]) * pre_gain + shift).astype(BF16)

    def project(j):
        r0, r1 = blocks[j]
        ubuf_ref[HIST + r0:HIST + r1, :] = jnp.dot(hb_ref[r0:r1, :], wfold_ref[...], preferred_element_type=F32)
        zqk[j] = (proj(r0, r1, o1, o2), proj(r0, r1, o2, o3))
        v_ref[r0:r1, :] = proj(r0, r1, o3, o4).astype(BF16)

    def mixer_inputs(j):
        r0, r1 = blocks[j]
        row = lax.broadcasted_iota(jnp.int32, (r1 - r0, 1), 0) + (t * tile + r0)
        for g, w in enumerate(POOL_WINDOWS):
            lo, hi = g * POOL_CG, (g + 1) * POOL_CG
            u_ext = ubuf_ref[r0:r1 + HIST, lo:hi]
            acc = u_ext
            k = 1
            while k < w:
                acc = acc + pltpu.roll(acc, k, 0)
                k *= 2
            cnt = jnp.minimum(row + 1, w).astype(F32)
            mix_ref[r0:r1, lo:hi] = (acc[HIST:] / cnt - u_ext[HIST:]) * pscale_ref[layer:layer + 1, lo:hi]
        zq, zk = zqk.pop(j)
        cs, sn = cs_ref[r0:r1, :], sn_ref[r0:r1, :]
        for sl in heads:
            qr = _rope(zq[:, sl], cs, sn)
            kr = _rope(zk[:, sl], cs, sn) * K_SCALE
            q_ref[r0:r1, sl] = qr.astype(BF16)
            k_ref[r0:r1, sl] = kr.astype(BF16)
            for c in range(r0 // RET_CHUNK, r1 // RET_CHUNK):
                rs, ls = chunks[c], slice(c * RET_CHUNK - r0, (c + 1) * RET_CHUNK - r0)
                qd_ref[rs, sl] = (qr[ls] * qdec_ref[:, sl]).astype(BF16)
                kd_ref[rs, sl] = (kr[ls] * kdec_ref[:, sl]).astype(BF16)

    def small_matmuls(j):
        r0, r1 = blocks[j]
        for c in range(r0 // RET_CHUNK, r1 // RET_CHUNK):
            rs = chunks[c]
            for hh, sl in enumerate(heads):
                scores = lax.dot_general(q_ref[rs, sl], k_ref[rs, sl], (((1,), (1,)), ((), ())),
                                         preferred_element_type=F32)
                sc_ref[rs, sl] = (scores * dmask_ref[hh]).astype(BF16)
                kvs[c, hh] = lax.dot_general(kd_ref[rs, sl], v_ref[rs, sl], (((0,), (0,)), ((), ())),
                                             preferred_element_type=F32)

    def gate(j):
        r0, r1 = blocks[j]
        zg = proj(r0, r1, o4, D_IN)
        sg_ref[r0:r1, :] = zg * jax.nn.sigmoid(zg)

    def state_recurrence():
        for hh, sl in enumerate(heads):
            s = s_ref[0, hh]
            for c in range(n_chunks):
                sprev_ref[c * RET_HEADS + hh] = s.astype(BF16)
                s = gc_ref[0:1, sl] * s + kvs.pop((c, hh))
            s_ref[0, hh] = s

    def retention_out(j):
        r0, r1 = blocks[j]
        for c in range(r0 // RET_CHUNK, r1 // RET_CHUNK):
            rs = chunks[c]
            for hh, sl in enumerate(heads):
                lhs = jnp.concatenate([sc_ref[rs, sl], qd_ref[rs, sl]], axis=1)
                rhs = jnp.concatenate([v_ref[rs, sl], sprev_ref[c * RET_HEADS + hh]], axis=0)
                o_h = jnp.dot(lhs, rhs, preferred_element_type=F32)
                mix_ref[rs, D_POOL + hh * RET_DK:D_POOL + (hh + 1) * RET_DK] = _rms(o_h)

    def out_proj(j):
        r0, r1 = blocks[j]
        mixb = (mix_ref[r0:r1, :] * sg_ref[r0:r1, :]).astype(BF16)
        y = jnp.dot(mixb, wo_ref[...], preferred_element_type=F32)
        xo_ref[0, r0:r1, :] = x_ref[0, r0:r1, :] + _rms(y) * post_gain

    n = len(blocks)
    pre_norm(0)
    project(0)
    for j in range(n):
        if j + 1 < n:
            pre_norm(j + 1)
        mixer_inputs(j)
        if j + 1 < n:
            project(j + 1)
        else:
            ubuf_ref[0:HIST, :] = ubuf_ref[tile:tile + HIST, :]
            gate(0)
        small_matmuls(j)
    gate(1)
    state_recurrence()

    gates_left = list(range(2, n))
    for j in range(n):
        retention_out(j)
        if gates_left:
            gate(gates_left.pop(0))
        if j >= 1:
            out_proj(j - 1)
    out_proj(n - 1)

    @pl.when(t == pl.num_programs(1) - 1)
    def _():
        nb_ref[0] = jnp.dot(hb_ref[tile - HIST:tile, :], win_ref[:, 0:o1], preferred_element_type=F32)


def _prompt_layer(l, x, mod, g_pre, g_post, w_in, w_fold, pool_scale, w_o, tabs, stacked):
    depth = w_in.shape[0]
    b, seq, d = x.shape
    tile = PROMPT_TILE
    cs, sn, dmask, qdec, kdec, gc = tabs
    layer = lambda *_: (l, 0, 0)
    n_in = 14
    carried = tuple(stacked)

    def kern(*refs):
        _prompt_kernel(*refs[:n_in], *refs[n_in + len(carried):], tile=tile, layer=l)

    return pl.pallas_call(
        kern,
        grid=(b, seq // tile),
        in_specs=[
            pl.BlockSpec((1, tile, d), lambda i, t: (i, t, 0)),
            pl.BlockSpec((None, b, 3 * d), layer),
            _const_spec(g_pre.shape),
            _const_spec(g_post.shape),
            pl.BlockSpec((None, d, D_IN), layer, pipeline_mode=pl.Buffered(1)),
            pl.BlockSpec((None, d, D_POOL), layer, pipeline_mode=pl.Buffered(1)),
            _const_spec(pool_scale.shape),
            pl.BlockSpec((None, d, d), layer, pipeline_mode=pl.Buffered(1)),
            pl.BlockSpec((tile, RET_DK), lambda i, t: (t, 0)),
            pl.BlockSpec((tile, RET_DK), lambda i, t: (t, 0)),
            _const_spec(dmask.shape), _const_spec(qdec.shape), _const_spec(kdec.shape), _const_spec(gc.shape),
        ] + [pl.BlockSpec(memory_space=pl.ANY)] * len(carried),
        out_specs=[
            pl.BlockSpec((1, tile, d), lambda i, t: (i, t, 0)),
            pl.BlockSpec((None, 1, RET_HEADS, RET_DK, RET_DK), lambda i, t: (l, i, 0, 0, 0)),
            pl.BlockSpec((None, 1, HIST, D_POOL), lambda i, t: (l, i, 0, 0)),
        ],
        out_shape=[
            jax.ShapeDtypeStruct((b, seq, d), F32),
            jax.ShapeDtypeStruct((depth, b, RET_HEADS, RET_DK, RET_DK), F32),
            jax.ShapeDtypeStruct((depth, b, HIST, D_POOL), F32),
        ],
        input_output_aliases={n_in + i: 1 + i for i in range(len(carried))},
        scratch_shapes=[
            pltpu.VMEM((tile, d), BF16),
            pltpu.VMEM((HIST + tile, D_POOL), F32),
            pltpu.VMEM((tile, D_RET), BF16),
            pltpu.VMEM((tile, D_RET), BF16),
            pltpu.VMEM((tile, D_RET), BF16),
            pltpu.VMEM((tile, D_RET), BF16),
            pltpu.VMEM((tile, D_RET), BF16),
            pltpu.VMEM((tile, D_RET), BF16),
            pltpu.VMEM((tile // RET_CHUNK * RET_HEADS, RET_DK, RET_DK), BF16),
            pltpu.VMEM((tile, d), F32),
            pltpu.VMEM((tile, d), F32),
        ],
        compiler_params=pltpu.CompilerParams(
            dimension_semantics=("arbitrary", "arbitrary"), vmem_limit_bytes=VMEM_LIMIT),
        name=f"prompt_layer{l}",
    )(x, mod, g_pre, g_post, w_in, w_fold, pool_scale, w_o, cs, sn, dmask, qdec, kdec, gc, *carried)


def _sample_kernel(x_ref, mod_ref, gpre_ref, gpost_ref, win_ref, wpool_ref, pscale_ref, wo_ref,
                   cs_ref, sn_ref, dmask_ref, qdec_ref, kdec_ref, gc_ref, s0_ref, hist_ref,
                   xo_ref, s_ref, nb_ref,
                   usc_ref, mix_ref, sg_ref, *, bt, dec_seq):
    n = bt * dec_seq
    pool_buf = hist_ref.shape[0]
    rows = pl.ds(pl.multiple_of(pl.program_id(1) * bt, bt), bt)

    @pl.when(pl.program_id(0) == 0)
    def _():
        xo_ref[rows] = x_ref[...]

    m = mod_ref[...][:, None, :]
    shift, scl, gres = m[:, :, 0:D_MODEL], m[:, :, D_MODEL:2 * D_MODEL], m[:, :, 2 * D_MODEL:]

    x3 = xo_ref[rows]
    layer = pl.ds(pl.program_id(0), 1)
    h3 = _rms(x3) * (gpre_ref[layer, :][None] * (1.0 + scl)) + shift
    hb = h3.reshape(n, D_MODEL).astype(BF16)

    o1, o2, o3, o4 = D_POOL, D_POOL + D_RET, D_POOL + 2 * D_RET, D_POOL + 3 * D_RET
    heads = [slice(hh * RET_DK, (hh + 1) * RET_DK) for hh in range(RET_HEADS)]
    u = jnp.dot(hb, win_ref[:, 0:o1], preferred_element_type=F32)
    zq = jnp.dot(hb, win_ref[:, o1:o2], preferred_element_type=F32).reshape(bt, dec_seq, D_RET)
    zk = jnp.dot(hb, win_ref[:, o2:o3], preferred_element_type=F32).reshape(bt, dec_seq, D_RET)
    zv = jnp.dot(hb, win_ref[:, o3:o4], preferred_element_type=F32).reshape(bt, dec_seq, D_RET)
    zg = jnp.dot(hb, win_ref[:, o4:], preferred_element_type=F32)

    pooled = []
    for g, w in enumerate(POOL_WINDOWS):
        lo, hi = g * POOL_CG, (g + 1) * POOL_CG
        usc_ref[g] = u[:, lo:hi]
        rows_g = [hist_ref[r, :, lo:hi] for r in range(pool_buf)]
        rows_g += [usc_ref[g, pl.ds(i, bt, stride=dec_seq), :] for i in range(dec_seq)]
        for i in range(dec_seq):
            cur = pool_buf + i
            acc = rows_g[cur]
            for k in range(1, w):
                acc = acc + rows_g[cur - k]
            usc_ref[g, pl.ds(i, bt, stride=dec_seq), :] = acc / float(w) - rows_g[cur]
        pooled.append(usc_ref[g].astype(BF16))
        for r in range(pool_buf):
            nb_ref[r, :, lo:hi] = rows_g[dec_seq + r]

    cs, sn = cs_ref[...][None], sn_ref[...][None]
    qr = [_rope(zq[:, :, sl], cs, sn) for sl in heads]
    kr = [_rope(zk[:, :, sl], cs, sn) * K_SCALE for sl in heads]
    sg_ref[...] = zg * jax.nn.sigmoid(zg)

    for g in range(len(POOL_WINDOWS)):
        lo, hi = g * POOL_CG, (g + 1) * POOL_CG
        mix_ref[:, lo:hi] = jnp.dot(pooled[g], wpool_ref[g], preferred_element_type=F32) * pscale_ref[layer, :][:, lo:hi]
    scores = [jnp.einsum("bid,bjd->bij", qr[hh], kr[hh], preferred_element_type=F32) * dmask_ref[hh][None]
              for hh in range(RET_HEADS)]
    kv, cross = [], []
    for hh, sl in enumerate(heads):
        kv.append(jnp.einsum("bjd,bje->bde", (kr[hh] * kdec_ref[:, sl][None]).astype(BF16),
                             zv[:, :, sl].astype(BF16), preferred_element_type=F32))
        cross.append(jnp.einsum("bid,bde->bie", qr[hh] * qdec_ref[:, sl][None], s0_ref[:, hh],
                                preferred_element_type=F32))
    intra = [jnp.einsum("bij,bje->bie", scores[hh], zv[:, :, sl], preferred_element_type=F32)
             for hh, sl in enumerate(heads)]
    for hh, sl in enumerate(heads):
        s_ref[:, hh] = gc_ref[0:1, sl][None] * s0_ref[:, hh] + kv[hh]
        mix_ref[:, D_POOL + hh * RET_DK:D_POOL + (hh + 1) * RET_DK] = _rms(intra[hh] + cross[hh]).reshape(n, RET_DK)

    mixb = (mix_ref[...] * sg_ref[...]).astype(BF16)
    y = jnp.dot(mixb, wo_ref[...], preferred_element_type=F32)
    xo_ref[rows] = x3 + (gres * gpost_ref[layer, :][None]) * _rms(y).reshape(bt, dec_seq, D_MODEL)


def _sample_layers(x, mod, g_pre, g_post, w_in, w_pool, pool_scale, w_o, tabs, state_ret, state_pool):
    b, dec_seq, d = x.shape
    depth, pool_buf, _, _ = state_pool.shape
    bt = SAMPLE_BT
    cs, sn, dmask, qdec, kdec, gc = tabs
    layer = lambda l, i: (l, 0, 0)
    kern = functools.partial(_sample_kernel, bt=bt, dec_seq=dec_seq)
    return pl.pallas_call(
        kern,
        grid=(depth, b // bt),
        in_specs=[
            pl.BlockSpec((bt, dec_seq, d), lambda l, i: (i, 0, 0)),
            pl.BlockSpec((None, bt, 3 * d), lambda l, i: (l, i, 0)),
            _const_spec(g_pre.shape),
            _const_spec(g_post.shape),
            pl.BlockSpec((None, d, D_IN), layer, pipeline_mode=pl.Buffered(1)),
            pl.BlockSpec((None, len(POOL_WINDOWS), POOL_CG, POOL_CG), lambda l, i: (l, 0, 0, 0)),
            _const_spec(pool_scale.shape),
            pl.BlockSpec((None, d, d), layer, pipeline_mode=pl.Buffered(1)),
            _const_spec(cs.shape), _const_spec(sn.shape),
            _const_spec(dmask.shape), _const_spec(qdec.shape), _const_spec(kdec.shape), _const_spec(gc.shape),
            pl.BlockSpec((None, bt, RET_HEADS, RET_DK, RET_DK), lambda l, i: (l, i, 0, 0, 0)),
            pl.BlockSpec((None, pool_buf, bt, D_POOL), lambda l, i: (l, 0, i, 0)),
        ],
        out_specs=[
            pl.BlockSpec((b, dec_seq, d), lambda l, i: (0, 0, 0)),
            pl.BlockSpec((None, bt, RET_HEADS, RET_DK, RET_DK), lambda l, i: (l, i, 0, 0, 0)),
            pl.BlockSpec((None, pool_buf, bt, D_POOL), lambda l, i: (l, 0, i, 0)),
        ],
        out_shape=[
            jax.ShapeDtypeStruct((b, dec_seq, d), F32),
            jax.ShapeDtypeStruct((depth, b, RET_HEADS, RET_DK, RET_DK), F32),
            jax.ShapeDtypeStruct((depth, pool_buf, b, D_POOL), F32),
        ],
        scratch_shapes=[
            pltpu.VMEM((len(POOL_WINDOWS), bt * dec_seq, POOL_CG), F32),
            pltpu.VMEM((bt * dec_seq, d), F32),
            pltpu.VMEM((bt * dec_seq, d), F32),
        ],
        compiler_params=pltpu.CompilerParams(
            dimension_semantics=("arbitrary", "arbitrary"), vmem_limit_bytes=VMEM_LIMIT),
        name="sample_layers",
    )(x, mod, g_pre, g_post, w_in, w_pool, pool_scale, w_o, cs, sn, dmask, qdec, kdec, gc, state_ret, state_pool)


def kernel(x_prompt, x_sample, c_prompt, c_sample, state_ret, state_pool, w_ada, b_ada,
           g_pre, g_post, w_in, w_pool, pool_scale, w_o):
    depth = w_in.shape[0]
    bp, seq, d = x_prompt.shape
    bs, dec_seq, _ = x_sample.shape
    assert d == D_MODEL and w_in.shape[2] == D_IN and state_pool.shape[2] == HIST - 1
    assert seq % PROMPT_TILE == 0 and bs % SAMPLE_BT == 0
    assert dec_seq == 8 and dec_seq % RET_CHUNK != 0

    (csp, snp, css, sns, dmp, qdp, kdp, gcp, dms, qds, kds, gcs) = _make_tables(seq, dec_seq, PAST_LEN)
    tabs_p = (csp, snp, dmp, qdp, kdp, gcp)
    tabs_s = (css, sns, dms, qds, kds, gcs)

    mod_s, mod_p, w_in_b, w_o_b, w_pool_b, w_fold_b, *stacked = _prep(c_sample, c_prompt, w_ada, b_ada,
                                                                      w_in, w_o, w_pool)
    hs, ret_s, pool_s = _sample_layers(x_sample, mod_s, g_pre, g_post, w_in_b, w_pool_b, pool_scale, w_o_b,
                                       tabs_s, state_ret, jnp.transpose(state_pool, (0, 2, 1, 3)))
    pool_s = jnp.transpose(pool_s, (0, 2, 1, 3))
    hp = x_prompt
    for l in range(depth):
        hp, *stacked = _prompt_layer(l, hp, mod_p, g_pre, g_post, w_in_b, w_fold_b, pool_scale, w_o_b, tabs_p,
                                     stacked)
    ret_p, hist_p = stacked
    return (hp, hs, ret_p, hist_p[:, :, HIST - state_pool.shape[2]:], ret_s, pool_s)
```

```python
import functools

import jax
import jax.numpy as jnp
from jax import lax
from jax.experimental import pallas as pl
from jax.experimental.pallas import tpu as pltpu

F32 = jnp.float32
BF16 = jnp.bfloat16

D_MODEL = 1024
D_POOL = 512
D_RET = 512
POOL_WINDOWS = (2, 4, 8, 16)
POOL_CG = 128
HIST = 16
RET_HEADS = 4
RET_DK = 128
RET_CHUNK = 128
ROPE_BASE = 10000.0
PAST_LEN = 16384
ROPE_ROWS = 128
D_IN = D_POOL + 3 * D_RET + D_MODEL
EPS = 1e-6
K_SCALE = RET_DK ** -0.5

PROMPT_TILE = 1024
PROMPT_BLOCK = 256
SAMPLE_BT = 32
PREP_ROWS = 256
VMEM_LIMIT = 60 * 1024 * 1024


def _const_spec(shape):
    return pl.BlockSpec(shape, lambda *_: (0,) * len(shape))


def _decay_tables(lg_lane, lg_row, c, dmask_ref, qdec_ref, kdec_ref, gc_ref):
    idx = lax.broadcasted_iota(jnp.int32, (c, D_RET), 0).astype(F32)
    qdec_ref[...] = jnp.exp((idx + 1.0) * lg_lane)
    kdec_ref[...] = jnp.exp((c - 1.0 - idx) * lg_lane)
    gc_ref[...] = jnp.exp(jnp.full((8, D_RET), float(c), F32) * lg_lane)
    ii = lax.broadcasted_iota(jnp.int32, (c, c), 0)
    jj = lax.broadcasted_iota(jnp.int32, (c, c), 1)
    diff = ii - jj
    for h in range(RET_HEADS):
        dec = jnp.exp(jnp.maximum(diff, 0).astype(F32) * lg_row[h])
        dmask_ref[h] = jnp.where(diff >= 0, dec, 0.0)


def _tables_kernel(inv_ref, csp_ref, snp_ref, css_ref, sns_ref,
                   dmp_ref, qdp_ref, kdp_ref, gcp_ref,
                   dms_ref, qds_ref, kds_ref, gcs_ref, *, seq, dec_seq, past_len):
    inv = inv_ref[...]
    lane = lax.broadcasted_iota(jnp.int32, (1, RET_DK), 1)
    sign = jnp.where(lane < RET_DK // 2, -1.0, 1.0).astype(F32)

    def angles(n, start, step):
        pos = (lax.broadcasted_iota(jnp.int32, (n, RET_DK), 0) * step + start).astype(F32)
        return pos * inv

    ang = angles(dec_seq, past_len, 1)
    css_ref[...] = jnp.cos(ang)
    sns_ref[...] = jnp.sin(ang) * sign
    lo = angles(ROPE_ROWS, 0, 1)
    hi = angles(seq // ROPE_ROWS, 0, ROPE_ROWS)
    c_lo, s_lo, c_hi, s_hi = jnp.cos(lo), jnp.sin(lo), jnp.cos(hi), jnp.sin(hi)
    for r in range(seq // ROPE_ROWS):
        rows = slice(r * ROPE_ROWS, (r + 1) * ROPE_ROWS)
        csp_ref[rows, :] = c_hi[r:r + 1] * c_lo - s_hi[r:r + 1] * s_lo
        snp_ref[rows, :] = (s_hi[r:r + 1] * c_lo + c_hi[r:r + 1] * s_lo) * sign

    head_lane = (lax.broadcasted_iota(jnp.int32, (1, D_RET), 1) // RET_DK).astype(F32)
    lg_lane = jnp.log(1.0 - jnp.exp2(-5.0 - head_lane))
    for c, refs in ((RET_CHUNK, (dmp_ref, qdp_ref, kdp_ref, gcp_ref)),
                    (dec_seq, (dms_ref, qds_ref, kds_ref, gcs_ref))):
        lg_row = [jnp.log(1.0 - jnp.exp2(jnp.full((1, c), -5.0 - h, F32))) for h in range(RET_HEADS)]
        _decay_tables(lg_lane, lg_row, c, *refs)


def _make_tables(seq, dec_seq, past_len):
    half = RET_DK // 2
    inv = 1.0 / (ROPE_BASE ** (jnp.arange(half, dtype=F32) / half))
    inv2 = jnp.concatenate([inv, inv])[None, :]
    c, cs = RET_CHUNK, dec_seq
    out_shape = (
        jax.ShapeDtypeStruct((seq, RET_DK), F32), jax.ShapeDtypeStruct((seq, RET_DK), F32),
        jax.ShapeDtypeStruct((dec_seq, RET_DK), F32), jax.ShapeDtypeStruct((dec_seq, RET_DK), F32),
        jax.ShapeDtypeStruct((RET_HEADS, c, c), F32), jax.ShapeDtypeStruct((c, D_RET), F32),
        jax.ShapeDtypeStruct((c, D_RET), F32), jax.ShapeDtypeStruct((8, D_RET), F32),
        jax.ShapeDtypeStruct((RET_HEADS, cs, cs), F32), jax.ShapeDtypeStruct((cs, D_RET), F32),
        jax.ShapeDtypeStruct((cs, D_RET), F32), jax.ShapeDtypeStruct((8, D_RET), F32),
    )
    return pl.pallas_call(
        functools.partial(_tables_kernel, seq=seq, dec_seq=dec_seq, past_len=past_len),
        out_shape=out_shape,
        name="tables",
    )(inv2)


def _prep_kernel(cs_ref, cp_ref, wada_ref, bada_ref, win_ref, wo_ref, wpool_ref,
                 mods_ref, modp_ref, winb_ref, wob_ref, wpoolb_ref, wfold_ref, ret0_ref, hist0_ref):
    j = pl.program_id(1)
    w = wada_ref[0].astype(BF16)
    for c_ref, o_ref in ((cs_ref, mods_ref), (cp_ref, modp_ref)):
        c = c_ref[...]
        part = jnp.dot((c * jax.nn.sigmoid(c)).astype(BF16), w, preferred_element_type=F32)

        @pl.when(j == 0)
        def _():
            o_ref[0] = part + bada_ref[pl.ds(pl.program_id(0), 1), :]

        @pl.when(j > 0)
        def _():
            o_ref[0] += part
    winb_ref[0] = win_ref[0].astype(BF16)
    wob_ref[0] = wo_ref[0].astype(BF16)
    wpoolb_ref[0] = wpool_ref[0].astype(BF16)

    @pl.when(j == 0)
    def _():
        ret0_ref[...] = jnp.zeros_like(ret0_ref)
        hist0_ref[...] = jnp.zeros_like(hist0_ref)

    for g in range(len(POOL_WINDOWS)):
        lo, hi = g * POOL_CG, (g + 1) * POOL_CG
        wfold_ref[0, :, lo:hi] = jnp.dot(winb_ref[0, :, lo:hi], wpoolb_ref[0, g],
                                         preferred_element_type=F32).astype(BF16)


def _prep(c_sample, c_prompt, w_ada, b_ada, w_in, w_o, w_pool):
    depth, d, d3 = w_ada.shape
    bs, bp = c_sample.shape[0], c_prompt.shape[0]
    rows = lambda l, j: (l, j, 0)
    whole = lambda l, j: (l, 0, 0)
    return pl.pallas_call(
        _prep_kernel,
        grid=(depth, d // PREP_ROWS),
        in_specs=[
            pl.BlockSpec((bs, PREP_ROWS), lambda l, j: (0, j)),
            pl.BlockSpec((bp, PREP_ROWS), lambda l, j: (0, j)),
            pl.BlockSpec((1, PREP_ROWS, d3), rows),
            _const_spec(b_ada.shape),
            pl.BlockSpec((1, PREP_ROWS, w_in.shape[2]), rows),
            pl.BlockSpec((1, PREP_ROWS, w_o.shape[2]), rows),
            pl.BlockSpec((1,) + w_pool.shape[1:], lambda l, j: (l, 0, 0, 0)),
        ],
        out_specs=[
            pl.BlockSpec((1, bs, d3), whole),
            pl.BlockSpec((1, bp, d3), whole),
            pl.BlockSpec((1, PREP_ROWS, w_in.shape[2]), rows),
            pl.BlockSpec((1, PREP_ROWS, w_o.shape[2]), rows),
            pl.BlockSpec((1,) + w_pool.shape[1:], lambda l, j: (l, 0, 0, 0)),
            pl.BlockSpec((1, PREP_ROWS, D_POOL), rows),
            pl.BlockSpec((1, bp, RET_HEADS, RET_DK, RET_DK), lambda l, j: (l, 0, 0, 0, 0)),
            pl.BlockSpec((1, bp, HIST, D_POOL), lambda l, j: (l, 0, 0, 0)),
        ],
        out_shape=[
            jax.ShapeDtypeStruct((depth, bs, d3), F32),
            jax.ShapeDtypeStruct((depth, bp, d3), F32),
            jax.ShapeDtypeStruct(w_in.shape, BF16),
            jax.ShapeDtypeStruct(w_o.shape, BF16),
            jax.ShapeDtypeStruct(w_pool.shape, BF16),
            jax.ShapeDtypeStruct((depth, d, D_POOL), BF16),
            jax.ShapeDtypeStruct((depth, bp, RET_HEADS, RET_DK, RET_DK), F32),
            jax.ShapeDtypeStruct((depth, bp, HIST, D_POOL), F32),
        ],
        compiler_params=pltpu.CompilerParams(
            dimension_semantics=("arbitrary", "arbitrary"), vmem_limit_bytes=VMEM_LIMIT),
        name="prep",
    )(c_sample, c_prompt, w_ada, b_ada, w_in, w_o, w_pool)


def _rope(x, cs, sn):
    return x * cs + pltpu.roll(x, RET_DK // 2, x.ndim - 1) * sn


def _rms(x):
    return x * lax.rsqrt(jnp.mean(x * x, axis=-1, keepdims=True) + EPS)


def _prompt_kernel(x_ref, mod_ref, gpre_ref, gpost_ref, win_ref, wfold_ref, pscale_ref, wo_ref,
                   cs_ref, sn_ref, dmask_ref, qdec_ref, kdec_ref, gc_ref,
                   xo_ref, s_ref, nb_ref,
                   hb_ref, ubuf_ref, q_ref, qd_ref, k_ref, kd_ref, v_ref, sc_ref, sprev_ref, mix_ref, sg_ref,
                   *, tile, layer):
    t = pl.program_id(1)
    n_chunks = tile // RET_CHUNK
    blocks = [(r0, r0 + PROMPT_BLOCK) for r0 in range(0, tile, PROMPT_BLOCK)]
    o1, o2, o3, o4 = D_POOL, D_POOL + D_RET, D_POOL + 2 * D_RET, D_POOL + 3 * D_RET
    heads = [slice(hh * RET_DK, (hh + 1) * RET_DK) for hh in range(RET_HEADS)]
    chunks = [slice(c * RET_CHUNK, (c + 1) * RET_CHUNK) for c in range(n_chunks)]

    @pl.when(t == 0)
    def _():
        s_ref[...] = jnp.zeros_like(s_ref)
        ubuf_ref[0:HIST, :] = jnp.zeros((HIST, D_POOL), F32)

    m = mod_ref[pl.ds(pl.program_id(0), 1), :]
    shift, scl, gres = m[:, 0:D_MODEL], m[:, D_MODEL:2 * D_MODEL], m[:, 2 * D_MODEL:]
    pre_gain = gpre_ref[layer:layer + 1, :] * (1.0 + scl)
    post_gain = gres * gpost_ref[layer:layer + 1, :]

    def proj(r0, r1, c0, c1):
        return jnp.dot(hb_ref[r0:r1, :], win_ref[:, c0:c1], preferred_element_type=F32)

    zqk, kvs = {}, {}

    def pre_norm(j):
        r0, r1 = blocks[j]
        hb_ref[r0:r1, :] = (_rms(x_ref[0, r0:r1, :]) * pre_gain + shift).astype(BF16)

    def project(j):
        r0, r1 = blocks[j]
        ubuf_ref[HIST + r0:HIST + r1, :] = jnp.dot(hb_ref[r0:r1, :], wfold_ref[...], preferred_element_type=F32)
        zqk[j] = (proj(r0, r1, o1, o2), proj(r0, r1, o2, o3))
        v_ref[r0:r1, :] = proj(r0, r1, o3, o4).astype(BF16)

    def mixer_inputs(j):
        r0, r1 = blocks[j]
        row = lax.broadcasted_iota(jnp.int32, (r1 - r0, 1), 0) + (t * tile + r0)
        for g, w in enumerate(POOL_WINDOWS):
            lo, hi = g * POOL_CG, (g + 1) * POOL_CG
            u_ext = ubuf_ref[r0:r1 + HIST, lo:hi]
            acc = u_ext
            k = 1
            while k < w:
                acc = acc + pltpu.roll(acc, k, 0)
                k *= 2
            cnt = jnp.minimum(row + 1, w).astype(F32)
            mix_ref[r0:r1, lo:hi] = (acc[HIST:] / cnt - u_ext[HIST:]) * pscale_ref[layer:layer + 1, lo:hi]
        zq, zk = zqk.pop(j)
        cs, sn = cs_ref[r0:r1, :], sn_ref[r0:r1, :]
        for sl in heads:
            qr = _rope(zq[:, sl], cs, sn)
            kr = _rope(zk[:, sl], cs, sn) * K_SCALE
            q_ref[r0:r1, sl] = qr.astype(BF16)
            k_ref[r0:r1, sl] = kr.astype(BF16)
            for c in range(r0 // RET_CHUNK, r1 // RET_CHUNK):
                rs, ls = chunks[c], slice(c * RET_CHUNK - r0, (c + 1) * RET_CHUNK - r0)
                qd_ref[rs, sl] = (qr[ls] * qdec_ref[:, sl]).astype(BF16)
                kd_ref[rs, sl] = (kr[ls] * kdec_ref[:, sl]).astype(BF16)

    def small_matmuls(j):
        r0, r1 = blocks[j]
        for c in range(r0 // RET_CHUNK, r1 // RET_CHUNK):
            rs = chunks[c]
            for hh, sl in enumerate(heads):
                scores = lax.dot_general(q_ref[rs, sl], k_ref[rs, sl], (((1,), (1,)), ((), ())),
                                         preferred_element_type=F32)
                sc_ref[rs, sl] = (scores * dmask_ref[hh]).astype(BF16)
                kvs[c, hh] = lax.dot_general(kd_ref[rs, sl], v_ref[rs, sl], (((0,), (0,)), ((), ())),
                                             preferred_element_type=F32)

    def gate(j):
        r0, r1 = blocks[j]
        zg = proj(r0, r1, o4, D_IN)
        sg_ref[r0:r1, :] = zg * jax.nn.sigmoid(zg)

    def state_recurrence():
        for hh, sl in enumerate(heads):
            s = s_ref[0, hh]
            for c in range(n_chunks):
                sprev_ref[c * RET_HEADS + hh] = s.astype(BF16)
                s = gc_ref[0:1, sl] * s + kvs.pop((c, hh))
            s_ref[0, hh] = s

    def retention_out(j):
        r0, r1 = blocks[j]
        for c in range(r0 // RET_CHUNK, r1 // RET_CHUNK):
            rs = chunks[c]
            for hh, sl in enumerate(heads):
                lhs = jnp.concatenate([sc_ref[rs, sl], qd_ref[rs, sl]], axis=1)
                rhs = jnp.concatenate([v_ref[rs, sl], sprev_ref[c * RET_HEADS + hh]], axis=0)
                o_h = jnp.dot(lhs, rhs, preferred_element_type=F32)
                mix_ref[rs, D_POOL + hh * RET_DK:D_POOL + (hh + 1) * RET_DK] = _rms(o_h)

    def out_proj(j):
        r0, r1 = blocks[j]
        mixb = (mix_ref[r0:r1, :] * sg_ref[r0:r1, :]).astype(BF16)
        y = jnp.dot(mixb, wo_ref[...], preferred_element_type=F32)
        xo_ref[0, r0:r1, :] = x_ref[0, r0:r1, :] + _rms(y) * post_gain

    n = len(blocks)
    pre_norm(0)
    project(0)
    for j in range(n):
        if j + 1 < n:
            pre_norm(j + 1)
        mixer_inputs(j)
        if j + 1 < n:
            project(j + 1)
        else:
            ubuf_ref[0:HIST, :] = ubuf_ref[tile:tile + HIST, :]
            gate(0)
        small_matmuls(j)
    gate(1)
    state_recurrence()

    gates_left = list(range(2, n))
    for j in range(n):
        retention_out(j)
        if gates_left:
            gate(gates_left.pop(0))
        if j >= 1:
            out_proj(j - 1)
    out_proj(n - 1)

    @pl.when(t == pl.num_programs(1) - 1)
    def _():
        nb_ref[0] = jnp.dot(hb_ref[tile - HIST:tile, :], win_ref[:, 0:o1], preferred_element_type=F32)


def _prompt_layer(l, x, mod, g_pre, g_post, w_in, w_fold, pool_scale, w_o, tabs, stacked):
    depth = w_in.shape[0]
    b, seq, d = x.shape
    tile = PROMPT_TILE
    cs, sn, dmask, qdec, kdec, gc = tabs
    layer = lambda *_: (l, 0, 0)
    n_in = 14
    carried = tuple(stacked)

    def kern(*refs):
        _prompt_kernel(*refs[:n_in], *refs[n_in + len(carried):], tile=tile, layer=l)

    return pl.pallas_call(
        kern,
        grid=(b, seq // tile),
        in_specs=[
            pl.BlockSpec((1, tile, d), lambda i, t: (i, t, 0)),
            pl.BlockSpec((None, b, 3 * d), layer),
            _const_spec(g_pre.shape),
            _const_spec(g_post.shape),
            pl.BlockSpec((None, d, D_IN), layer, pipeline_mode=pl.Buffered(1)),
            pl.BlockSpec((None, d, D_POOL), layer, pipeline_mode=pl.Buffered(1)),
            _const_spec(pool_scale.shape),
            pl.BlockSpec((None, d, d), layer, pipeline_mode=pl.Buffered(1)),
            pl.BlockSpec((tile, RET_DK), lambda i, t: (t, 0)),
            pl.BlockSpec((tile, RET_DK), lambda i, t: (t, 0)),
            _const_spec(dmask.shape), _const_spec(qdec.shape), _const_spec(kdec.shape), _const_spec(gc.shape),
        ] + [pl.BlockSpec(memory_space=pl.ANY)] * len(carried),
        out_specs=[
            pl.BlockSpec((1, tile, d), lambda i, t: (i, t, 0)),
            pl.BlockSpec((None, 1, RET_HEADS, RET_DK, RET_DK), lambda i, t: (l, i, 0, 0, 0)),
            pl.BlockSpec((None, 1, HIST, D_POOL), lambda i, t: (l, i, 0, 0)),
        ],
        out_shape=[
            jax.ShapeDtypeStruct((b, seq, d), F32),
            jax.ShapeDtypeStruct((depth, b, RET_HEADS, RET_DK, RET_DK), F32),
            jax.ShapeDtypeStruct((depth, b, HIST, D_POOL), F32),
        ],
        input_output_aliases={n_in + i: 1 + i for i in range(len(carried))},
        scratch_shapes=[
            pltpu.VMEM((tile, d), BF16),
            pltpu.VMEM((HIST + tile, D_POOL), F32),
            pltpu.VMEM((tile, D_RET), BF16),
            pltpu.VMEM((tile, D_RET), BF16),
            pltpu.VMEM((tile, D_RET), BF16),
            pltpu.VMEM((tile, D_RET), BF16),
            pltpu.VMEM((tile, D_RET), BF16),
            pltpu.VMEM((tile, D_RET), BF16),
            pltpu.VMEM((tile // RET_CHUNK * RET_HEADS, RET_DK, RET_DK), BF16),
            pltpu.VMEM((tile, d), F32),
            pltpu.VMEM((tile, d), F32),
        ],
        compiler_params=pltpu.CompilerParams(
            dimension_semantics=("arbitrary", "arbitrary"), vmem_limit_bytes=VMEM_LIMIT),
        name=f"prompt_layer{l}",
    )(x, mod, g_pre, g_post, w_in, w_fold, pool_scale, w_o, cs, sn, dmask, qdec, kdec, gc, *carried)


def _sample_kernel(x_ref, mod_ref, gpre_ref, gpost_ref, win_ref, wpool_ref, pscale_ref, wo_ref,
                   cs_ref, sn_ref, dmask_ref, qdec_ref, kdec_ref, gc_ref, s0_ref, hist_ref,
                   xo_ref, s_ref, nb_ref,
                   usc_ref, mix_ref, sg_ref, *, bt, dec_seq):
    n = bt * dec_seq
    pool_buf = hist_ref.shape[0]
    rows = pl.ds(pl.multiple_of(pl.program_id(1) * bt, bt), bt)

    @pl.when(pl.program_id(0) == 0)
    def _():
        xo_ref[rows] = x_ref[...]

    m = mod_ref[...][:, None, :]
    shift, scl, gres = m[:, :, 0:D_MODEL], m[:, :, D_MODEL:2 * D_MODEL], m[:, :, 2 * D_MODEL:]

    x3 = xo_ref[rows]
    layer = pl.ds(pl.program_id(0), 1)
    h3 = _rms(x3) * (gpre_ref[layer, :][None] * (1.0 + scl)) + shift
    hb = h3.reshape(n, D_MODEL).astype(BF16)

    o1, o2, o3, o4 = D_POOL, D_POOL + D_RET, D_POOL + 2 * D_RET, D_POOL + 3 * D_RET
    heads = [slice(hh * RET_DK, (hh + 1) * RET_DK) for hh in range(RET_HEADS)]
    u = jnp.dot(hb, win_ref[:, 0:o1], preferred_element_type=F32)
    zq = jnp.dot(hb, win_ref[:, o1:o2], preferred_element_type=F32).reshape(bt, dec_seq, D_RET)
    zk = jnp.dot(hb, win_ref[:, o2:o3], preferred_element_type=F32).reshape(bt, dec_seq, D_RET)
    zv = jnp.dot(hb, win_ref[:, o3:o4], preferred_element_type=F32).reshape(bt, dec_seq, D_RET)
    zg = jnp.dot(hb, win_ref[:, o4:], preferred_element_type=F32)

    pooled = []
    for g, w in enumerate(POOL_WINDOWS):
        lo, hi = g * POOL_CG, (g + 1) * POOL_CG
        usc_ref[g] = u[:, lo:hi]
        rows_g = [hist_ref[r, :, lo:hi] for r in range(pool_buf)]
        rows_g += [usc_ref[g, pl.ds(i, bt, stride=dec_seq), :] for i in range(dec_seq)]
        for i in range(dec_seq):
            cur = pool_buf + i
            acc = rows_g[cur]
            for k in range(1, w):
                acc = acc + rows_g[cur - k]
            usc_ref[g, pl.ds(i, bt, stride=dec_seq), :] = acc / float(w) - rows_g[cur]
        pooled.append(usc_ref[g].astype(BF16))
        for r in range(pool_buf):
            nb_ref[r, :, lo:hi] = rows_g[dec_seq + r]

    cs, sn = cs_ref[...][None], sn_ref[...][None]
    qr = [_rope(zq[:, :, sl], cs, sn) for sl in heads]
    kr = [_rope(zk[:, :, sl], cs, sn) * K_SCALE for sl in heads]
    sg_ref[...] = zg * jax.nn.sigmoid(zg)

    for g in range(len(POOL_WINDOWS)):
        lo, hi = g * POOL_CG, (g + 1) * POOL_CG
        mix_ref[:, lo:hi] = jnp.dot(pooled[g], wpool_ref[g], preferred_element_type=F32) * pscale_ref[layer, :][:, lo:hi]
    scores = [jnp.einsum("bid,bjd->bij", qr[hh], kr[hh], preferred_element_type=F32) * dmask_ref[hh][None]
              for hh in range(RET_HEADS)]
    kv, cross = [], []
    for hh, sl in enumerate(heads):
        kv.append(jnp.einsum("bjd,bje->bde", (kr[hh] * kdec_ref[:, sl][None]).astype(BF16),
                             zv[:, :, sl].astype(BF16), preferred_element_type=F32))
        cross.append(jnp.einsum("bid,bde->bie", qr[hh] * qdec_ref[:, sl][None], s0_ref[:, hh],
                                preferred_element_type=F32))
    intra = [jnp.einsum("bij,bje->bie", scores[hh], zv[:, :, sl], preferred_element_type=F32)
             for hh, sl in enumerate(heads)]
    for hh, sl in enumerate(heads):
        s_ref[:, hh] = gc_ref[0:1, sl][None] * s0_ref[:, hh] + kv[hh]
        mix_ref[:, D_POOL + hh * RET_DK:D_POOL + (hh + 1) * RET_DK] = _rms(intra[hh] + cross[hh]).reshape(n, RET_DK)

    mixb = (mix_ref[...] * sg_ref[...]).astype(BF16)
    y = jnp.dot(mixb, wo_ref[...], preferred_element_type=F32)
    xo_ref[rows] = x3 + (gres * gpost_ref[layer, :][None]) * _rms(y).reshape(bt, dec_seq, D_MODEL)


def _sample_layers(x, mod, g_pre, g_post, w_in, w_pool, pool_scale, w_o, tabs, state_ret, state_pool):
    b, dec_seq, d = x.shape
    depth, pool_buf, _, _ = state_pool.shape
    bt = SAMPLE_BT
    cs, sn, dmask, qdec, kdec, gc = tabs
    layer = lambda l, i: (l, 0, 0)
    kern = functools.partial(_sample_kernel, bt=bt, dec_seq=dec_seq)
    return pl.pallas_call(
        kern,
        grid=(depth, b // bt),
        in_specs=[
            pl.BlockSpec((bt, dec_seq, d), lambda l, i: (i, 0, 0)),
            pl.BlockSpec((None, bt, 3 * d), lambda l, i: (l, i, 0)),
            _const_spec(g_pre.shape),
            _const_spec(g_post.shape),
            pl.BlockSpec((None, d, D_IN), layer, pipeline_mode=pl.Buffered(1)),
            pl.BlockSpec((None, len(POOL_WINDOWS), POOL_CG, POOL_CG), lambda l, i: (l, 0, 0, 0)),
            _const_spec(pool_scale.shape),
            pl.BlockSpec((None, d, d), layer, pipeline_mode=pl.Buffered(1)),
            _const_spec(cs.shape), _const_spec(sn.shape),
            _const_spec(dmask.shape), _const_spec(qdec.shape), _const_spec(kdec.shape), _const_spec(gc.shape),
            pl.BlockSpec((None, bt, RET_HEADS, RET_DK, RET_DK), lambda l, i: (l, i, 0, 0, 0)),
            pl.BlockSpec((None, pool_buf, bt, D_POOL), lambda l, i: (l, 0, i, 0)),
        ],
        out_specs=[
            pl.BlockSpec((b, dec_seq, d), lambda l, i: (0, 0, 0)),
            pl.BlockSpec((None, bt, RET_HEADS, RET_DK, RET_DK), lambda l, i: (l, i, 0, 0, 0)),
            pl.BlockSpec((None, pool_buf, bt, D_POOL), lambda l, i: (l, 0, i, 0)),
        ],
        out_shape=[
            jax.ShapeDtypeStruct((b, dec_seq, d), F32),
            jax.ShapeDtypeStruct((depth, b, RET_HEADS, RET_DK, RET_DK), F32),
            jax.ShapeDtypeStruct((depth, pool_buf, b, D_POOL), F32),
        ],
        scratch_shapes=[
            pltpu.VMEM((len(POOL_WINDOWS), bt * dec_seq, POOL_CG), F32),
            pltpu.VMEM((bt * dec_seq, d), F32),
            pltpu.VMEM((bt * dec_seq, d), F32),
        ],
        compiler_params=pltpu.CompilerParams(
            dimension_semantics=("arbitrary", "arbitrary"), vmem_limit_bytes=VMEM_LIMIT),
        name="sample_layers",
    )(x, mod, g_pre, g_post, w_in, w_pool, pool_scale, w_o, cs, sn, dmask, qdec, kdec, gc, state_ret, state_pool)


def kernel(x_prompt, x_sample, c_prompt, c_sample, state_ret, state_pool, w_ada, b_ada,
           g_pre, g_post, w_in, w_pool, pool_scale, w_o):
    depth = w_in.shape[0]
    bp, seq, d = x_prompt.shape
    bs, dec_seq, _ = x_sample.shape
    assert d == D_MODEL and w_in.shape[2] == D_IN and state_pool.shape[2] == HIST - 1
    assert seq % PROMPT_TILE == 0 and bs % SAMPLE_BT == 0
    assert dec_seq == 8 and dec_seq % RET_CHUNK != 0

    (csp, snp, css, sns, dmp, qdp, kdp, gcp, dms, qds, kds, gcs) = _make_tables(seq, dec_seq, PAST_LEN)
    tabs_p = (csp, snp, dmp, qdp, kdp, gcp)
    tabs_s = (css, sns, dms, qds, kds, gcs)

    mod_s, mod_p, w_in_b, w_o_b, w_pool_b, w_fold_b, *stacked = _prep(c_sample, c_prompt, w_ada, b_ada,
                                                                      w_in, w_o, w_pool)
    hs, ret_s, pool_s = _sample_layers(x_sample, mod_s, g_pre, g_post, w_in_b, w_pool_b, pool_scale, w_o_b,
                                       tabs_s, state_ret, jnp.transpose(state_pool, (0, 2, 1, 3)))
    pool_s = jnp.transpose(pool_s, (0, 2, 1, 3))
    hp = x_prompt
    for l in range(depth):
        hp, *stacked = _prompt_layer(l, hp, mod_p, g_pre, g_post, w_in_b, w_fold_b, pool_scale, w_o_b, tabs_p,
                                     stacked)
    ret_p, hist_p = stacked
    return (hp, hs, ret_p, hist_p[:, :, HIST - state_pool.shape[2]:], ret_s, pool_s)
```

```python
import functools

import jax
import jax.numpy as jnp
from jax import lax
from jax.experimental import pallas as pl
from jax.experimental.pallas import tpu as pltpu

F32 = jnp.float32
BF16 = jnp.bfloat16

D_MODEL = 1024
D_POOL = 512
D_RET = 512
POOL_WINDOWS = (2, 4, 8, 16)
POOL_CG = 128
HIST = 16
RET_HEADS = 4
RET_DK = 128
RET_CHUNK = 128
ROPE_BASE = 10000.0
PAST_LEN = 16384
ROPE_ROWS = 128
D_IN = D_POOL + 3 * D_RET + D_MODEL
EPS = 1e-6
K_SCALE = RET_DK ** -0.5

PROMPT_TILE = 1024
PROMPT_BLOCK = 256
SAMPLE_BT = 32
PREP_ROWS = 256
VMEM_LIMIT = 60 * 1024 * 1024


def _const_spec(shape):
    return pl.BlockSpec(shape, lambda *_: (0,) * len(shape))


def _decay_tables(lg_lane, lg_row, c, dmask_ref, qdec_ref, kdec_ref, gc_ref):
    idx = lax.broadcasted_iota(jnp.int32, (c, D_RET), 0).astype(F32)
    qdec_ref[...] = jnp.exp((idx + 1.0) * lg_lane)
    kdec_ref[...] = jnp.exp((c - 1.0 - idx) * lg_lane)
    gc_ref[...] = jnp.exp(jnp.full((8, D_RET), float(c), F32) * lg_lane)
    ii = lax.broadcasted_iota(jnp.int32, (c, c), 0)
    jj = lax.broadcasted_iota(jnp.int32, (c, c), 1)
    diff = ii - jj
    for h in range(RET_HEADS):
        dec = jnp.exp(jnp.maximum(diff, 0).astype(F32) * lg_row[h])
        dmask_ref[h] = jnp.where(diff >= 0, dec, 0.0)


def _tables_kernel(inv_ref, csp_ref, snp_ref, css_ref, sns_ref,
                   dmp_ref, qdp_ref, kdp_ref, gcp_ref,
                   dms_ref, qds_ref, kds_ref, gcs_ref, ret0_ref, tail0_ref, *, seq, dec_seq, past_len):
    ret0_ref[...] = jnp.zeros_like(ret0_ref)
    tail0_ref[...] = jnp.zeros_like(tail0_ref)

    inv = inv_ref[...]
    lane = lax.broadcasted_iota(jnp.int32, (1, RET_DK), 1)
    sign = jnp.where(lane < RET_DK // 2, -1.0, 1.0).astype(F32)

    def angles(n, start, step):
        pos = (lax.broadcasted_iota(jnp.int32, (n, RET_DK), 0) * step + start).astype(F32)
        return pos * inv

    ang = angles(dec_seq, past_len, 1)
    css_ref[...] = jnp.cos(ang)
    sns_ref[...] = jnp.sin(ang) * sign
    lo = angles(ROPE_ROWS, 0, 1)
    hi = angles(seq // ROPE_ROWS, 0, ROPE_ROWS)
    c_lo, s_lo, c_hi, s_hi = jnp.cos(lo), jnp.sin(lo), jnp.cos(hi), jnp.sin(hi)
    for r in range(seq // ROPE_ROWS):
        rows = slice(r * ROPE_ROWS, (r + 1) * ROPE_ROWS)
        csp_ref[rows, :] = c_hi[r:r + 1] * c_lo - s_hi[r:r + 1] * s_lo
        snp_ref[rows, :] = (s_hi[r:r + 1] * c_lo + c_hi[r:r + 1] * s_lo) * sign

    head_lane = (lax.broadcasted_iota(jnp.int32, (1, D_RET), 1) // RET_DK).astype(F32)
    lg_lane = jnp.log(1.0 - jnp.exp2(-5.0 - head_lane))
    for c, refs in ((RET_CHUNK, (dmp_ref, qdp_ref, kdp_ref, gcp_ref)),
                    (dec_seq, (dms_ref, qds_ref, kds_ref, gcs_ref))):
        lg_row = [jnp.log(1.0 - jnp.exp2(jnp.full((1, c), -5.0 - h, F32))) for h in range(RET_HEADS)]
        _decay_tables(lg_lane, lg_row, c, *refs)


def _make_tables(seq, dec_seq, past_len, depth, bp):
    half = RET_DK // 2
    inv = 1.0 / (ROPE_BASE ** (jnp.arange(half, dtype=F32) / half))
    inv2 = jnp.concatenate([inv, inv])[None, :]
    c, cs = RET_CHUNK, dec_seq
    out_shape = (
        jax.ShapeDtypeStruct((seq, RET_DK), F32), jax.ShapeDtypeStruct((seq, RET_DK), F32),
        jax.ShapeDtypeStruct((dec_seq, RET_DK), F32), jax.ShapeDtypeStruct((dec_seq, RET_DK), F32),
        jax.ShapeDtypeStruct((RET_HEADS, c, c), F32), jax.ShapeDtypeStruct((c, D_RET), F32),
        jax.ShapeDtypeStruct((c, D_RET), F32), jax.ShapeDtypeStruct((8, D_RET), F32),
        jax.ShapeDtypeStruct((RET_HEADS, cs, cs), F32), jax.ShapeDtypeStruct((cs, D_RET), F32),
        jax.ShapeDtypeStruct((cs, D_RET), F32), jax.ShapeDtypeStruct((8, D_RET), F32),
        jax.ShapeDtypeStruct((depth, bp, RET_HEADS, RET_DK, RET_DK), F32),
        jax.ShapeDtypeStruct((depth, bp, HIST, D_MODEL), BF16),
    )
    return pl.pallas_call(
        functools.partial(_tables_kernel, seq=seq, dec_seq=dec_seq, past_len=past_len),
        out_shape=out_shape,
        name="tables",
    )(inv2)


def _prep_kernel(cs_ref, cp_ref, wada_ref, bada_ref, win_ref, wo_ref, wpool_ref,
                 mods_ref, modp_ref, winb_ref, wob_ref, wpoolb_ref, wfold_ref):
    j = pl.program_id(1)
    w = wada_ref[0].astype(BF16)
    for c_ref, o_ref in ((cs_ref, mods_ref), (cp_ref, modp_ref)):
        c = c_ref[...]
        part = jnp.dot((c * jax.nn.sigmoid(c)).astype(BF16), w, preferred_element_type=F32)

        @pl.when(j == 0)
        def _():
            o_ref[0] = part + bada_ref[pl.ds(pl.program_id(0), 1), :]

        @pl.when(j > 0)
        def _():
            o_ref[0] += part
    winb_ref[0] = win_ref[0].astype(BF16)
    wob_ref[0] = wo_ref[0].astype(BF16)
    wpoolb_ref[0] = wpool_ref[0].astype(BF16)
    for g in range(len(POOL_WINDOWS)):
        lo, hi = g * POOL_CG, (g + 1) * POOL_CG
        wfold_ref[0, :, lo:hi] = jnp.dot(win_ref[0, :, lo:hi], wpool_ref[0, g], precision=lax.Precision.HIGHEST,
                                         preferred_element_type=F32).astype(BF16)


def _prep(c_sample, c_prompt, w_ada, b_ada, w_in, w_o, w_pool):
    depth, d, d3 = w_ada.shape
    bs, bp = c_sample.shape[0], c_prompt.shape[0]
    rows = lambda l, j: (l, j, 0)
    whole = lambda l, j: (l, 0, 0)
    return pl.pallas_call(
        _prep_kernel,
        grid=(depth, d // PREP_ROWS),
        in_specs=[
            pl.BlockSpec((bs, PREP_ROWS), lambda l, j: (0, j)),
            pl.BlockSpec((bp, PREP_ROWS), lambda l, j: (0, j)),
            pl.BlockSpec((1, PREP_ROWS, d3), rows),
            _const_spec(b_ada.shape),
            pl.BlockSpec((1, PREP_ROWS, w_in.shape[2]), rows),
            pl.BlockSpec((1, PREP_ROWS, w_o.shape[2]), rows),
            pl.BlockSpec((1,) + w_pool.shape[1:], lambda l, j: (l, 0, 0, 0)),
        ],
        out_specs=[
            pl.BlockSpec((1, bs, d3), whole),
            pl.BlockSpec((1, bp, d3), whole),
            pl.BlockSpec((1, PREP_ROWS, w_in.shape[2]), rows),
            pl.BlockSpec((1, PREP_ROWS, w_o.shape[2]), rows),
            pl.BlockSpec((1,) + w_pool.shape[1:], lambda l, j: (l, 0, 0, 0)),
            pl.BlockSpec((1, PREP_ROWS, D_POOL), rows),
        ],
        out_shape=[
            jax.ShapeDtypeStruct((depth, bs, d3), F32),
            jax.ShapeDtypeStruct((depth, bp, d3), F32),
            jax.ShapeDtypeStruct(w_in.shape, BF16),
            jax.ShapeDtypeStruct(w_o.shape, BF16),
            jax.ShapeDtypeStruct(w_pool.shape, BF16),
            jax.ShapeDtypeStruct((depth, d, D_POOL), BF16),
        ],
        compiler_params=pltpu.CompilerParams(
            dimension_semantics=("arbitrary", "arbitrary"), vmem_limit_bytes=VMEM_LIMIT),
        name="prep",
    )(c_sample, c_prompt, w_ada, b_ada, w_in, w_o, w_pool)


def _rope(x, cs, sn):
    return x * cs + pltpu.roll(x, RET_DK // 2, x.ndim - 1) * sn


def _rms(x):
    return x * lax.rsqrt(jnp.mean(x * x, axis=-1, keepdims=True) + EPS)


def _prompt_kernel(x_ref, mod_ref, gpre_ref, gpost_ref, win_ref, wfold_ref, pscale_ref, wo_ref,
                   cs_ref, sn_ref, dmask_ref, qdec_ref, kdec_ref, gc_ref,
                   xo_ref, s_ref, tail_ref,
                   hb_ref, ubuf_ref, q_ref, qd_ref, k_ref, kd_ref, v_ref, sc_ref, sprev_ref, mix_ref, sg_ref,
                   *, tile, layer):
    t = pl.program_id(1)
    n_chunks = tile // RET_CHUNK
    blocks = [(r0, r0 + PROMPT_BLOCK) for r0 in range(0, tile, PROMPT_BLOCK)]
    o1, o2, o3, o4 = D_POOL, D_POOL + D_RET, D_POOL + 2 * D_RET, D_POOL + 3 * D_RET
    heads = [slice(hh * RET_DK, (hh + 1) * RET_DK) for hh in range(RET_HEADS)]
    chunks = [slice(c * RET_CHUNK, (c + 1) * RET_CHUNK) for c in range(n_chunks)]

    @pl.when(t == 0)
    def _():
        s_ref[...] = jnp.zeros_like(s_ref)
        ubuf_ref[0:HIST, :] = jnp.zeros((HIST, D_POOL), F32)

    m = mod_ref[pl.ds(pl.program_id(0), 1), :]
    shift, scl, gres = m[:, 0:D_MODEL], m[:, D_MODEL:2 * D_MODEL], m[:, 2 * D_MODEL:]
    pre_gain = gpre_ref[layer:layer + 1, :] * (1.0 + scl)
    post_gain = gres * gpost_ref[layer:layer + 1, :]

    def proj(r0, r1, c0, c1):
        return jnp.dot(hb_ref[r0:r1, :], win_ref[:, c0:c1], preferred_element_type=F32)

    zqk, kvs = {}, {}

    def pre_norm(j):
        r0, r1 = blocks[j]
        hb_ref[r0:r1, :] = (_rms(x_ref[0, r0:r1, :]) * pre_gain + shift).astype(BF16)

    def project(j):
        r0, r1 = blocks[j]
        ubuf_ref[HIST + r0:HIST + r1, :] = jnp.dot(hb_ref[r0:r1, :], wfold_ref[...], preferred_element_type=F32)
        zqk[j] = (proj(r0, r1, o1, o2), proj(r0, r1, o2, o3))
        v_ref[r0:r1, :] = proj(r0, r1, o3, o4).astype(BF16)

    def mixer_inputs(j):
        r0, r1 = blocks[j]
        row = lax.broadcasted_iota(jnp.int32, (r1 - r0, 1), 0) + (t * tile + r0)
        for g, w in enumerate(POOL_WINDOWS):
            lo, hi = g * POOL_CG, (g + 1) * POOL_CG
            u_ext = ubuf_ref[r0:r1 + HIST, lo:hi]
            acc = u_ext
            k = 1
            while k < w:
                acc = acc + pltpu.roll(acc, k, 0)
                k *= 2
            cnt = jnp.minimum(row + 1, w).astype(F32)
            mix_ref[r0:r1, lo:hi] = (acc[HIST:] / cnt - u_ext[HIST:]) * pscale_ref[layer:layer + 1, lo:hi]
        zq, zk = zqk.pop(j)
        cs, sn = cs_ref[r0:r1, :], sn_ref[r0:r1, :]
        for sl in heads:
            qr = _rope(zq[:, sl], cs, sn)
            kr = _rope(zk[:, sl], cs, sn) * K_SCALE
            q_ref[r0:r1, sl] = qr.astype(BF16)
            k_ref[r0:r1, sl] = kr.astype(BF16)
            for c in range(r0 // RET_CHUNK, r1 // RET_CHUNK):
                rs, ls = chunks[c], slice(c * RET_CHUNK - r0, (c + 1) * RET_CHUNK - r0)
                qd_ref[rs, sl] = (qr[ls] * qdec_ref[:, sl]).astype(BF16)
                kd_ref[rs, sl] = (kr[ls] * kdec_ref[:, sl]).astype(BF16)

    def small_matmuls(j):
        r0, r1 = blocks[j]
        for c in range(r0 // RET_CHUNK, r1 // RET_CHUNK):
            rs = chunks[c]
            for hh, sl in enumerate(heads):
                scores = lax.dot_general(q_ref[rs, sl], k_ref[rs, sl], (((1,), (1,)), ((), ())),
                                         preferred_element_type=F32)
                sc_ref[rs, sl] = (scores * dmask_ref[hh]).astype(BF16)
                kvs[c, hh] = lax.dot_general(kd_ref[rs, sl], v_ref[rs, sl], (((0,), (0,)), ((), ())),
                                             preferred_element_type=F32)

    def gate(j):
        r0, r1 = blocks[j]
        zg = proj(r0, r1, o4, D_IN)
        sg_ref[r0:r1, :] = zg * jax.nn.sigmoid(zg)

    def state_recurrence():
        for hh, sl in enumerate(heads):
            s = s_ref[0, hh]
            for c in range(n_chunks):
                sprev_ref[c * RET_HEADS + hh] = s.astype(BF16)
                s = gc_ref[0:1, sl] * s + kvs.pop((c, hh))
            s_ref[0, hh] = s

    def retention_out(j):
        r0, r1 = blocks[j]
        for c in range(r0 // RET_CHUNK, r1 // RET_CHUNK):
            rs = chunks[c]
            for hh, sl in enumerate(heads):
                lhs = jnp.concatenate([sc_ref[rs, sl], qd_ref[rs, sl]], axis=1)
                rhs = jnp.concatenate([v_ref[rs, sl], sprev_ref[c * RET_HEADS + hh]], axis=0)
                o_h = jnp.dot(lhs, rhs, preferred_element_type=F32)
                mix_ref[rs, D_POOL + hh * RET_DK:D_POOL + (hh + 1) * RET_DK] = _rms(o_h)

    def out_proj(j):
        r0, r1 = blocks[j]
        mixb = (mix_ref[r0:r1, :] * sg_ref[r0:r1, :]).astype(BF16)
        y = jnp.dot(mixb, wo_ref[...], preferred_element_type=F32)
        xo_ref[0, r0:r1, :] = x_ref[0, r0:r1, :] + _rms(y) * post_gain

    n = len(blocks)
    pre_norm(0)
    project(0)
    for j in range(n):
        if j + 1 < n:
            pre_norm(j + 1)
        mixer_inputs(j)
        if j + 1 < n:
            project(j + 1)
        else:
            ubuf_ref[0:HIST, :] = ubuf_ref[tile:tile + HIST, :]
            gate(0)
        small_matmuls(j)
    gate(1)
    state_recurrence()

    gates_left = list(range(2, n))
    for j in range(n):
        retention_out(j)
        if gates_left:
            gate(gates_left.pop(0))
        if j >= 1:
            out_proj(j - 1)
    out_proj(n - 1)
    tail_ref[0] = hb_ref[tile - HIST:tile, :]


def _prompt_layer(l, x, mod, g_pre, g_post, w_in, w_fold, pool_scale, w_o, tabs, stacked):
    depth = w_in.shape[0]
    b, seq, d = x.shape
    tile = PROMPT_TILE
    cs, sn, dmask, qdec, kdec, gc = tabs
    layer = lambda *_: (l, 0, 0)
    n_in = 14
    carried = tuple(stacked)

    def kern(*refs):
        _prompt_kernel(*refs[:n_in], *refs[n_in + len(carried):], tile=tile, layer=l)

    return pl.pallas_call(
        kern,
        grid=(b, seq // tile),
        in_specs=[
            pl.BlockSpec((1, tile, d), lambda i, t: (i, t, 0)),
            pl.BlockSpec((None, b, 3 * d), layer),
            _const_spec(g_pre.shape),
            _const_spec(g_post.shape),
            pl.BlockSpec((None, d, D_IN), layer, pipeline_mode=pl.Buffered(1)),
            pl.BlockSpec((None, d, D_POOL), layer, pipeline_mode=pl.Buffered(1)),
            _const_spec(pool_scale.shape),
            pl.BlockSpec((None, d, d), layer, pipeline_mode=pl.Buffered(1)),
            pl.BlockSpec((tile, RET_DK), lambda i, t: (t, 0)),
            pl.BlockSpec((tile, RET_DK), lambda i, t: (t, 0)),
            _const_spec(dmask.shape), _const_spec(qdec.shape), _const_spec(kdec.shape), _const_spec(gc.shape),
        ] + [pl.BlockSpec(memory_space=pl.ANY)] * len(carried),
        out_specs=[
            pl.BlockSpec((1, tile, d), lambda i, t: (i, t, 0)),
            pl.BlockSpec((None, 1, RET_HEADS, RET_DK, RET_DK), lambda i, t: (l, i, 0, 0, 0)),
            pl.BlockSpec((None, 1, HIST, d), lambda i, t: (l, i, 0, 0)),
        ],
        out_shape=[
            jax.ShapeDtypeStruct((b, seq, d), F32),
            jax.ShapeDtypeStruct((depth, b, RET_HEADS, RET_DK, RET_DK), F32),
            jax.ShapeDtypeStruct((depth, b, HIST, d), BF16),
        ],
        input_output_aliases={n_in + i: 1 + i for i in range(len(carried))},
        scratch_shapes=[
            pltpu.VMEM((tile, d), BF16),
            pltpu.VMEM((HIST + tile, D_POOL), F32),
            pltpu.VMEM((tile, D_RET), BF16),
            pltpu.VMEM((tile, D_RET), BF16),
            pltpu.VMEM((tile, D_RET), BF16),
            pltpu.VMEM((tile, D_RET), BF16),
            pltpu.VMEM((tile, D_RET), BF16),
            pltpu.VMEM((tile, D_RET), BF16),
            pltpu.VMEM((tile // RET_CHUNK * RET_HEADS, RET_DK, RET_DK), BF16),
            pltpu.VMEM((tile, d), F32),
            pltpu.VMEM((tile, d), F32),
        ],
        compiler_params=pltpu.CompilerParams(
            dimension_semantics=("arbitrary", "arbitrary"), vmem_limit_bytes=VMEM_LIMIT),
        name=f"prompt_layer{l}",
    )(x, mod, g_pre, g_post, w_in, w_fold, pool_scale, w_o, cs, sn, dmask, qdec, kdec, gc, *carried)


def _pool_state_kernel(tail_ref, wu_ref, o_ref, usc_ref):
    b, rows, d = tail_ref.shape
    pool_buf = o_ref.shape[0]
    u = jnp.dot(tail_ref[...].reshape(b * rows, d), wu_ref[...], preferred_element_type=F32)
    for g in range(len(POOL_WINDOWS)):
        lo, hi = g * POOL_CG, (g + 1) * POOL_CG
        usc_ref[g] = u[:, lo:hi]
        for r in range(pool_buf):
            o_ref[r, :, lo:hi] = usc_ref[g, pl.ds(r + rows - pool_buf, b, stride=rows), :]


def _prompt_pool_state(tails, w_in, pool_buf):
    depth, b, rows, d = tails.shape
    return pl.pallas_call(
        _pool_state_kernel,
        grid=(depth,),
        in_specs=[pl.BlockSpec((None, b, rows, d), lambda l: (l, 0, 0, 0)),
                  pl.BlockSpec((None, d, D_POOL), lambda l: (l, 0, 0))],
        out_specs=pl.BlockSpec((None, pool_buf, b, D_POOL), lambda l: (l, 0, 0, 0)),
        out_shape=jax.ShapeDtypeStruct((depth, pool_buf, b, D_POOL), F32),
        scratch_shapes=[pltpu.VMEM((len(POOL_WINDOWS), b * rows, POOL_CG), F32)],
        compiler_params=pltpu.CompilerParams(dimension_semantics=("arbitrary",)),
        name="prompt_pool_state",
    )(tails, w_in)


def _sample_kernel(x_ref, mod_ref, gpre_ref, gpost_ref, win_ref, wpool_ref, pscale_ref, wo_ref,
                   cs_ref, sn_ref, dmask_ref, qdec_ref, kdec_ref, gc_ref, s0_ref, hist_ref,
                   xo_ref, s_ref, nb_ref,
                   usc_ref, mix_ref, sg_ref, *, bt, dec_seq):
    n = bt * dec_seq
    pool_buf = hist_ref.shape[0]
    rows = pl.ds(pl.multiple_of(pl.program_id(1) * bt, bt), bt)

    @pl.when(pl.program_id(0) == 0)
    def _():
        xo_ref[rows] = x_ref[...]

    m = mod_ref[...][:, None, :]
    shift, scl, gres = m[:, :, 0:D_MODEL], m[:, :, D_MODEL:2 * D_MODEL], m[:, :, 2 * D_MODEL:]

    x3 = xo_ref[rows]
    layer = pl.ds(pl.program_id(0), 1)
    h3 = _rms(x3) * (gpre_ref[layer, :][None] * (1.0 + scl)) + shift
    hb = h3.reshape(n, D_MODEL).astype(BF16)

    o1, o2, o3, o4 = D_POOL, D_POOL + D_RET, D_POOL + 2 * D_RET, D_POOL + 3 * D_RET
    heads = [slice(hh * RET_DK, (hh + 1) * RET_DK) for hh in range(RET_HEADS)]
    u = jnp.dot(hb, win_ref[:, 0:o1], preferred_element_type=F32)
    zq = jnp.dot(hb, win_ref[:, o1:o2], preferred_element_type=F32).reshape(bt, dec_seq, D_RET)
    zk = jnp.dot(hb, win_ref[:, o2:o3], preferred_element_type=F32).reshape(bt, dec_seq, D_RET)
    zv = jnp.dot(hb, win_ref[:, o3:o4], preferred_element_type=F32).reshape(bt, dec_seq, D_RET)
    zg = jnp.dot(hb, win_ref[:, o4:], preferred_element_type=F32)

    pooled = []
    for g, w in enumerate(POOL_WINDOWS):
        lo, hi = g * POOL_CG, (g + 1) * POOL_CG
        usc_ref[g] = u[:, lo:hi]
        rows_g = [hist_ref[r, :, lo:hi] for r in range(pool_buf)]
        rows_g += [usc_ref[g, pl.ds(i, bt, stride=dec_seq), :] for i in range(dec_seq)]
        for i in range(dec_seq):
            cur = pool_buf + i
            acc = rows_g[cur]
            for k in range(1, w):
                acc = acc + rows_g[cur - k]
            usc_ref[g, pl.ds(i, bt, stride=dec_seq), :] = acc / float(w) - rows_g[cur]
        pooled.append(usc_ref[g].astype(BF16))
        for r in range(pool_buf):
            nb_ref[r, :, lo:hi] = rows_g[dec_seq + r]

    cs, sn = cs_ref[...][None], sn_ref[...][None]
    qr = [_rope(zq[:, :, sl], cs, sn) for sl in heads]
    kr = [_rope(zk[:, :, sl], cs, sn) * K_SCALE for sl in heads]
    sg_ref[...] = zg * jax.nn.sigmoid(zg)

    for g in range(len(POOL_WINDOWS)):
        lo, hi = g * POOL_CG, (g + 1) * POOL_CG
        mix_ref[:, lo:hi] = jnp.dot(pooled[g], wpool_ref[g], preferred_element_type=F32) * pscale_ref[layer, :][:, lo:hi]
    scores = [jnp.einsum("bid,bjd->bij", qr[hh], kr[hh], preferred_element_type=F32) * dmask_ref[hh][None]
              for hh in range(RET_HEADS)]
    kv, cross = [], []
    for hh, sl in enumerate(heads):
        kv.append(jnp.einsum("bjd,bje->bde", (kr[hh] * kdec_ref[:, sl][None]).astype(BF16),
                             zv[:, :, sl].astype(BF16), preferred_element_type=F32))
        cross.append(jnp.einsum("bid,bde->bie", qr[hh] * qdec_ref[:, sl][None], s0_ref[:, hh],
                                preferred_element_type=F32))
    intra = [jnp.einsum("bij,bje->bie", scores[hh], zv[:, :, sl], preferred_element_type=F32)
             for hh, sl in enumerate(heads)]
    for hh, sl in enumerate(heads):
        s_ref[:, hh] = gc_ref[0:1, sl][None] * s0_ref[:, hh] + kv[hh]
        mix_ref[:, D_POOL + hh * RET_DK:D_POOL + (hh + 1) * RET_DK] = _rms(intra[hh] + cross[hh]).reshape(n, RET_DK)

    mixb = (mix_ref[...] * sg_ref[...]).astype(BF16)
    y = jnp.dot(mixb, wo_ref[...], preferred_element_type=F32)
    xo_ref[rows] = x3 + (gres * gpost_ref[layer, :][None]) * _rms(y).reshape(bt, dec_seq, D_MODEL)


def _sample_layers(x, mod, g_pre, g_post, w_in, w_pool, pool_scale, w_o, tabs, state_ret, state_pool):
    b, dec_seq, d = x.shape
    depth, pool_buf, _, _ = state_pool.shape
    bt = SAMPLE_BT
    cs, sn, dmask, qdec, kdec, gc = tabs
    layer = lambda l, i: (l, 0, 0)
    kern = functools.partial(_sample_kernel, bt=bt, dec_seq=dec_seq)
    return pl.pallas_call(
        kern,
        grid=(depth, b // bt),
        in_specs=[
            pl.BlockSpec((bt, dec_seq, d), lambda l, i: (i, 0, 0)),
            pl.BlockSpec((None, bt, 3 * d), lambda l, i: (l, i, 0)),
            _const_spec(g_pre.shape),
            _const_spec(g_post.shape),
            pl.BlockSpec((None, d, D_IN), layer, pipeline_mode=pl.Buffered(1)),
            pl.BlockSpec((None, len(POOL_WINDOWS), POOL_CG, POOL_CG), lambda l, i: (l, 0, 0, 0)),
            _const_spec(pool_scale.shape),
            pl.BlockSpec((None, d, d), layer, pipeline_mode=pl.Buffered(1)),
            _const_spec(cs.shape), _const_spec(sn.shape),
            _const_spec(dmask.shape), _const_spec(qdec.shape), _const_spec(kdec.shape), _const_spec(gc.shape),
            pl.BlockSpec((None, bt, RET_HEADS, RET_DK, RET_DK), lambda l, i: (l, i, 0, 0, 0)),
            pl.BlockSpec((None, pool_buf, bt, D_POOL), lambda l, i: (l, 0, i, 0)),
        ],
        out_specs=[
            pl.BlockSpec((b, dec_seq, d), lambda l, i: (0, 0, 0)),
            pl.BlockSpec((None, bt, RET_HEADS, RET_DK, RET_DK), lambda l, i: (l, i, 0, 0, 0)),
            pl.BlockSpec((None, pool_buf, bt, D_POOL), lambda l, i: (l, 0, i, 0)),
        ],
        out_shape=[
            jax.ShapeDtypeStruct((b, dec_seq, d), F32),
            jax.ShapeDtypeStruct((depth, b, RET_HEADS, RET_DK, RET_DK), F32),
            jax.ShapeDtypeStruct((depth, pool_buf, b, D_POOL), F32),
        ],
        scratch_shapes=[
            pltpu.VMEM((len(POOL_WINDOWS), bt * dec_seq, POOL_CG), F32),
            pltpu.VMEM((bt * dec_seq, d), F32),
            pltpu.VMEM((bt * dec_seq, d), F32),
        ],
        compiler_params=pltpu.CompilerParams(
            dimension_semantics=("arbitrary", "arbitrary"), vmem_limit_bytes=VMEM_LIMIT),
        name="sample_layers",
    )(x, mod, g_pre, g_post, w_in, w_pool, pool_scale, w_o, cs, sn, dmask, qdec, kdec, gc, state_ret, state_pool)


def kernel(x_prompt, x_sample, c_prompt, c_sample, state_ret, state_pool, w_ada, b_ada,
           g_pre, g_post, w_in, w_pool, pool_scale, w_o):
    depth = w_in.shape[0]
    bp, seq, d = x_prompt.shape
    bs, dec_seq, _ = x_sample.shape
    assert d == D_MODEL and w_in.shape[2] == D_IN and state_pool.shape[2] == HIST - 1
    assert seq % PROMPT_TILE == 0 and bs % SAMPLE_BT == 0
    assert dec_seq == 8 and dec_seq % RET_CHUNK != 0

    (csp, snp, css, sns, dmp, qdp, kdp, gcp, dms, qds, kds, gcs, *stacked) = _make_tables(seq, dec_seq, PAST_LEN,
                                                                                      depth, bp)
    tabs_p = (csp, snp, dmp, qdp, kdp, gcp)
    tabs_s = (css, sns, dms, qds, kds, gcs)

    mod_s, mod_p, w_in_b, w_o_b, w_pool_b, w_fold_b = _prep(c_sample, c_prompt, w_ada, b_ada, w_in, w_o, w_pool)
    hs, ret_s, pool_s = _sample_layers(x_sample, mod_s, g_pre, g_post, w_in_b, w_pool_b, pool_scale, w_o_b,
                                       tabs_s, state_ret, jnp.transpose(state_pool, (0, 2, 1, 3)))
    pool_s = jnp.transpose(pool_s, (0, 2, 1, 3))
    hp = x_prompt
    for l in range(depth):
        hp, *stacked = _prompt_layer(l, hp, mod_p, g_pre, g_post, w_in_b, w_fold_b, pool_scale, w_o_b, tabs_p,
                                     stacked)
    ret_p, tails = stacked
    pool_p = jnp.transpose(_prompt_pool_state(tails, w_in_b, state_pool.shape[2]), (0, 2, 1, 3))
    return (hp, hs, ret_p, pool_p, ret_s, pool_s)
```

```python
import functools

import jax
import jax.numpy as jnp
from jax import lax
from jax.experimental import pallas as pl
from jax.experimental.pallas import tpu as pltpu

F32 = jnp.float32
BF16 = jnp.bfloat16

D_MODEL = 1024
D_POOL = 512
D_RET = 512
POOL_WINDOWS = (2, 4, 8, 16)
POOL_CG = 128
HIST = 16
RET_HEADS = 4
RET_DK = 128
RET_CHUNK = 128
ROPE_BASE = 10000.0
PAST_LEN = 16384
ROPE_ROWS = 128
D_IN = D_POOL + 3 * D_RET + D_MODEL
EPS = 1e-6
K_SCALE = RET_DK ** -0.5

PROMPT_TILE = 1024
PROMPT_BLOCK = 256
SAMPLE_BT = 32
PREP_ROWS = 256
VMEM_LIMIT = 62 * 1024 * 1024


def _const_spec(shape):
    return pl.BlockSpec(shape, lambda *_: (0,) * len(shape))


def _decay_tables(lg_lane, lg_row, c, dmask_ref, qdec_ref, kdec_ref, gc_ref):
    idx = lax.broadcasted_iota(jnp.int32, (c, D_RET), 0).astype(F32)
    qdec_ref[...] = jnp.exp((idx + 1.0) * lg_lane)
    kdec_ref[...] = jnp.exp((c - 1.0 - idx) * lg_lane)
    gc_ref[...] = jnp.exp(jnp.full((8, D_RET), float(c), F32) * lg_lane)
    ii = lax.broadcasted_iota(jnp.int32, (c, c), 0)
    jj = lax.broadcasted_iota(jnp.int32, (c, c), 1)
    diff = ii - jj
    for h in range(RET_HEADS):
        dec = jnp.exp(jnp.maximum(diff, 0).astype(F32) * lg_row[h])
        dmask_ref[h] = jnp.where(diff >= 0, dec, 0.0)


def _tables_kernel(inv_ref, csp_ref, snp_ref, css_ref, sns_ref,
                   dmp_ref, qdp_ref, kdp_ref, gcp_ref,
                   dms_ref, qds_ref, kds_ref, gcs_ref, ret0_ref, tail0_ref, *, seq, dec_seq, past_len):
    ret0_ref[...] = jnp.zeros_like(ret0_ref)
    tail0_ref[...] = jnp.zeros_like(tail0_ref)

    inv = inv_ref[...]
    lane = lax.broadcasted_iota(jnp.int32, (1, RET_DK), 1)
    sign = jnp.where(lane < RET_DK // 2, -1.0, 1.0).astype(F32)

    def angles(n, start, step):
        pos = (lax.broadcasted_iota(jnp.int32, (n, RET_DK), 0) * step + start).astype(F32)
        return pos * inv

    ang = angles(dec_seq, past_len, 1)
    css_ref[...] = jnp.cos(ang)
    sns_ref[...] = jnp.sin(ang) * sign
    lo = angles(ROPE_ROWS, 0, 1)
    hi = angles(seq // ROPE_ROWS, 0, ROPE_ROWS)
    c_lo, s_lo, c_hi, s_hi = jnp.cos(lo), jnp.sin(lo), jnp.cos(hi), jnp.sin(hi)
    for r in range(seq // ROPE_ROWS):
        rows = slice(r * ROPE_ROWS, (r + 1) * ROPE_ROWS)
        csp_ref[rows, :] = c_hi[r:r + 1] * c_lo - s_hi[r:r + 1] * s_lo
        snp_ref[rows, :] = (s_hi[r:r + 1] * c_lo + c_hi[r:r + 1] * s_lo) * sign

    head_lane = (lax.broadcasted_iota(jnp.int32, (1, D_RET), 1) // RET_DK).astype(F32)
    lg_lane = jnp.log(1.0 - jnp.exp2(-5.0 - head_lane))
    for c, refs in ((RET_CHUNK, (dmp_ref, qdp_ref, kdp_ref, gcp_ref)),
                    (dec_seq, (dms_ref, qds_ref, kds_ref, gcs_ref))):
        lg_row = [jnp.log(1.0 - jnp.exp2(jnp.full((1, c), -5.0 - h, F32))) for h in range(RET_HEADS)]
        _decay_tables(lg_lane, lg_row, c, *refs)


def _make_tables(seq, dec_seq, past_len, depth, bp):
    half = RET_DK // 2
    inv = 1.0 / (ROPE_BASE ** (jnp.arange(half, dtype=F32) / half))
    inv2 = jnp.concatenate([inv, inv])[None, :]
    c, cs = RET_CHUNK, dec_seq
    out_shape = (
        jax.ShapeDtypeStruct((seq, RET_DK), F32), jax.ShapeDtypeStruct((seq, RET_DK), F32),
        jax.ShapeDtypeStruct((dec_seq, RET_DK), F32), jax.ShapeDtypeStruct((dec_seq, RET_DK), F32),
        jax.ShapeDtypeStruct((RET_HEADS, c, c), F32), jax.ShapeDtypeStruct((c, D_RET), F32),
        jax.ShapeDtypeStruct((c, D_RET), F32), jax.ShapeDtypeStruct((8, D_RET), F32),
        jax.ShapeDtypeStruct((RET_HEADS, cs, cs), F32), jax.ShapeDtypeStruct((cs, D_RET), F32),
        jax.ShapeDtypeStruct((cs, D_RET), F32), jax.ShapeDtypeStruct((8, D_RET), F32),
        jax.ShapeDtypeStruct((depth, bp, RET_HEADS, RET_DK, RET_DK), F32),
        jax.ShapeDtypeStruct((depth, bp, HIST, D_MODEL), BF16),
    )
    return pl.pallas_call(
        functools.partial(_tables_kernel, seq=seq, dec_seq=dec_seq, past_len=past_len),
        out_shape=out_shape,
        name="tables",
    )(inv2)


def _prep_kernel(cs_ref, cp_ref, wada_ref, bada_ref, win_ref, wo_ref, wpool_ref,
                 mods_ref, modp_ref, winb_ref, wob_ref, wpoolb_ref, wfold_ref):
    j = pl.program_id(1)
    w = wada_ref[0].astype(BF16)
    for c_ref, o_ref in ((cs_ref, mods_ref), (cp_ref, modp_ref)):
        c = c_ref[...]
        part = jnp.dot((c * jax.nn.sigmoid(c)).astype(BF16), w, preferred_element_type=F32)

        @pl.when(j == 0)
        def _():
            o_ref[0] = part + bada_ref[pl.ds(pl.program_id(0), 1), :]

        @pl.when(j > 0)
        def _():
            o_ref[0] += part
    winb_ref[0] = win_ref[0].astype(BF16)
    wob_ref[0] = wo_ref[0].astype(BF16)
    wpoolb_ref[0] = wpool_ref[0].astype(BF16)
    for g in range(len(POOL_WINDOWS)):
        lo, hi = g * POOL_CG, (g + 1) * POOL_CG
        wfold_ref[0, :, lo:hi] = jnp.dot(win_ref[0, :, lo:hi], wpool_ref[0, g], precision=lax.Precision.HIGHEST,
                                         preferred_element_type=F32).astype(BF16)


def _prep(c_sample, c_prompt, w_ada, b_ada, w_in, w_o, w_pool):
    depth, d, d3 = w_ada.shape
    bs, bp = c_sample.shape[0], c_prompt.shape[0]
    rows = lambda l, j: (l, j, 0)
    whole = lambda l, j: (l, 0, 0)
    return pl.pallas_call(
        _prep_kernel,
        grid=(depth, d // PREP_ROWS),
        in_specs=[
            pl.BlockSpec((bs, PREP_ROWS), lambda l, j: (0, j)),
            pl.BlockSpec((bp, PREP_ROWS), lambda l, j: (0, j)),
            pl.BlockSpec((1, PREP_ROWS, d3), rows),
            _const_spec(b_ada.shape),
            pl.BlockSpec((1, PREP_ROWS, w_in.shape[2]), rows),
            pl.BlockSpec((1, PREP_ROWS, w_o.shape[2]), rows),
            pl.BlockSpec((1,) + w_pool.shape[1:], lambda l, j: (l, 0, 0, 0)),
        ],
        out_specs=[
            pl.BlockSpec((1, bs, d3), whole),
            pl.BlockSpec((1, bp, d3), whole),
            pl.BlockSpec((1, PREP_ROWS, w_in.shape[2]), rows),
            pl.BlockSpec((1, PREP_ROWS, w_o.shape[2]), rows),
            pl.BlockSpec((1,) + w_pool.shape[1:], lambda l, j: (l, 0, 0, 0)),
            pl.BlockSpec((1, PREP_ROWS, D_POOL), rows),
        ],
        out_shape=[
            jax.ShapeDtypeStruct((depth, bs, d3), F32),
            jax.ShapeDtypeStruct((depth, bp, d3), F32),
            jax.ShapeDtypeStruct(w_in.shape, BF16),
            jax.ShapeDtypeStruct(w_o.shape, BF16),
            jax.ShapeDtypeStruct(w_pool.shape, BF16),
            jax.ShapeDtypeStruct((depth, d, D_POOL), BF16),
        ],
        compiler_params=pltpu.CompilerParams(
            dimension_semantics=("arbitrary", "arbitrary"), vmem_limit_bytes=VMEM_LIMIT),
        name="prep",
    )(c_sample, c_prompt, w_ada, b_ada, w_in, w_o, w_pool)


def _rope(x, cs, sn):
    return x * cs + pltpu.roll(x, RET_DK // 2, x.ndim - 1) * sn


def _rms(x):
    return x * lax.rsqrt(jnp.mean(x * x, axis=-1, keepdims=True) + EPS)


def _prompt_kernel(x_ref, mod_ref, gpre_ref, gpost_ref, win_ref, wfold_ref, pscale_ref, wo_ref,
                   cs_ref, sn_ref, dmask_ref, qdec_ref, kdec_ref, gc_ref,
                   xo_ref, s_ref, tail_ref,
                   hb_ref, ubuf_ref, q_ref, qd_ref, k_ref, kd_ref, v_ref, sc_ref, sprev_ref, mix_ref, sg_ref,
                   *, tile, layer):
    t = pl.program_id(1)
    n_chunks = tile // RET_CHUNK
    blocks = [(r0, r0 + PROMPT_BLOCK) for r0 in range(0, tile, PROMPT_BLOCK)]
    o1, o2, o3, o4 = D_POOL, D_POOL + D_RET, D_POOL + 2 * D_RET, D_POOL + 3 * D_RET
    heads = [slice(hh * RET_DK, (hh + 1) * RET_DK) for hh in range(RET_HEADS)]
    chunks = [slice(c * RET_CHUNK, (c + 1) * RET_CHUNK) for c in range(n_chunks)]

    @pl.when(t == 0)
    def _():
        s_ref[...] = jnp.zeros_like(s_ref)
        ubuf_ref[0:HIST, :] = jnp.zeros((HIST, D_POOL), F32)

    m = mod_ref[pl.ds(pl.program_id(0), 1), :]
    shift, scl, gres = m[:, 0:D_MODEL], m[:, D_MODEL:2 * D_MODEL], m[:, 2 * D_MODEL:]
    pre_gain = gpre_ref[layer:layer + 1, :] * (1.0 + scl)
    post_gain = gres * gpost_ref[layer:layer + 1, :]

    def proj(r0, r1, c0, c1):
        return jnp.dot(hb_ref[r0:r1, :], win_ref[:, c0:c1], preferred_element_type=F32)

    zqk, kvs = {}, {}

    def pre_norm(j):
        r0, r1 = blocks[j]
        hb_ref[r0:r1, :] = (_rms(x_ref[0, r0:r1, :]) * pre_gain + shift).astype(BF16)

    def project(j):
        r0, r1 = blocks[j]
        ubuf_ref[HIST + r0:HIST + r1, :] = jnp.dot(hb_ref[r0:r1, :], wfold_ref[...], preferred_element_type=F32)
        zqk[j] = (proj(r0, r1, o1, o2), proj(r0, r1, o2, o3))
        v_ref[r0:r1, :] = proj(r0, r1, o3, o4).astype(BF16)

    def mixer_inputs(j):
        r0, r1 = blocks[j]
        row = lax.broadcasted_iota(jnp.int32, (r1 - r0, 1), 0) + (t * tile + r0)
        for g, w in enumerate(POOL_WINDOWS):
            lo, hi = g * POOL_CG, (g + 1) * POOL_CG
            u_ext = ubuf_ref[r0:r1 + HIST, lo:hi]
            acc = u_ext
            k = 1
            while k < w:
                acc = acc + pltpu.roll(acc, k, 0)
                k *= 2
            cnt = jnp.minimum(row + 1, w).astype(F32)
            mix_ref[r0:r1, lo:hi] = (acc[HIST:] / cnt - u_ext[HIST:]) * pscale_ref[layer:layer + 1, lo:hi]
        zq, zk = zqk.pop(j)
        cs, sn = cs_ref[r0:r1, :], sn_ref[r0:r1, :]
        for sl in heads:
            qr = _rope(zq[:, sl], cs, sn)
            kr = _rope(zk[:, sl], cs, sn) * K_SCALE
            q_ref[r0:r1, sl] = qr.astype(BF16)
            k_ref[r0:r1, sl] = kr.astype(BF16)
            for c in range(r0 // RET_CHUNK, r1 // RET_CHUNK):
                rs, ls = chunks[c], slice(c * RET_CHUNK - r0, (c + 1) * RET_CHUNK - r0)
                qd_ref[rs, sl] = (qr[ls] * qdec_ref[:, sl]).astype(BF16)
                kd_ref[rs, sl] = (kr[ls] * kdec_ref[:, sl]).astype(BF16)

    def small_matmuls(j):
        r0, r1 = blocks[j]
        for c in range(r0 // RET_CHUNK, r1 // RET_CHUNK):
            rs = chunks[c]
            for hh, sl in enumerate(heads):
                scores = lax.dot_general(q_ref[rs, sl], k_ref[rs, sl], (((1,), (1,)), ((), ())),
                                         preferred_element_type=F32)
                sc_ref[rs, sl] = (scores * dmask_ref[hh]).astype(BF16)
                kvs[c, hh] = lax.dot_general(kd_ref[rs, sl], v_ref[rs, sl], (((0,), (0,)), ((), ())),
                                             preferred_element_type=F32)

    def gate(j):
        r0, r1 = blocks[j]
        zg = proj(r0, r1, o4, D_IN)
        sg_ref[r0:r1, :] = zg * jax.nn.sigmoid(zg)

    def state_recurrence():
        for hh, sl in enumerate(heads):
            s = s_ref[0, hh]
            for c in range(n_chunks):
                sprev_ref[c * RET_HEADS + hh] = s.astype(BF16)
                s = gc_ref[0:1, sl] * s + kvs.pop((c, hh))
            s_ref[0, hh] = s

    def retention_out(j):
        r0, r1 = blocks[j]
        for c in range(r0 // RET_CHUNK, r1 // RET_CHUNK):
            rs = chunks[c]
            for hh, sl in enumerate(heads):
                lhs = jnp.concatenate([sc_ref[rs, sl], qd_ref[rs, sl]], axis=1)
                rhs = jnp.concatenate([v_ref[rs, sl], sprev_ref[c * RET_HEADS + hh]], axis=0)
                o_h = jnp.dot(lhs, rhs, preferred_element_type=F32)
                mix_ref[rs, D_POOL + hh * RET_DK:D_POOL + (hh + 1) * RET_DK] = _rms(o_h)

    def out_proj(j):
        r0, r1 = blocks[j]
        mixb = (mix_ref[r0:r1, :] * sg_ref[r0:r1, :]).astype(BF16)
        y = jnp.dot(mixb, wo_ref[...], preferred_element_type=F32)
        xo_ref[0, r0:r1, :] = x_ref[0, r0:r1, :] + _rms(y) * post_gain

    n = len(blocks)
    pre_norm(0)
    project(0)
    for j in range(n):
        if j + 1 < n:
            pre_norm(j + 1)
        mixer_inputs(j)
        if j + 1 < n:
            project(j + 1)
        else:
            ubuf_ref[0:HIST, :] = ubuf_ref[tile:tile + HIST, :]
            gate(0)
        small_matmuls(j)
    gate(1)
    state_recurrence()

    gates_left = list(range(2, n))
    for j in range(n):
        retention_out(j)
        if gates_left:
            gate(gates_left.pop(0))
        if j >= 1:
            out_proj(j - 1)
    out_proj(n - 1)
    tail_ref[0] = hb_ref[tile - HIST:tile, :]


def _prompt_layer(l, x, mod, g_pre, g_post, w_in, w_fold, pool_scale, w_o, tabs, stacked):
    depth = w_in.shape[0]
    b, seq, d = x.shape
    tile = PROMPT_TILE
    cs, sn, dmask, qdec, kdec, gc = tabs
    layer = lambda *_: (l, 0, 0)
    n_in = 14
    carried = tuple(stacked)

    def kern(*refs):
        _prompt_kernel(*refs[:n_in], *refs[n_in + len(carried):], tile=tile, layer=l)

    return pl.pallas_call(
        kern,
        grid=(b, seq // tile),
        in_specs=[
            pl.BlockSpec((1, tile, d), lambda i, t: (i, t, 0)),
            pl.BlockSpec((None, b, 3 * d), layer),
            _const_spec(g_pre.shape),
            _const_spec(g_post.shape),
            pl.BlockSpec((None, d, D_IN), layer, pipeline_mode=pl.Buffered(1)),
            pl.BlockSpec((None, d, D_POOL), layer, pipeline_mode=pl.Buffered(1)),
            _const_spec(pool_scale.shape),
            pl.BlockSpec((None, d, d), layer, pipeline_mode=pl.Buffered(1)),
            pl.BlockSpec((tile, RET_DK), lambda i, t: (t, 0)),
            pl.BlockSpec((tile, RET_DK), lambda i, t: (t, 0)),
            _const_spec(dmask.shape), _const_spec(qdec.shape), _const_spec(kdec.shape), _const_spec(gc.shape),
        ] + [pl.BlockSpec(memory_space=pl.ANY)] * len(carried),
        out_specs=[
            pl.BlockSpec((1, tile, d), lambda i, t: (i, t, 0)),
            pl.BlockSpec((None, 1, RET_HEADS, RET_DK, RET_DK), lambda i, t: (l, i, 0, 0, 0)),
            pl.BlockSpec((None, 1, HIST, d), lambda i, t: (l, i, 0, 0)),
        ],
        out_shape=[
            jax.ShapeDtypeStruct((b, seq, d), F32),
            jax.ShapeDtypeStruct((depth, b, RET_HEADS, RET_DK, RET_DK), F32),
            jax.ShapeDtypeStruct((depth, b, HIST, d), BF16),
        ],
        input_output_aliases={n_in + i: 1 + i for i in range(len(carried))},
        scratch_shapes=[
            pltpu.VMEM((tile, d), BF16),
            pltpu.VMEM((HIST + tile, D_POOL), F32),
            pltpu.VMEM((tile, D_RET), BF16),
            pltpu.VMEM((tile, D_RET), BF16),
            pltpu.VMEM((tile, D_RET), BF16),
            pltpu.VMEM((tile, D_RET), BF16),
            pltpu.VMEM((tile, D_RET), BF16),
            pltpu.VMEM((tile, D_RET), BF16),
            pltpu.VMEM((tile // RET_CHUNK * RET_HEADS, RET_DK, RET_DK), BF16),
            pltpu.VMEM((tile, d), F32),
            pltpu.VMEM((tile, d), F32),
        ],
        compiler_params=pltpu.CompilerParams(
            dimension_semantics=("arbitrary", "arbitrary"), vmem_limit_bytes=VMEM_LIMIT),
        name=f"prompt_layer{l}",
    )(x, mod, g_pre, g_post, w_in, w_fold, pool_scale, w_o, cs, sn, dmask, qdec, kdec, gc, *carried)


def _pool_state_kernel(tail_ref, wu_ref, o_ref, usc_ref):
    b, rows, d = tail_ref.shape
    pool_buf = o_ref.shape[0]
    u = jnp.dot(tail_ref[...].reshape(b * rows, d), wu_ref[...], preferred_element_type=F32)
    for g in range(len(POOL_WINDOWS)):
        lo, hi = g * POOL_CG, (g + 1) * POOL_CG
        usc_ref[g] = u[:, lo:hi]
        for r in range(pool_buf):
            o_ref[r, :, lo:hi] = usc_ref[g, pl.ds(r + rows - pool_buf, b, stride=rows), :]


def _prompt_pool_state(tails, w_in, pool_buf):
    depth, b, rows, d = tails.shape
    return pl.pallas_call(
        _pool_state_kernel,
        grid=(depth,),
        in_specs=[pl.BlockSpec((None, b, rows, d), lambda l: (l, 0, 0, 0)),
                  pl.BlockSpec((None, d, D_POOL), lambda l: (l, 0, 0))],
        out_specs=pl.BlockSpec((None, pool_buf, b, D_POOL), lambda l: (l, 0, 0, 0)),
        out_shape=jax.ShapeDtypeStruct((depth, pool_buf, b, D_POOL), F32),
        scratch_shapes=[pltpu.VMEM((len(POOL_WINDOWS), b * rows, POOL_CG), F32)],
        compiler_params=pltpu.CompilerParams(dimension_semantics=("arbitrary",)),
        name="prompt_pool_state",
    )(tails, w_in)


def _sample_kernel(x_ref, mod_ref, gpre_ref, gpost_ref, win_ref, wpool_ref, pscale_ref, wo_ref,
                   cs_ref, sn_ref, dmask_ref, qdec_ref, kdec_ref, gc_ref, s0_ref, hist_ref,
                   xo_ref, s_ref, nb_ref,
                   usc_ref, mix_ref, sg_ref, *, bt, dec_seq):
    n = bt * dec_seq
    pool_buf = hist_ref.shape[0]
    rows = pl.ds(pl.multiple_of(pl.program_id(1) * bt, bt), bt)

    @pl.when(pl.program_id(0) == 0)
    def _():
        xo_ref[rows] = x_ref[...]

    m = mod_ref[...][:, None, :]
    shift, scl, gres = m[:, :, 0:D_MODEL], m[:, :, D_MODEL:2 * D_MODEL], m[:, :, 2 * D_MODEL:]

    x3 = xo_ref[rows]
    layer = pl.ds(pl.program_id(0), 1)
    h3 = _rms(x3) * (gpre_ref[layer, :][None] * (1.0 + scl)) + shift
    hb = h3.reshape(n, D_MODEL).astype(BF16)

    o1, o2, o3, o4 = D_POOL, D_POOL + D_RET, D_POOL + 2 * D_RET, D_POOL + 3 * D_RET
    heads = [slice(hh * RET_DK, (hh + 1) * RET_DK) for hh in range(RET_HEADS)]
    u = jnp.dot(hb, win_ref[:, 0:o1], preferred_element_type=F32)
    zq = jnp.dot(hb, win_ref[:, o1:o2], preferred_element_type=F32).reshape(bt, dec_seq, D_RET)
    zk = jnp.dot(hb, win_ref[:, o2:o3], preferred_element_type=F32).reshape(bt, dec_seq, D_RET)
    zv = jnp.dot(hb, win_ref[:, o3:o4], preferred_element_type=F32).reshape(bt, dec_seq, D_RET)
    zg = jnp.dot(hb, win_ref[:, o4:], preferred_element_type=F32)

    pooled = []
    for g, w in enumerate(POOL_WINDOWS):
        lo, hi = g * POOL_CG, (g + 1) * POOL_CG
        usc_ref[g] = u[:, lo:hi]
        rows_g = [hist_ref[r, :, lo:hi] for r in range(pool_buf)]
        rows_g += [usc_ref[g, pl.ds(i, bt, stride=dec_seq), :] for i in range(dec_seq)]
        for i in range(dec_seq):
            cur = pool_buf + i
            acc = rows_g[cur]
            for k in range(1, w):
                acc = acc + rows_g[cur - k]
            usc_ref[g, pl.ds(i, bt, stride=dec_seq), :] = acc / float(w) - rows_g[cur]
        pooled.append(usc_ref[g].astype(BF16))
        for r in range(pool_buf):
            nb_ref[r, :, lo:hi] = rows_g[dec_seq + r]

    cs, sn = cs_ref[...][None], sn_ref[...][None]
    qr = [_rope(zq[:, :, sl], cs, sn) for sl in heads]
    kr = [_rope(zk[:, :, sl], cs, sn) * K_SCALE for sl in heads]
    sg_ref[...] = zg * jax.nn.sigmoid(zg)

    for g in range(len(POOL_WINDOWS)):
        lo, hi = g * POOL_CG, (g + 1) * POOL_CG
        mix_ref[:, lo:hi] = jnp.dot(pooled[g], wpool_ref[g], preferred_element_type=F32) * pscale_ref[layer, :][:, lo:hi]
    scores = [jnp.einsum("bid,bjd->bij", qr[hh], kr[hh], preferred_element_type=F32) * dmask_ref[hh][None]
              for hh in range(RET_HEADS)]
    kv, cross = [], []
    for hh, sl in enumerate(heads):
        kv.append(jnp.einsum("bjd,bje->bde", (kr[hh] * kdec_ref[:, sl][None]).astype(BF16),
                             zv[:, :, sl].astype(BF16), preferred_element_type=F32))
        cross.append(jnp.einsum("bid,bde->bie", qr[hh] * qdec_ref[:, sl][None], s0_ref[:, hh],
                                preferred_element_type=F32))
    intra = [jnp.einsum("bij,bje->bie", scores[hh], zv[:, :, sl], preferred_element_type=F32)
             for hh, sl in enumerate(heads)]
    for hh, sl in enumerate(heads):
        s_ref[:, hh] = gc_ref[0:1, sl][None] * s0_ref[:, hh] + kv[hh]
        mix_ref[:, D_POOL + hh * RET_DK:D_POOL + (hh + 1) * RET_DK] = _rms(intra[hh] + cross[hh]).reshape(n, RET_DK)

    mixb = (mix_ref[...] * sg_ref[...]).astype(BF16)
    y = jnp.dot(mixb, wo_ref[...], preferred_element_type=F32)
    xo_ref[rows] = x3 + (gres * gpost_ref[layer, :][None]) * _rms(y).reshape(bt, dec_seq, D_MODEL)


def _sample_layers(x, mod, g_pre, g_post, w_in, w_pool, pool_scale, w_o, tabs, state_ret, state_pool):
    b, dec_seq, d = x.shape
    depth, pool_buf, _, _ = state_pool.shape
    bt = SAMPLE_BT
    cs, sn, dmask, qdec, kdec, gc = tabs
    layer = lambda l, i: (l, 0, 0)
    kern = functools.partial(_sample_kernel, bt=bt, dec_seq=dec_seq)
    return pl.pallas_call(
        kern,
        grid=(depth, b // bt),
        in_specs=[
            pl.BlockSpec((bt, dec_seq, d), lambda l, i: (i, 0, 0)),
            pl.BlockSpec((None, bt, 3 * d), lambda l, i: (l, i, 0)),
            _const_spec(g_pre.shape),
            _const_spec(g_post.shape),
            pl.BlockSpec((None, d, D_IN), layer, pipeline_mode=pl.Buffered(1)),
            pl.BlockSpec((None, len(POOL_WINDOWS), POOL_CG, POOL_CG), lambda l, i: (l, 0, 0, 0)),
            _const_spec(pool_scale.shape),
            pl.BlockSpec((None, d, d), layer),
            _const_spec(cs.shape), _const_spec(sn.shape),
            _const_spec(dmask.shape), _const_spec(qdec.shape), _const_spec(kdec.shape), _const_spec(gc.shape),
            pl.BlockSpec((None, bt, RET_HEADS, RET_DK, RET_DK), lambda l, i: (l, i, 0, 0, 0)),
            pl.BlockSpec((None, pool_buf, bt, D_POOL), lambda l, i: (l, 0, i, 0)),
        ],
        out_specs=[
            pl.BlockSpec((b, dec_seq, d), lambda l, i: (0, 0, 0)),
            pl.BlockSpec((None, bt, RET_HEADS, RET_DK, RET_DK), lambda l, i: (l, i, 0, 0, 0)),
            pl.BlockSpec((None, pool_buf, bt, D_POOL), lambda l, i: (l, 0, i, 0)),
        ],
        out_shape=[
            jax.ShapeDtypeStruct((b, dec_seq, d), F32),
            jax.ShapeDtypeStruct((depth, b, RET_HEADS, RET_DK, RET_DK), F32),
            jax.ShapeDtypeStruct((depth, pool_buf, b, D_POOL), F32),
        ],
        scratch_shapes=[
            pltpu.VMEM((len(POOL_WINDOWS), bt * dec_seq, POOL_CG), F32),
            pltpu.VMEM((bt * dec_seq, d), F32),
            pltpu.VMEM((bt * dec_seq, d), F32),
        ],
        compiler_params=pltpu.CompilerParams(
            dimension_semantics=("arbitrary", "arbitrary"), vmem_limit_bytes=VMEM_LIMIT),
        name="sample_layers",
    )(x, mod, g_pre, g_post, w_in, w_pool, pool_scale, w_o, cs, sn, dmask, qdec, kdec, gc, state_ret, state_pool)


def kernel(x_prompt, x_sample, c_prompt, c_sample, state_ret, state_pool, w_ada, b_ada,
           g_pre, g_post, w_in, w_pool, pool_scale, w_o):
    depth = w_in.shape[0]
    bp, seq, d = x_prompt.shape
    bs, dec_seq, _ = x_sample.shape
    assert d == D_MODEL and w_in.shape[2] == D_IN and state_pool.shape[2] == HIST - 1
    assert seq % PROMPT_TILE == 0 and bs % SAMPLE_BT == 0
    assert dec_seq == 8 and dec_seq % RET_CHUNK != 0

    (csp, snp, css, sns, dmp, qdp, kdp, gcp, dms, qds, kds, gcs, *stacked) = _make_tables(seq, dec_seq, PAST_LEN,
                                                                                      depth, bp)
    tabs_p = (csp, snp, dmp, qdp, kdp, gcp)
    tabs_s = (css, sns, dms, qds, kds, gcs)

    mod_s, mod_p, w_in_b, w_o_b, w_pool_b, w_fold_b = _prep(c_sample, c_prompt, w_ada, b_ada, w_in, w_o, w_pool)
    hs, ret_s, pool_s = _sample_layers(x_sample, mod_s, g_pre, g_post, w_in_b, w_pool_b, pool_scale, w_o_b,
                                       tabs_s, state_ret, jnp.transpose(state_pool, (0, 2, 1, 3)))
    pool_s = jnp.transpose(pool_s, (0, 2, 1, 3))
    hp = x_prompt
    for l in range(depth):
        hp, *stacked = _prompt_layer(l, hp, mod_p, g_pre, g_post, w_in_b, w_fold_b, pool_scale, w_o_b, tabs_p,
                                     stacked)
    ret_p, tails = stacked
    pool_p = jnp.transpose(_prompt_pool_state(tails, w_in_b, state_pool.shape[2]), (0, 2, 1, 3))
    return (hp, hs, ret_p, pool_p, ret_s, pool_s)
```

```python
import functools

import jax
import jax.numpy as jnp
from jax import lax
from jax.experimental import pallas as pl
from jax.experimental.pallas import tpu as pltpu

F32 = jnp.float32
BF16 = jnp.bfloat16

D_MODEL = 1024
D_POOL = 512
D_RET = 512
POOL_WINDOWS = (2, 4, 8, 16)
POOL_CG = 128
HIST = 16
RET_HEADS = 4
RET_DK = 128
RET_CHUNK = 128
ROPE_BASE = 10000.0
PAST_LEN = 16384
ROPE_ROWS = 128
D_IN = D_POOL + 3 * D_RET + D_MODEL
EPS = 1e-6
K_SCALE = RET_DK ** -0.5

PROMPT_TILE = 1024
PROMPT_BLOCK = 256
SAMPLE_BT = 32
PREP_ROWS = 512
VMEM_LIMIT = 60 * 1024 * 1024


def _const_spec(shape):
    return pl.BlockSpec(shape, lambda *_: (0,) * len(shape))


def _decay_tables(lg_lane, lg_row, c, dmask_ref, qdec_ref, kdec_ref, gc_ref):
    idx = lax.broadcasted_iota(jnp.int32, (c, D_RET), 0).astype(F32)
    qdec_ref[...] = jnp.exp((idx + 1.0) * lg_lane)
    kdec_ref[...] = jnp.exp((c - 1.0 - idx) * lg_lane)
    gc_ref[...] = jnp.exp(jnp.full((8, D_RET), float(c), F32) * lg_lane)
    ii = lax.broadcasted_iota(jnp.int32, (c, c), 0)
    jj = lax.broadcasted_iota(jnp.int32, (c, c), 1)
    diff = ii - jj
    for h in range(RET_HEADS):
        dec = jnp.exp(jnp.maximum(diff, 0).astype(F32) * lg_row[h])
        dmask_ref[h] = jnp.where(diff >= 0, dec, 0.0)


def _tables_kernel(inv_ref, csp_ref, snp_ref, css_ref, sns_ref,
                   dmp_ref, qdp_ref, kdp_ref, gcp_ref,
                   dms_ref, qds_ref, kds_ref, gcs_ref, ret0_ref, tail0_ref, *, seq, dec_seq, past_len):
    ret0_ref[...] = jnp.zeros_like(ret0_ref)
    tail0_ref[...] = jnp.zeros_like(tail0_ref)

    inv = inv_ref[...]
    lane = lax.broadcasted_iota(jnp.int32, (1, RET_DK), 1)
    sign = jnp.where(lane < RET_DK // 2, -1.0, 1.0).astype(F32)

    def angles(n, start, step):
        pos = (lax.broadcasted_iota(jnp.int32, (n, RET_DK), 0) * step + start).astype(F32)
        return pos * inv

    ang = angles(dec_seq, past_len, 1)
    css_ref[...] = jnp.cos(ang)
    sns_ref[...] = jnp.sin(ang) * sign
    lo = angles(ROPE_ROWS, 0, 1)
    hi = angles(seq // ROPE_ROWS, 0, ROPE_ROWS)
    c_lo, s_lo, c_hi, s_hi = jnp.cos(lo), jnp.sin(lo), jnp.cos(hi), jnp.sin(hi)
    for r in range(seq // ROPE_ROWS):
        rows = slice(r * ROPE_ROWS, (r + 1) * ROPE_ROWS)
        csp_ref[rows, :] = c_hi[r:r + 1] * c_lo - s_hi[r:r + 1] * s_lo
        snp_ref[rows, :] = (s_hi[r:r + 1] * c_lo + c_hi[r:r + 1] * s_lo) * sign

    head_lane = (lax.broadcasted_iota(jnp.int32, (1, D_RET), 1) // RET_DK).astype(F32)
    lg_lane = jnp.log(1.0 - jnp.exp2(-5.0 - head_lane))
    for c, refs in ((RET_CHUNK, (dmp_ref, qdp_ref, kdp_ref, gcp_ref)),
                    (dec_seq, (dms_ref, qds_ref, kds_ref, gcs_ref))):
        lg_row = [jnp.log(1.0 - jnp.exp2(jnp.full((1, c), -5.0 - h, F32))) for h in range(RET_HEADS)]
        _decay_tables(lg_lane, lg_row, c, *refs)


def _make_tables(seq, dec_seq, past_len, depth, bp):
    half = RET_DK // 2
    inv = 1.0 / (ROPE_BASE ** (jnp.arange(half, dtype=F32) / half))
    inv2 = jnp.concatenate([inv, inv])[None, :]
    c, cs = RET_CHUNK, dec_seq
    out_shape = (
        jax.ShapeDtypeStruct((seq, RET_DK), F32), jax.ShapeDtypeStruct((seq, RET_DK), F32),
        jax.ShapeDtypeStruct((dec_seq, RET_DK), F32), jax.ShapeDtypeStruct((dec_seq, RET_DK), F32),
        jax.ShapeDtypeStruct((RET_HEADS, c, c), F32), jax.ShapeDtypeStruct((c, D_RET), F32),
        jax.ShapeDtypeStruct((c, D_RET), F32), jax.ShapeDtypeStruct((8, D_RET), F32),
        jax.ShapeDtypeStruct((RET_HEADS, cs, cs), F32), jax.ShapeDtypeStruct((cs, D_RET), F32),
        jax.ShapeDtypeStruct((cs, D_RET), F32), jax.ShapeDtypeStruct((8, D_RET), F32),
        jax.ShapeDtypeStruct((depth, bp, RET_HEADS, RET_DK, RET_DK), F32),
        jax.ShapeDtypeStruct((depth, bp, HIST, D_MODEL), BF16),
    )
    return pl.pallas_call(
        functools.partial(_tables_kernel, seq=seq, dec_seq=dec_seq, past_len=past_len),
        out_shape=out_shape,
        name="tables",
    )(inv2)


def _prep_kernel(cs_ref, cp_ref, wada_ref, bada_ref, win_ref, wo_ref, wpool_ref,
                 mods_ref, modp_ref, winb_ref, wob_ref, wpoolb_ref, wfold_ref):
    j = pl.program_id(1)
    w = wada_ref[0].astype(BF16)
    for c_ref, o_ref in ((cs_ref, mods_ref), (cp_ref, modp_ref)):
        c = c_ref[...]
        part = jnp.dot((c * jax.nn.sigmoid(c)).astype(BF16), w, preferred_element_type=F32)

        @pl.when(j == 0)
        def _():
            o_ref[0] = part + bada_ref[pl.ds(pl.program_id(0), 1), :]

        @pl.when(j > 0)
        def _():
            o_ref[0] += part
    winb_ref[0] = win_ref[0].astype(BF16)
    wob_ref[0] = wo_ref[0].astype(BF16)
    wpoolb_ref[0] = wpool_ref[0].astype(BF16)
    for g in range(len(POOL_WINDOWS)):
        lo, hi = g * POOL_CG, (g + 1) * POOL_CG
        wfold_ref[0, :, lo:hi] = jnp.dot(win_ref[0, :, lo:hi], wpool_ref[0, g], precision=lax.Precision.HIGHEST,
                                         preferred_element_type=F32).astype(BF16)


def _prep(c_sample, c_prompt, w_ada, b_ada, w_in, w_o, w_pool):
    depth, d, d3 = w_ada.shape
    bs, bp = c_sample.shape[0], c_prompt.shape[0]
    rows = lambda l, j: (l, j, 0)
    whole = lambda l, j: (l, 0, 0)
    return pl.pallas_call(
        _prep_kernel,
        grid=(depth, d // PREP_ROWS),
        in_specs=[
            pl.BlockSpec((bs, PREP_ROWS), lambda l, j: (0, j)),
            pl.BlockSpec((bp, PREP_ROWS), lambda l, j: (0, j)),
            pl.BlockSpec((1, PREP_ROWS, d3), rows),
            _const_spec(b_ada.shape),
            pl.BlockSpec((1, PREP_ROWS, w_in.shape[2]), rows),
            pl.BlockSpec((1, PREP_ROWS, w_o.shape[2]), rows),
            pl.BlockSpec((1,) + w_pool.shape[1:], lambda l, j: (l, 0, 0, 0)),
        ],
        out_specs=[
            pl.BlockSpec((1, bs, d3), whole),
            pl.BlockSpec((1, bp, d3), whole),
            pl.BlockSpec((1, PREP_ROWS, w_in.shape[2]), rows),
            pl.BlockSpec((1, PREP_ROWS, w_o.shape[2]), rows),
            pl.BlockSpec((1,) + w_pool.shape[1:], lambda l, j: (l, 0, 0, 0)),
            pl.BlockSpec((1, PREP_ROWS, D_POOL), rows),
        ],
        out_shape=[
            jax.ShapeDtypeStruct((depth, bs, d3), F32),
            jax.ShapeDtypeStruct((depth, bp, d3), F32),
            jax.ShapeDtypeStruct(w_in.shape, BF16),
            jax.ShapeDtypeStruct(w_o.shape, BF16),
            jax.ShapeDtypeStruct(w_pool.shape, BF16),
            jax.ShapeDtypeStruct((depth, d, D_POOL), BF16),
        ],
        compiler_params=pltpu.CompilerParams(
            dimension_semantics=("arbitrary", "arbitrary"), vmem_limit_bytes=VMEM_LIMIT),
        name="prep",
    )(c_sample, c_prompt, w_ada, b_ada, w_in, w_o, w_pool)


def _rope(x, cs, sn):
    return x * cs + pltpu.roll(x, RET_DK // 2, x.ndim - 1) * sn


def _rms(x):
    return x * lax.rsqrt(jnp.mean(x * x, axis=-1, keepdims=True) + EPS)


def _prompt_kernel(x_ref, mod_ref, gpre_ref, gpost_ref, win_ref, wfold_ref, pscale_ref, wo_ref,
                   cs_ref, sn_ref, dmask_ref, qdec_ref, kdec_ref, gc_ref,
                   xo_ref, s_ref, tail_ref,
                   hb_ref, ubuf_ref, q_ref, qd_ref, k_ref, kd_ref, v_ref, sc_ref, sprev_ref, mix_ref, sg_ref,
                   *, tile, layer):
    t = pl.program_id(1)
    n_chunks = tile // RET_CHUNK
    blocks = [(r0, r0 + PROMPT_BLOCK) for r0 in range(0, tile, PROMPT_BLOCK)]
    o1, o2, o3, o4 = D_POOL, D_POOL + D_RET, D_POOL + 2 * D_RET, D_POOL + 3 * D_RET
    heads = [slice(hh * RET_DK, (hh + 1) * RET_DK) for hh in range(RET_HEADS)]
    chunks = [slice(c * RET_CHUNK, (c + 1) * RET_CHUNK) for c in range(n_chunks)]

    @pl.when(t == 0)
    def _():
        s_ref[...] = jnp.zeros_like(s_ref)
        ubuf_ref[0:HIST, :] = jnp.zeros((HIST, D_POOL), F32)

    m = mod_ref[pl.ds(pl.program_id(0), 1), :]
    shift, scl, gres = m[:, 0:D_MODEL], m[:, D_MODEL:2 * D_MODEL], m[:, 2 * D_MODEL:]
    pre_gain = gpre_ref[layer:layer + 1, :] * (1.0 + scl)
    post_gain = gres * gpost_ref[layer:layer + 1, :]

    def proj(r0, r1, c0, c1):
        return jnp.dot(hb_ref[r0:r1, :], win_ref[:, c0:c1], preferred_element_type=F32)

    zqk, kvs = {}, {}

    def pre_norm(j):
        r0, r1 = blocks[j]
        hb_ref[r0:r1, :] = (_rms(x_ref[0, r0:r1, :]) * pre_gain + shift).astype(BF16)

    def project(j):
        r0, r1 = blocks[j]
        ubuf_ref[HIST + r0:HIST + r1, :] = jnp.dot(hb_ref[r0:r1, :], wfold_ref[...], preferred_element_type=F32)
        zqk[j] = (proj(r0, r1, o1, o2), proj(r0, r1, o2, o3))
        v_ref[r0:r1, :] = proj(r0, r1, o3, o4).astype(BF16)

    def mixer_inputs(j):
        r0, r1 = blocks[j]
        row = lax.broadcasted_iota(jnp.int32, (r1 - r0, 1), 0) + (t * tile + r0)
        for g, w in enumerate(POOL_WINDOWS):
            lo, hi = g * POOL_CG, (g + 1) * POOL_CG
            u_ext = ubuf_ref[r0:r1 + HIST, lo:hi]
            acc = u_ext
            k = 1
            while k < w:
                acc = acc + pltpu.roll(acc, k, 0)
                k *= 2
            cnt = jnp.minimum(row + 1, w).astype(F32)
            mix_ref[r0:r1, lo:hi] = (acc[HIST:] / cnt - u_ext[HIST:]) * pscale_ref[layer:layer + 1, lo:hi]
        zq, zk = zqk.pop(j)
        cs, sn = cs_ref[r0:r1, :], sn_ref[r0:r1, :]
        for sl in heads:
            qr = _rope(zq[:, sl], cs, sn)
            kr = _rope(zk[:, sl], cs, sn) * K_SCALE
            q_ref[r0:r1, sl] = qr.astype(BF16)
            k_ref[r0:r1, sl] = kr.astype(BF16)
            for c in range(r0 // RET_CHUNK, r1 // RET_CHUNK):
                rs, ls = chunks[c], slice(c * RET_CHUNK - r0, (c + 1) * RET_CHUNK - r0)
                qd_ref[rs, sl] = (qr[ls] * qdec_ref[:, sl]).astype(BF16)
                kd_ref[rs, sl] = (kr[ls] * kdec_ref[:, sl]).astype(BF16)

    def small_matmuls(j):
        r0, r1 = blocks[j]
        for c in range(r0 // RET_CHUNK, r1 // RET_CHUNK):
            rs = chunks[c]
            for hh, sl in enumerate(heads):
                scores = lax.dot_general(q_ref[rs, sl], k_ref[rs, sl], (((1,), (1,)), ((), ())),
                                         preferred_element_type=F32)
                sc_ref[rs, sl] = (scores * dmask_ref[hh]).astype(BF16)
                kvs[c, hh] = lax.dot_general(kd_ref[rs, sl], v_ref[rs, sl], (((0,), (0,)), ((), ())),
                                             preferred_element_type=F32)

    def gate(j):
        r0, r1 = blocks[j]
        zg = proj(r0, r1, o4, D_IN)
        sg_ref[r0:r1, :] = zg * jax.nn.sigmoid(zg)

    def state_recurrence():
        for hh, sl in enumerate(heads):
            s = s_ref[0, hh]
            for c in range(n_chunks):
                sprev_ref[c * RET_HEADS + hh] = s.astype(BF16)
                s = gc_ref[0:1, sl] * s + kvs.pop((c, hh))
            s_ref[0, hh] = s

    def retention_out(j):
        r0, r1 = blocks[j]
        for c in range(r0 // RET_CHUNK, r1 // RET_CHUNK):
            rs = chunks[c]
            for hh, sl in enumerate(heads):
                lhs = jnp.concatenate([sc_ref[rs, sl], qd_ref[rs, sl]], axis=1)
                rhs = jnp.concatenate([v_ref[rs, sl], sprev_ref[c * RET_HEADS + hh]], axis=0)
                o_h = jnp.dot(lhs, rhs, preferred_element_type=F32)
                mix_ref[rs, D_POOL + hh * RET_DK:D_POOL + (hh + 1) * RET_DK] = _rms(o_h)

    def out_proj(j):
        r0, r1 = blocks[j]
        mixb = (mix_ref[r0:r1, :] * sg_ref[r0:r1, :]).astype(BF16)
        y = jnp.dot(mixb, wo_ref[...], preferred_element_type=F32)
        xo_ref[0, r0:r1, :] = x_ref[0, r0:r1, :] + _rms(y) * post_gain

    n = len(blocks)
    pre_norm(0)
    project(0)
    for j in range(n):
        if j + 1 < n:
            pre_norm(j + 1)
        mixer_inputs(j)
        if j + 1 < n:
            project(j + 1)
        else:
            ubuf_ref[0:HIST, :] = ubuf_ref[tile:tile + HIST, :]
            gate(0)
        small_matmuls(j)
    gate(1)
    state_recurrence()

    gates_left = list(range(2, n))
    for j in range(n):
        retention_out(j)
        if gates_left:
            gate(gates_left.pop(0))
        if j >= 1:
            out_proj(j - 1)
    out_proj(n - 1)
    tail_ref[0] = hb_ref[tile - HIST:tile, :]


def _prompt_layer(l, x, mod, g_pre, g_post, w_in, w_fold, pool_scale, w_o, tabs, stacked):
    depth = w_in.shape[0]
    b, seq, d = x.shape
    tile = PROMPT_TILE
    cs, sn, dmask, qdec, kdec, gc = tabs
    layer = lambda *_: (l, 0, 0)
    n_in = 14
    carried = tuple(stacked)

    def kern(*refs):
        _prompt_kernel(*refs[:n_in], *refs[n_in + len(carried):], tile=tile, layer=l)

    return pl.pallas_call(
        kern,
        grid=(b, seq // tile),
        in_specs=[
            pl.BlockSpec((1, tile, d), lambda i, t: (i, t, 0)),
            pl.BlockSpec((None, b, 3 * d), layer),
            _const_spec(g_pre.shape),
            _const_spec(g_post.shape),
            pl.BlockSpec((None, d, D_IN), layer, pipeline_mode=pl.Buffered(1)),
            pl.BlockSpec((None, d, D_POOL), layer, pipeline_mode=pl.Buffered(1)),
            _const_spec(pool_scale.shape),
            pl.BlockSpec((None, d, d), layer, pipeline_mode=pl.Buffered(1)),
            pl.BlockSpec((tile, RET_DK), lambda i, t: (t, 0)),
            pl.BlockSpec((tile, RET_DK), lambda i, t: (t, 0)),
            _const_spec(dmask.shape), _const_spec(qdec.shape), _const_spec(kdec.shape), _const_spec(gc.shape),
        ] + [pl.BlockSpec(memory_space=pl.ANY)] * len(carried),
        out_specs=[
            pl.BlockSpec((1, tile, d), lambda i, t: (i, t, 0)),
            pl.BlockSpec((None, 1, RET_HEADS, RET_DK, RET_DK), lambda i, t: (l, i, 0, 0, 0)),
            pl.BlockSpec((None, 1, HIST, d), lambda i, t: (l, i, 0, 0)),
        ],
        out_shape=[
            jax.ShapeDtypeStruct((b, seq, d), F32),
            jax.ShapeDtypeStruct((depth, b, RET_HEADS, RET_DK, RET_DK), F32),
            jax.ShapeDtypeStruct((depth, b, HIST, d), BF16),
        ],
        input_output_aliases={n_in + i: 1 + i for i in range(len(carried))},
        scratch_shapes=[
            pltpu.VMEM((tile, d), BF16),
            pltpu.VMEM((HIST + tile, D_POOL), F32),
            pltpu.VMEM((tile, D_RET), BF16),
            pltpu.VMEM((tile, D_RET), BF16),
            pltpu.VMEM((tile, D_RET), BF16),
            pltpu.VMEM((tile, D_RET), BF16),
            pltpu.VMEM((tile, D_RET), BF16),
            pltpu.VMEM((tile, D_RET), BF16),
            pltpu.VMEM((tile // RET_CHUNK * RET_HEADS, RET_DK, RET_DK), BF16),
            pltpu.VMEM((tile, d), F32),
            pltpu.VMEM((tile, d), F32),
        ],
        compiler_params=pltpu.CompilerParams(
            dimension_semantics=("arbitrary", "arbitrary"), vmem_limit_bytes=VMEM_LIMIT),
        name=f"prompt_layer{l}",
    )(x, mod, g_pre, g_post, w_in, w_fold, pool_scale, w_o, cs, sn, dmask, qdec, kdec, gc, *carried)


def _pool_state_kernel(tail_ref, wu_ref, o_ref, usc_ref):
    b, rows, d = tail_ref.shape
    pool_buf = o_ref.shape[0]
    u = jnp.dot(tail_ref[...].reshape(b * rows, d), wu_ref[...], preferred_element_type=F32)
    for g in range(len(POOL_WINDOWS)):
        lo, hi = g * POOL_CG, (g + 1) * POOL_CG
        usc_ref[g] = u[:, lo:hi]
        for r in range(pool_buf):
            o_ref[r, :, lo:hi] = usc_ref[g, pl.ds(r + rows - pool_buf, b, stride=rows), :]


def _prompt_pool_state(tails, w_in, pool_buf):
    depth, b, rows, d = tails.shape
    return pl.pallas_call(
        _pool_state_kernel,
        grid=(depth,),
        in_specs=[pl.BlockSpec((None, b, rows, d), lambda l: (l, 0, 0, 0)),
                  pl.BlockSpec((None, d, D_POOL), lambda l: (l, 0, 0))],
        out_specs=pl.BlockSpec((None, pool_buf, b, D_POOL), lambda l: (l, 0, 0, 0)),
        out_shape=jax.ShapeDtypeStruct((depth, pool_buf, b, D_POOL), F32),
        scratch_shapes=[pltpu.VMEM((len(POOL_WINDOWS), b * rows, POOL_CG), F32)],
        compiler_params=pltpu.CompilerParams(dimension_semantics=("arbitrary",)),
        name="prompt_pool_state",
    )(tails, w_in)


def _sample_kernel(x_ref, mod_ref, gpre_ref, gpost_ref, win_ref, wpool_ref, pscale_ref, wo_ref,
                   cs_ref, sn_ref, dmask_ref, qdec_ref, kdec_ref, gc_ref, s0_ref, hist_ref,
                   xo_ref, s_ref, nb_ref,
                   usc_ref, mix_ref, sg_ref, *, bt, dec_seq):
    n = bt * dec_seq
    pool_buf = hist_ref.shape[0]
    rows = pl.ds(pl.multiple_of(pl.program_id(1) * bt, bt), bt)

    @pl.when(pl.program_id(0) == 0)
    def _():
        xo_ref[rows] = x_ref[...]

    m = mod_ref[...][:, None, :]
    shift, scl, gres = m[:, :, 0:D_MODEL], m[:, :, D_MODEL:2 * D_MODEL], m[:, :, 2 * D_MODEL:]

    x3 = xo_ref[rows]
    layer = pl.ds(pl.program_id(0), 1)
    h3 = _rms(x3) * (gpre_ref[layer, :][None] * (1.0 + scl)) + shift
    hb = h3.reshape(n, D_MODEL).astype(BF16)

    o1, o2, o3, o4 = D_POOL, D_POOL + D_RET, D_POOL + 2 * D_RET, D_POOL + 3 * D_RET
    heads = [slice(hh * RET_DK, (hh + 1) * RET_DK) for hh in range(RET_HEADS)]
    u = jnp.dot(hb, win_ref[:, 0:o1], preferred_element_type=F32)
    zq = jnp.dot(hb, win_ref[:, o1:o2], preferred_element_type=F32).reshape(bt, dec_seq, D_RET)
    zk = jnp.dot(hb, win_ref[:, o2:o3], preferred_element_type=F32).reshape(bt, dec_seq, D_RET)
    zv = jnp.dot(hb, win_ref[:, o3:o4], preferred_element_type=F32).reshape(bt, dec_seq, D_RET)
    zg = jnp.dot(hb, win_ref[:, o4:], preferred_element_type=F32)

    pooled = []
    for g, w in enumerate(POOL_WINDOWS):
        lo, hi = g * POOL_CG, (g + 1) * POOL_CG
        usc_ref[g] = u[:, lo:hi]
        rows_g = [hist_ref[r, :, lo:hi] for r in range(pool_buf)]
        rows_g += [usc_ref[g, pl.ds(i, bt, stride=dec_seq), :] for i in range(dec_seq)]
        for i in range(dec_seq):
            cur = pool_buf + i
            acc = rows_g[cur]
            for k in range(1, w):
                acc = acc + rows_g[cur - k]
            usc_ref[g, pl.ds(i, bt, stride=dec_seq), :] = acc / float(w) - rows_g[cur]
        pooled.append(usc_ref[g].astype(BF16))
        for r in range(pool_buf):
            nb_ref[r, :, lo:hi] = rows_g[dec_seq + r]

    cs, sn = cs_ref[...][None], sn_ref[...][None]
    qr = [_rope(zq[:, :, sl], cs, sn) for sl in heads]
    kr = [_rope(zk[:, :, sl], cs, sn) * K_SCALE for sl in heads]
    sg_ref[...] = zg * jax.nn.sigmoid(zg)

    for g in range(len(POOL_WINDOWS)):
        lo, hi = g * POOL_CG, (g + 1) * POOL_CG
        mix_ref[:, lo:hi] = jnp.dot(pooled[g], wpool_ref[g], preferred_element_type=F32) * pscale_ref[layer, :][:, lo:hi]
    scores = [jnp.einsum("bid,bjd->bij", qr[hh], kr[hh], preferred_element_type=F32) * dmask_ref[hh][None]
              for hh in range(RET_HEADS)]
    kv, cross = [], []
    for hh, sl in enumerate(heads):
        kv.append(jnp.einsum("bjd,bje->bde", (kr[hh] * kdec_ref[:, sl][None]).astype(BF16),
                             zv[:, :, sl].astype(BF16), preferred_element_type=F32))
        cross.append(jnp.einsum("bid,bde->bie", qr[hh] * qdec_ref[:, sl][None], s0_ref[:, hh],
                                preferred_element_type=F32))
    intra = [jnp.einsum("bij,bje->bie", scores[hh], zv[:, :, sl], preferred_element_type=F32)
             for hh, sl in enumerate(heads)]
    for hh, sl in enumerate(heads):
        s_ref[:, hh] = gc_ref[0:1, sl][None] * s0_ref[:, hh] + kv[hh]
        mix_ref[:, D_POOL + hh * RET_DK:D_POOL + (hh + 1) * RET_DK] = _rms(intra[hh] + cross[hh]).reshape(n, RET_DK)

    mixb = (mix_ref[...] * sg_ref[...]).astype(BF16)
    y = jnp.dot(mixb, wo_ref[...], preferred_element_type=F32)
    xo_ref[rows] = x3 + (gres * gpost_ref[layer, :][None]) * _rms(y).reshape(bt, dec_seq, D_MODEL)


def _sample_layers(x, mod, g_pre, g_post, w_in, w_pool, pool_scale, w_o, tabs, state_ret, state_pool):
    b, dec_seq, d = x.shape
    depth, pool_buf, _, _ = state_pool.shape
    bt = SAMPLE_BT
    cs, sn, dmask, qdec, kdec, gc = tabs
    layer = lambda l, i: (l, 0, 0)
    kern = functools.partial(_sample_kernel, bt=bt, dec_seq=dec_seq)
    return pl.pallas_call(
        kern,
        grid=(depth, b // bt),
        in_specs=[
            pl.BlockSpec((bt, dec_seq, d), lambda l, i: (i, 0, 0)),
            pl.BlockSpec((None, bt, 3 * d), lambda l, i: (l, i, 0)),
            _const_spec(g_pre.shape),
            _const_spec(g_post.shape),
            pl.BlockSpec((None, d, D_IN), layer, pipeline_mode=pl.Buffered(1)),
            pl.BlockSpec((None, len(POOL_WINDOWS), POOL_CG, POOL_CG), lambda l, i: (l, 0, 0, 0)),
            _const_spec(pool_scale.shape),
            pl.BlockSpec((None, d, d), layer, pipeline_mode=pl.Buffered(1)),
            _const_spec(cs.shape), _const_spec(sn.shape),
            _const_spec(dmask.shape), _const_spec(qdec.shape), _const_spec(kdec.shape), _const_spec(gc.shape),
            pl.BlockSpec((None, bt, RET_HEADS, RET_DK, RET_DK), lambda l, i: (l, i, 0, 0, 0)),
            pl.BlockSpec((None, pool_buf, bt, D_POOL), lambda l, i: (l, 0, i, 0)),
        ],
        out_specs=[
            pl.BlockSpec((b, dec_seq, d), lambda l, i: (0, 0, 0)),
            pl.BlockSpec((None, bt, RET_HEADS, RET_DK, RET_DK), lambda l, i: (l, i, 0, 0, 0)),
            pl.BlockSpec((None, pool_buf, bt, D_POOL), lambda l, i: (l, 0, i, 0)),
        ],
        out_shape=[
            jax.ShapeDtypeStruct((b, dec_seq, d), F32),
            jax.ShapeDtypeStruct((depth, b, RET_HEADS, RET_DK, RET_DK), F32),
            jax.ShapeDtypeStruct((depth, pool_buf, b, D_POOL), F32),
        ],
        scratch_shapes=[
            pltpu.VMEM((len(POOL_WINDOWS), bt * dec_seq, POOL_CG), F32),
            pltpu.VMEM((bt * dec_seq, d), F32),
            pltpu.VMEM((bt * dec_seq, d), F32),
        ],
        compiler_params=pltpu.CompilerParams(
            dimension_semantics=("arbitrary", "arbitrary"), vmem_limit_bytes=VMEM_LIMIT),
        name="sample_layers",
    )(x, mod, g_pre, g_post, w_in, w_pool, pool_scale, w_o, cs, sn, dmask, qdec, kdec, gc, state_ret, state_pool)


def kernel(x_prompt, x_sample, c_prompt, c_sample, state_ret, state_pool, w_ada, b_ada,
           g_pre, g_post, w_in, w_pool, pool_scale, w_o):
    depth = w_in.shape[0]
    bp, seq, d = x_prompt.shape
    bs, dec_seq, _ = x_sample.shape
    assert d == D_MODEL and w_in.shape[2] == D_IN and state_pool.shape[2] == HIST - 1
    assert seq % PROMPT_TILE == 0 and bs % SAMPLE_BT == 0
    assert dec_seq == 8 and dec_seq % RET_CHUNK != 0

    (csp, snp, css, sns, dmp, qdp, kdp, gcp, dms, qds, kds, gcs, *stacked) = _make_tables(seq, dec_seq, PAST_LEN,
                                                                                      depth, bp)
    tabs_p = (csp, snp, dmp, qdp, kdp, gcp)
    tabs_s = (css, sns, dms, qds, kds, gcs)

    mod_s, mod_p, w_in_b, w_o_b, w_pool_b, w_fold_b = _prep(c_sample, c_prompt, w_ada, b_ada, w_in, w_o, w_pool)
    hs, ret_s, pool_s = _sample_layers(x_sample, mod_s, g_pre, g_post, w_in_b, w_pool_b, pool_scale, w_o_b,
                                       tabs_s, state_ret, jnp.transpose(state_pool, (0, 2, 1, 3)))
    pool_s = jnp.transpose(pool_s, (0, 2, 1, 3))
    hp = x_prompt
    for l in range(depth):
        hp, *stacked = _prompt_layer(l, hp, mod_p, g_pre, g_post, w_in_b, w_fold_b, pool_scale, w_o_b, tabs_p,
                                     stacked)
    ret_p, tails = stacked
    pool_p = jnp.transpose(_prompt_pool_state(tails, w_in_b, state_pool.shape[2]), (0, 2, 1, 3))
    return (hp, hs, ret_p, pool_p, ret_s, pool_s)
```
